```python
import math
import jax, jax.numpy as jnp
from jax import lax
import numpy as np

D_MODEL = 1024
BATCH = 4
SEQ = 4096
DEPTH = 2
DEC_BATCH = 32
DEC_SEQ = 64
PAST_LEN = 4096

CHUNK = 64
SSM_WIDTH = 512
SSM_GROUP = 16
SSM_GROUPS = SSM_WIDTH // SSM_GROUP
SSM_STATE = 64
ATTN_WIDTH = D_MODEL - SSM_WIDTH
HEAD_DIM = 64
N_HEADS = ATTN_WIDTH // (2 * HEAD_DIM)
IN_COLS = SSM_WIDTH + 3 * ATTN_WIDTH
N_EXPERTS = 32
TOP_K = 4
D_FF = D_MODEL
SWIGLU_ALPHA = 1.702
SWIGLU_LIMIT = 7.0
Q_BLOCK = 128
MOE_BLOCK = 128
EPS = 1e-6
NEG_INF = -1e30

kernel_name = "hymba_s5_diffattn_moe_stream"


def rms_norm(x, g):
    xf = x.astype(jnp.float32)
    y = xf * lax.rsqrt(jnp.mean(xf * xf, axis=-1, keepdims=True) + EPS)
    return (y * g.astype(jnp.float32)).astype(x.dtype)


def alibi_slopes():
    return jnp.asarray([2.0 ** (-8.0 * (h + 1) / N_HEADS) for h in range(N_HEADS)], jnp.float32)


def lambda_init(layer):
    return 0.8 - 0.6 * math.exp(-0.3 * layer)


def _complex_affine_combine(left, right):
    ar1, ai1, br1, bi1 = left
    ar2, ai2, br2, bi2 = right
    return (ar2 * ar1 - ai2 * ai1,
            ar2 * ai1 + ai2 * ar1,
            ar2 * br1 - ai2 * bi1 + br2,
            ar2 * bi1 + ai2 * br1 + bi2)


def s5_ssm(u, h0_re, h0_im, a_re, a_im, log_dt, b_re, b_im, c_re, c_im, d_skip):
    bt, sl, _ = u.shape
    ug = u.reshape(bt, sl, SSM_GROUPS, SSM_GROUP)
    dt = jnp.exp(log_dt)[:, None]
    za_re, za_im = dt * a_re, dt * a_im
    mag = jnp.exp(za_re)
    ab_re, ab_im = mag * jnp.cos(za_im), mag * jnp.sin(za_im)
    den = a_re * a_re + a_im * a_im
    n_re, n_im = ab_re - 1.0, ab_im
    f_re = (n_re * a_re + n_im * a_im) / den
    f_im = (n_im * a_re - n_re * a_im) / den
    bb_re = f_re[..., None] * b_re - f_im[..., None] * b_im
    bb_im = f_re[..., None] * b_im + f_im[..., None] * b_re
    bu_re = jnp.einsum('blgc,gnc->blgn', ug, bb_re)
    bu_im = jnp.einsum('blgc,gnc->blgn', ug, bb_im)
    bu_re = bu_re.at[:, 0].add(ab_re * h0_re - ab_im * h0_im)
    bu_im = bu_im.at[:, 0].add(ab_re * h0_im + ab_im * h0_re)
    a_el_re = jnp.broadcast_to(ab_re, bu_re.shape)
    a_el_im = jnp.broadcast_to(ab_im, bu_im.shape)
    _, _, h_re, h_im = lax.associative_scan(
        _complex_affine_combine, (a_el_re, a_el_im, bu_re, bu_im), axis=1)
    y = (jnp.einsum('blgn,gcn->blgc', h_re, c_re)
         - jnp.einsum('blgn,gcn->blgc', h_im, c_im))
    y = y.reshape(bt, sl, SSM_WIDTH) + d_skip * u
    return y, h_re[:, -1], h_im[:, -1]


def diff_attention_block(q, k, v, q_pos, k_pos, lam, slopes):
    q1, q2 = q[..., :HEAD_DIM], q[..., HEAD_DIM:]
    k1, k2 = k[..., :HEAD_DIM], k[..., HEAD_DIM:]
    scale = HEAD_DIM ** -0.5
    dist = jnp.abs(q_pos[:, None] - k_pos[None, :]).astype(jnp.float32)
    bias = -slopes[:, None, None] * dist[None]
    visible = (k_pos[None, :] // CHUNK) <= (q_pos[:, None] // CHUNK)

    def probs(qa, ka):
        s = jnp.einsum('bqhd,bkhd->bhqk', qa, ka).astype(jnp.float32) * scale + bias
        s = jnp.where(visible, s, NEG_INF)
        return jax.nn.softmax(s, axis=-1)

    attn = probs(q1, k1) - lam * probs(q2, k2)
    return jnp.einsum('bhqk,bkhd->bqhd', attn.astype(v.dtype), v)


def moe_ffn(x, w_router, b_router, w_gate_up, b_gate_up, w_down, b_down):
    lead = x.shape[:-1]
    xt = x.reshape(-1, D_MODEL)
    n_tok = xt.shape[0]
    logits = (xt @ w_router + b_router).astype(jnp.float32)
    top_val, top_idx = lax.top_k(logits, TOP_K)
    gates = jax.nn.softmax(top_val, axis=-1)
    n_assign = n_tok * TOP_K
    e_flat = top_idx.reshape(-1).astype(jnp.int32)
    tok_flat = jnp.arange(n_assign, dtype=jnp.int32) // TOP_K
    g_flat = gates.reshape(-1)
    order = jnp.argsort(e_flat)
    e_s, tok_s, g_s = e_flat[order], tok_flat[order], g_flat[order]
    counts = jnp.bincount(e_flat, length=N_EXPERTS).astype(jnp.int32)
    starts = jnp.cumsum(counts) - counts
    padded = (counts + MOE_BLOCK - 1) // MOE_BLOCK * MOE_BLOCK
    pends = jnp.cumsum(padded)
    pstarts = pends - padded
    rows = pstarts[e_s] + (jnp.arange(n_assign, dtype=jnp.int32) - starts[e_s])
    n_blocks = -(-n_assign // MOE_BLOCK) + N_EXPERTS
    row_tok = jnp.full((n_blocks * MOE_BLOCK,), n_tok, jnp.int32).at[rows].set(tok_s)
    row_gate = jnp.zeros((n_blocks * MOE_BLOCK,), jnp.float32).at[rows].set(g_s)
    block_start = jnp.arange(n_blocks, dtype=jnp.int32) * MOE_BLOCK
    block_expert = jnp.minimum(jnp.searchsorted(pends, block_start, side='right'),
                               N_EXPERTS - 1).astype(jnp.int32)
    x_pad = jnp.concatenate([xt, jnp.zeros((1, D_MODEL), xt.dtype)], axis=0)

    def expert_block(args):
        tok_b, gate_b, e = args
        xb = x_pad[tok_b]
        gu = xb @ w_gate_up[e] + b_gate_up[e]
        x_glu = jnp.minimum(gu[:, :D_FF], SWIGLU_LIMIT)
        x_lin = jnp.clip(gu[:, D_FF:], -SWIGLU_LIMIT, SWIGLU_LIMIT)
        hdn = x_glu * jax.nn.sigmoid(SWIGLU_ALPHA * x_glu) * (x_lin + 1.0)
        out = hdn @ w_down[e] + b_down[e]
        return out * gate_b[:, None].astype(out.dtype)

    out_blocks = lax.map(expert_block, (row_tok.reshape(n_blocks, MOE_BLOCK),
                                        row_gate.reshape(n_blocks, MOE_BLOCK),
                                        block_expert))
    y = jnp.zeros((n_tok + 1, D_MODEL), x.dtype).at[row_tok].add(
        out_blocks.reshape(-1, D_MODEL).astype(x.dtype))
    return y[:n_tok].reshape(lead + (D_MODEL,))


def hybrid_layer(x, p, layer, k_past, v_past, h0_re, h0_im):
    bt, sl, _ = x.shape
    h = rms_norm(x, p['g_mix'])
    proj = h @ p['w_in']
    u = proj[..., :SSM_WIDTH]
    q = proj[..., SSM_WIDTH:SSM_WIDTH + ATTN_WIDTH]
    k = proj[..., SSM_WIDTH + ATTN_WIDTH:SSM_WIDTH + 2 * ATTN_WIDTH]
    v = proj[..., SSM_WIDTH + 2 * ATTN_WIDTH:]

    y_s, hT_re, hT_im = s5_ssm(u, h0_re, h0_im, p['a_re'], p['a_im'], p['log_dt'],
                               p['b_re'], p['b_im'], p['c_re'], p['c_im'], p['d'])
    y_s = jax.nn.gelu(y_s)
    y_s = y_s * jax.nn.sigmoid(y_s @ p['w_glu'])
    y_s = rms_norm(y_s, p['g_ssm_out'])

    q = rms_norm(q.reshape(bt, sl, N_HEADS, 2, HEAD_DIM), p['g_q']).reshape(bt, sl, N_HEADS, 2 * HEAD_DIM)
    k = rms_norm(k.reshape(bt, sl, N_HEADS, 2, HEAD_DIM), p['g_k']).reshape(bt, sl, N_HEADS, 2 * HEAD_DIM)
    v = v.reshape(bt, sl, N_HEADS, 2 * HEAD_DIM)
    lam_init = lambda_init(layer)
    f32 = jnp.float32
    lam = (jnp.exp(jnp.sum(p['lq1'].astype(f32) * p['lk1'].astype(f32)))
           - jnp.exp(jnp.sum(p['lq2'].astype(f32) * p['lk2'].astype(f32))) + lam_init)
    slopes = alibi_slopes()
    if k_past is None:
        pos = jnp.arange(sl, dtype=jnp.int32)
        nqb = sl // Q_BLOCK
        qb = q.reshape(bt, nqb, Q_BLOCK, N_HEADS, 2 * HEAD_DIM).swapaxes(0, 1)
        pb = pos.reshape(nqb, Q_BLOCK)
        o = lax.map(lambda qp: diff_attention_block(qp[0], k, v, qp[1], pos, lam, slopes), (qb, pb))
        o = o.swapaxes(0, 1).reshape(bt, sl, N_HEADS, 2 * HEAD_DIM)
    else:
        past = k_past.shape[1]
        k_all = jnp.concatenate([k_past, k], axis=1)
        v_all = jnp.concatenate([v_past, v], axis=1)
        k_pos = jnp.arange(past + sl, dtype=jnp.int32)
        q_pos = past + jnp.arange(sl, dtype=jnp.int32)
        o = diff_attention_block(q, k_all, v_all, q_pos, k_pos, lam, slopes)
    o = rms_norm(o, p['g_subln']) * (1.0 - lam_init)
    y_a = o.reshape(bt, sl, ATTN_WIDTH)

    x = x + jnp.concatenate([y_s, y_a], axis=-1) @ p['w_out']
    x = x + moe_ffn(rms_norm(x, p['g_ffn']), p['w_router'], p['b_router'],
                    p['w_gate_up'], p['b_gate_up'], p['w_down'], p['b_down'])
    return x, k, v, hT_re, hT_im


def setup_inputs(seed: int = 0) -> dict:
    key = jax.random.key(seed)
    ks = jax.random.split(key, 34)
    nrm = lambda k, shape, s: jax.random.normal(k, shape, jnp.float32) * s
    G, N, C = SSM_GROUPS, SSM_STATE, SSM_GROUP
    return {
        "x_prompt": nrm(ks[0], (BATCH, SEQ, D_MODEL), 1.0),
        "x_sample": nrm(ks[1], (DEC_BATCH, DEC_SEQ, D_MODEL), 1.0),
        "cache_k": nrm(ks[2], (DEPTH, DEC_BATCH, PAST_LEN, N_HEADS, 2 * HEAD_DIM), 1.0),
        "cache_v": nrm(ks[3], (DEPTH, DEC_BATCH, PAST_LEN, N_HEADS, 2 * HEAD_DIM), 1.0),
        "state_ssm_re": nrm(ks[4], (DEPTH, DEC_BATCH, G, N), 0.5),
        "state_ssm_im": nrm(ks[5], (DEPTH, DEC_BATCH, G, N), 0.5),
        "g_mix": 1.0 + nrm(ks[6], (DEPTH, D_MODEL), 0.02),
        "w_in": nrm(ks[7], (DEPTH, D_MODEL, IN_COLS), D_MODEL ** -0.5),
        "ssm_a_re": -0.5 + nrm(ks[8], (DEPTH, G, N), 0.01),
        "ssm_a_im": jnp.pi * jnp.arange(N, dtype=jnp.float32) + nrm(ks[9], (DEPTH, G, N), 0.01),
        "ssm_log_dt": jax.random.uniform(ks[10], (DEPTH, G), jnp.float32,
                                         minval=math.log(0.001), maxval=math.log(0.1)),
        "ssm_b_re": nrm(ks[11], (DEPTH, G, N, C), (2 * C) ** -0.5),
        "ssm_b_im": nrm(ks[12], (DEPTH, G, N, C), (2 * C) ** -0.5),
        "ssm_c_re": nrm(ks[13], (DEPTH, G, C, N), (2 * N) ** -0.5),
        "ssm_c_im": nrm(ks[14], (DEPTH, G, C, N), (2 * N) ** -0.5),
        "ssm_d": nrm(ks[15], (DEPTH, SSM_WIDTH), 1.0),
        "w_glu": nrm(ks[16], (DEPTH, SSM_WIDTH, SSM_WIDTH), SSM_WIDTH ** -0.5),
        "g_ssm_out": 1.0 + nrm(ks[17], (DEPTH, SSM_WIDTH), 0.02),
        "g_q": 1.0 + nrm(ks[18], (DEPTH, HEAD_DIM), 0.02),
        "g_k": 1.0 + nrm(ks[19], (DEPTH, HEAD_DIM), 0.02),
        "lambda_q1": nrm(ks[20], (DEPTH, HEAD_DIM), 0.1),
        "lambda_k1": nrm(ks[21], (DEPTH, HEAD_DIM), 0.1),
        "lambda_q2": nrm(ks[22], (DEPTH, HEAD_DIM), 0.1),
        "lambda_k2": nrm(ks[23], (DEPTH, HEAD_DIM), 0.1),
        "g_subln": 1.0 + nrm(ks[24], (DEPTH, 2 * HEAD_DIM), 0.02),
        "w_out": nrm(ks[25], (DEPTH, D_MODEL, D_MODEL), D_MODEL ** -0.5),
        "g_ffn": 1.0 + nrm(ks[26], (DEPTH, D_MODEL), 0.02),
        "w_router": nrm(ks[27], (DEPTH, D_MODEL, N_EXPERTS), D_MODEL ** -0.5),
        "b_router": nrm(ks[28], (DEPTH, N_EXPERTS), 0.01),
        "w_gate_up": nrm(ks[29], (DEPTH, N_EXPERTS, D_MODEL, 2 * D_FF), D_MODEL ** -0.5),
        "b_gate_up": nrm(ks[30], (DEPTH, N_EXPERTS, 2 * D_FF), 0.01),
        "w_down": nrm(ks[31], (DEPTH, N_EXPERTS, D_FF, D_MODEL), D_FF ** -0.5),
        "b_down": nrm(ks[32], (DEPTH, N_EXPERTS, D_MODEL), 0.01),
    }


def reference(x_prompt, x_sample, cache_k, cache_v, state_ssm_re, state_ssm_im,
              g_mix, w_in, ssm_a_re, ssm_a_im, ssm_log_dt, ssm_b_re, ssm_b_im,
              ssm_c_re, ssm_c_im, ssm_d, w_glu, g_ssm_out, g_q, g_k,
              lambda_q1, lambda_k1, lambda_q2, lambda_k2, g_subln, w_out,
              g_ffn, w_router, b_router, w_gate_up, b_gate_up, w_down, b_down):
    yp, ys = x_prompt, x_sample
    kp, vp, srp, sip = [], [], [], []
    kss, vss, srs, sis = [], [], [], []
    for l in range(DEPTH):
        p = dict(g_mix=g_mix[l], w_in=w_in[l], a_re=ssm_a_re[l], a_im=ssm_a_im[l],
                 log_dt=ssm_log_dt[l], b_re=ssm_b_re[l], b_im=ssm_b_im[l],
                 c_re=ssm_c_re[l], c_im=ssm_c_im[l], d=ssm_d[l], w_glu=w_glu[l],
                 g_ssm_out=g_ssm_out[l], g_q=g_q[l], g_k=g_k[l],
                 lq1=lambda_q1[l], lk1=lambda_k1[l], lq2=lambda_q2[l], lk2=lambda_k2[l],
                 g_subln=g_subln[l], w_out=w_out[l], g_ffn=g_ffn[l],
                 w_router=w_router[l], b_router=b_router[l], w_gate_up=w_gate_up[l],
                 b_gate_up=b_gate_up[l], w_down=w_down[l], b_down=b_down[l])
        h0 = jnp.zeros((x_prompt.shape[0], SSM_GROUPS, SSM_STATE), x_prompt.dtype)
        yp, k_new, v_new, h_re, h_im = hybrid_layer(yp, p, l, None, None, h0, h0)
        kp.append(k_new); vp.append(v_new); srp.append(h_re); sip.append(h_im)
        ys, k_new, v_new, h_re, h_im = hybrid_layer(ys, p, l, cache_k[l], cache_v[l],
                                                    state_ssm_re[l], state_ssm_im[l])
        kss.append(k_new); vss.append(v_new); srs.append(h_re); sis.append(h_im)
    return (yp, ys, jnp.stack(kp), jnp.stack(vp), jnp.stack(srp), jnp.stack(sip),
            jnp.stack(kss), jnp.stack(vss), jnp.stack(srs), jnp.stack(sis))
```

```python
import functools
import math

import jax
import jax.numpy as jnp
from jax import lax
from jax.experimental import pallas as pl
from jax.experimental.pallas import tpu as pltpu

F32 = jnp.float32
BF16 = jnp.bfloat16

CHUNK = 64
HEAD_DIM = 64
HEAD_W = 2 * HEAD_DIM
SSM_GROUP = 16
SSM_STATE = 64
SSM_QG = 8
SSM_QW = SSM_QG * SSM_STATE
TOP_K = 4
SWIGLU_ALPHA = 1.702
SWIGLU_LIMIT = 7.0
EPS = 1e-6
NEG_INF = -1e30

TOK_TILE = 512
SSM_ROWS = 8 * CHUNK
ATT_TQ = 512
ATT_TKC = 2048
MOE_TILE = 256
VMEM_LIMIT = 56 * 1024 * 1024


def _cparams(sem):
    return pltpu.CompilerParams(dimension_semantics=sem, vmem_limit_bytes=VMEM_LIMIT)


def _dot(a, b):
    return jnp.dot(a, b, preferred_element_type=F32)


def _dot_t(a, b):
    return lax.dot_general(a, b, (((1,), (1,)), ((), ())), preferred_element_type=F32)


def _split_bf16(x):
    hi = x.astype(BF16)
    lo = (x - hi.astype(F32)).astype(BF16)
    return hi, lo


def _inproj_kernel(x_ref, g_ref, w_ref, gq_ref, gk_ref, seg_ref,
                   u_ref, q_ref, k_ref, v_ref, kb_ref, vb_ref):
    x = x_ref[...]
    ms = jnp.mean(x * x, axis=-1, keepdims=True)
    h = (x * lax.rsqrt(ms + EPS) * g_ref[...]).astype(BF16)
    proj = _dot(h, w_ref[...])
    w = u_ref.shape[-1]
    seg = seg_ref[...]

    def head_norm(z, g):
        hi, lo = _split_bf16(z * z)
        ms_ = _dot(hi, seg) + _dot(lo, seg)
        return z * lax.rsqrt(ms_ + EPS) * g

    u_ref[...] = proj[:, :w]
    qn = head_norm(proj[:, w:2 * w], gq_ref[...])
    q_ref[...] = (qn * (HEAD_DIM ** -0.5)).astype(BF16)
    kn = head_norm(proj[:, 2 * w:3 * w], gk_ref[...])
    k_ref[...] = kn
    kb_ref[...] = kn.astype(BF16)
    vv = proj[:, 3 * w:]
    v_ref[...] = vv
    vb_ref[...] = vv.astype(BF16)


def _inproj(x, g_mix, w_in_bf, g_q, g_k):
    t, d = x.shape
    aw = w_in_bf.shape[1] // 4
    nrep = aw // HEAD_DIM
    gq = jnp.tile(g_q.astype(F32), nrep)[None]
    gk = jnp.tile(g_k.astype(F32), nrep)[None]
    ids = jnp.arange(aw) // HEAD_DIM
    seg = jnp.where(ids[:, None] == ids[None, :], 1.0 / HEAD_DIM, 0.0).astype(BF16)
    tm = TOK_TILE
    row = lambda i: (i, 0)
    fixed = lambda i: (0, 0)
    outs = [jax.ShapeDtypeStruct((t, aw), dt) for dt in (F32, BF16, F32, F32, BF16, BF16)]
    return pl.pallas_call(
        _inproj_kernel,
        grid=(t // tm,),
        in_specs=[pl.BlockSpec((tm, d), row), pl.BlockSpec((1, d), fixed),
                  pl.BlockSpec(w_in_bf.shape, fixed), pl.BlockSpec((1, aw), fixed),
                  pl.BlockSpec((1, aw), fixed), pl.BlockSpec((aw, aw), fixed)],
        out_specs=[pl.BlockSpec((tm, aw), row)] * 6,
        out_shape=outs,
        compiler_params=_cparams(("parallel",)),
        name="inproj",
    )(x, g_mix[None].astype(F32), w_in_bf, gq, gk, seg)


def _ssm_kernel(u_ref, wb_ref, wc_ref, ab_ref, a64_ref, pw_ref, d_ref, h0_ref,
                y_ref, fin_ref, uperm_ref, st_ref, carry_ref, *, n_prompt_sc, sc_per_seq):
    sc = pl.program_id(0)
    qb = pl.program_id(1)
    nsteps = CHUNK
    w = SSM_QW

    for t in range(nsteps):
        uperm_ref[t * 8:(t + 1) * 8, :] = u_ref[pl.ds(t, 8, stride=nsteps), :]
    up = uperm_ref[...]
    st_ref[...] = _dot(up.astype(BF16), wb_ref[0])

    ab = ab_ref[0]
    ar = jnp.broadcast_to(ab[:, :w], (8, w))
    ai = jnp.broadcast_to(ab[:, w:], (8, w))

    def scan_step(t, carry):
        hr, hi = carry
        r0 = pl.multiple_of(t * 8, 8)
        br = st_ref[pl.ds(r0, 8), :w]
        bi = st_ref[pl.ds(r0, 8), w:]
        nr = ar * hr - ai * hi + br
        ni = ar * hi + ai * hr + bi
        st_ref[pl.ds(r0, 8), :w] = nr
        st_ref[pl.ds(r0, 8), w:] = ni
        return nr, ni

    zero = jnp.zeros((8, w), F32)
    er, ei = lax.fori_loop(0, nsteps, scan_step, (zero, zero))

    a64 = a64_ref[0]
    a64r = a64[:, :w]
    a64i = a64[:, w:]
    is_sample = sc >= n_prompt_sc

    @pl.when(jnp.logical_or(is_sample, sc % sc_per_seq == 0))
    def _():
        carry_ref[qb] = jnp.zeros(carry_ref.shape[1:], F32)

    cin = carry_ref[qb]
    cr = cin[:, :w]
    ci = cin[:, w:]
    rows = lax.broadcasted_iota(jnp.int32, (8, w), 0)
    sr = jnp.zeros((8, w), F32)
    si = jnp.zeros((8, w), F32)
    for j in range(8):
        sr = jnp.where(rows == j, cr, sr)
        si = jnp.where(rows == j, ci, si)
        ejr = er[j:j + 1]
        eji = ei[j:j + 1]
        cr, ci = a64r * cr - a64i * ci + ejr, a64r * ci + a64i * cr + eji
    carry_ref[qb] = jnp.concatenate([cr, ci], axis=1)
    h0 = h0_ref[0, 0]
    given = (jnp.zeros((8, w), jnp.int32) + is_sample.astype(jnp.int32)) > 0
    sr = jnp.where(given, h0[:, :w], sr)
    si = jnp.where(given, h0[:, w:], si)
    fr = a64r * sr - a64i * si + er
    fi = a64r * si + a64i * sr + ei
    fin_ref[0, 0] = jnp.concatenate([fr, fi], axis=1)

    def fix_step(t, _):
        r0 = pl.multiple_of(t * 8, 8)
        p = pw_ref[0, pl.ds(t, 1), :]
        pr = p[:, :w]
        pi = p[:, w:]
        st_ref[pl.ds(r0, 8), :w] = st_ref[pl.ds(r0, 8), :w] + (pr * sr - pi * si)
        st_ref[pl.ds(r0, 8), w:] = st_ref[pl.ds(r0, 8), w:] + (pr * si + pi * sr)
        return 0

    lax.fori_loop(0, nsteps, fix_step, 0)

    y = _dot(st_ref[...].astype(BF16), wc_ref[0]) + up * d_ref[0]
    for t in range(nsteps):
        y_ref[pl.ds(t, 8, stride=nsteps), :] = y[t * 8:(t + 1) * 8, :]


def _ssm_tables(a_re, a_im, log_dt, b_re, b_im, c_re, c_im, d_skip):
    g, n = a_re.shape
    c = b_re.shape[-1]
    nq = g // SSM_QG
    dt = jnp.exp(log_dt)[:, None]
    za_re, za_im = dt * a_re, dt * a_im
    mag = jnp.exp(za_re)
    ab_re, ab_im = mag * jnp.cos(za_im), mag * jnp.sin(za_im)
    den = a_re * a_re + a_im * a_im
    n_re, n_im = ab_re - 1.0, ab_im
    f_re = (n_re * a_re + n_im * a_im) / den
    f_im = (n_im * a_re - n_re * a_im) / den
    bb_re = f_re[..., None] * b_re - f_im[..., None] * b_im
    bb_im = f_re[..., None] * b_im + f_im[..., None] * b_re
    eye = jnp.eye(SSM_QG, dtype=F32)

    def in_w(bb):
        bq = bb.reshape(nq, SSM_QG, n, c)
        return jnp.einsum('qgnc,gh->qgchn', bq, eye).reshape(nq, SSM_QG * c, SSM_QG * n)

    def out_w(cc):
        cq = cc.reshape(nq, SSM_QG, c, n)
        return jnp.einsum('qgcn,gh->qgnhc', cq, eye).reshape(nq, SSM_QG * n, SSM_QG * c)

    wb = jnp.concatenate([in_w(bb_re), in_w(bb_im)], axis=2).astype(BF16)
    wc = jnp.concatenate([out_w(c_re), out_w(-c_im)], axis=1).astype(BF16)

    def lanes(z):
        return jnp.moveaxis(z.reshape(z.shape[:-2] + (nq, SSM_QG * n)), -2, 0)

    def power(k):
        m = jnp.exp(k * za_re)
        return m * jnp.cos(k * za_im), m * jnp.sin(k * za_im)

    ab = jnp.concatenate([lanes(ab_re), lanes(ab_im)], axis=-1)[:, None]
    p64 = power(float(CHUNK))
    a64 = jnp.concatenate([lanes(p64[0]), lanes(p64[1])], axis=-1)[:, None]
    ks = jnp.arange(1, CHUNK + 1, dtype=F32)[:, None, None]
    pk = power(ks)
    pw = jnp.concatenate([lanes(pk[0]), lanes(pk[1])], axis=-1)
    dq = d_skip.reshape(nq, 1, SSM_QG * c).astype(F32)
    return wb, wc, ab, a64, pw, dq


def _ssm(u, tables, h0_all, n_prompt_sc, sc_per_seq):
    wb, wc, ab, a64, pw, dq = tables
    t, cw = u.shape
    nq = wb.shape[0]
    n_sc = t // SSM_ROWS
    sw = 2 * SSM_QW
    kern = functools.partial(_ssm_kernel, n_prompt_sc=n_prompt_sc, sc_per_seq=sc_per_seq)
    per_q = lambda s, q: (q, 0, 0)
    return pl.pallas_call(
        kern,
        grid=(n_sc, nq),
        in_specs=[pl.BlockSpec((SSM_ROWS, 128), lambda s, q: (s, q)),
                  pl.BlockSpec((1,) + wb.shape[1:], per_q), pl.BlockSpec((1,) + wc.shape[1:], per_q),
                  pl.BlockSpec((1, 1, sw), per_q), pl.BlockSpec((1, 1, sw), per_q),
                  pl.BlockSpec((1, CHUNK, sw), per_q), pl.BlockSpec((1, 1, 128), per_q),
                  pl.BlockSpec((1, 1, 8, sw), lambda s, q: (s, q, 0, 0))],
        out_specs=[pl.BlockSpec((SSM_ROWS, 128), lambda s, q: (s, q)),
                   pl.BlockSpec((1, 1, 8, sw), lambda s, q: (s, q, 0, 0))],
        out_shape=[jax.ShapeDtypeStruct((t, cw), F32),
                   jax.ShapeDtypeStruct((n_sc, nq, 8, sw), F32)],
        scratch_shapes=[pltpu.VMEM((SSM_ROWS, 128), F32), pltpu.VMEM((SSM_ROWS, sw), F32),
                        pltpu.VMEM((nq, 1, sw), F32)],
        compiler_params=_cparams(("arbitrary", "arbitrary")),
        name="ssm",
    )(u, wb, wc, ab, a64, pw, dq, h0_all)


def _stack_q(q):
    lane = lax.broadcasted_iota(jnp.int32, q.shape, 1)
    zero = jnp.zeros_like(q)
    return jnp.concatenate([jnp.where(lane < HEAD_DIM, q, zero), jnp.where(lane >= HEAD_DIM, q, zero)], axis=0)


def _online_update(s, v, m_ref, l_ref, acc_ref):
    m_old = m_ref[...]
    m_new = jnp.maximum(m_old, jnp.max(s, axis=-1, keepdims=True))
    alpha = jnp.exp(m_old - m_new)
    p = jnp.exp(s - m_new)
    l_ref[...] = alpha * l_ref[...] + jnp.sum(p, axis=-1, keepdims=True)
    acc_ref[...] = alpha * acc_ref[...] + _dot(p.astype(BF16), v)
    m_ref[...] = m_new


def _finish_head(m_ref, l_ref, acc_ref, lam, g, out_scale, tq):
    acc = acc_ref[...]
    l = l_ref[...]
    o = acc[:tq] / l[:tq] - lam * (acc[tq:] / l[tq:])
    ms = jnp.mean(o * o, axis=-1, keepdims=True)
    return o * lax.rsqrt(ms + EPS) * g * out_scale


def _attn_prompt_kernel(slope_ref, lam_ref, q_ref, k_ref, v_ref, g_ref, o_ref,
                        m_ref, l_ref, acc_ref, *, out_scale):
    h = pl.program_id(1)
    i = pl.program_id(2)
    tq = q_ref.shape[0]
    tk = tq
    slope = slope_ref[h]
    qq = _stack_q(q_ref[...])
    m_ref[...] = jnp.full(m_ref.shape, NEG_INF, F32)
    l_ref[...] = jnp.zeros(l_ref.shape, F32)
    acc_ref[...] = jnp.zeros(acc_ref.shape, F32)

    def past(j, _):
        k0 = pl.multiple_of(j * tk, tk)
        k = k_ref[pl.ds(k0, tk), :]
        v = v_ref[pl.ds(k0, tk), :]
        kpos = (k0 + lax.broadcasted_iota(jnp.int32, (1, tk), 1)).astype(F32)
        s = _dot_t(qq, k) + slope * kpos
        _online_update(s, v, m_ref, l_ref, acc_ref)
        return 0

    lax.fori_loop(0, i, past, 0)

    k0 = pl.multiple_of(i * tk, tk)
    k = k_ref[pl.ds(k0, tk), :]
    v = v_ref[pl.ds(k0, tk), :]
    r = lax.broadcasted_iota(jnp.int32, (2 * tq, tk), 0)
    r = jnp.where(r >= tq, r - tq, r)
    c = lax.broadcasted_iota(jnp.int32, (2 * tq, tk), 1)
    bias = slope * (k0 + r - jnp.abs(r - c)).astype(F32)
    visible = (c >> 6) <= (r >> 6)
    s = jnp.where(visible, _dot_t(qq, k) + bias, NEG_INF)
    _online_update(s, v, m_ref, l_ref, acc_ref)
    o_ref[...] = _finish_head(m_ref, l_ref, acc_ref, lam_ref[0], g_ref[...], out_scale, tq).astype(o_ref.dtype)


def _attn_prompt(q, kb, vb, slopes, lam, g_subln, out_scale, batch, seq):
    n_heads = q.shape[1] // HEAD_W
    tq = ATT_TQ
    nq = seq // tq
    kern = functools.partial(_attn_prompt_kernel, out_scale=out_scale)
    smem = pl.BlockSpec(memory_space=pltpu.SMEM)
    return pl.pallas_call(
        kern,
        grid=(batch, n_heads, nq),
        in_specs=[smem, smem,
                  pl.BlockSpec((tq, HEAD_W), lambda b, h, i: (b * nq + i, h)),
                  pl.BlockSpec((seq, HEAD_W), lambda b, h, i: (b, h)),
                  pl.BlockSpec((seq, HEAD_W), lambda b, h, i: (b, h)),
                  pl.BlockSpec((1, HEAD_W), lambda b, h, i: (0, 0))],
        out_specs=pl.BlockSpec((tq, HEAD_W), lambda b, h, i: (b * nq + i, h)),
        out_shape=jax.ShapeDtypeStruct((batch * seq, q.shape[1]), BF16),
        scratch_shapes=[pltpu.VMEM((2 * tq, 1), F32), pltpu.VMEM((2 * tq, 1), F32),
                        pltpu.VMEM((2 * tq, HEAD_W), F32)],
        compiler_params=_cparams(("parallel", "parallel", "arbitrary")),
        name="attn_prompt",
    )(slopes, lam, q, kb, vb, g_subln[None].astype(F32))


def _attn_sample_kernel(slope_ref, lam_ref, q_ref, kn_ref, vn_ref, kc_ref, vc_ref, g_ref, o_ref,
                        m_ref, l_ref, acc_ref, *, out_scale, past_len):
    j = pl.program_id(1)
    nj = pl.num_programs(1)
    tq = q_ref.shape[0]
    tk = kc_ref.shape[1]
    n_heads = q_ref.shape[1] // HEAD_W

    @pl.when(j == 0)
    def _():
        m_ref[...] = jnp.full(m_ref.shape, NEG_INF, F32)
        l_ref[...] = jnp.zeros(l_ref.shape, F32)
        acc_ref[...] = jnp.zeros(acc_ref.shape, F32)

    kpos = (j * tk + lax.broadcasted_iota(jnp.int32, (1, tk), 1)).astype(F32)
    for h in range(n_heads):
        cols = slice(h * HEAD_W, (h + 1) * HEAD_W)
        qq = _stack_q(q_ref[:, cols])
        k = kc_ref[0, :, cols].astype(BF16)
        v = vc_ref[0, :, cols].astype(BF16)
        s = _dot_t(qq, k) + slope_ref[h] * kpos
        _online_update(s, v, m_ref.at[h], l_ref.at[h], acc_ref.at[h])

    @pl.when(j == nj - 1)
    def _():
        r = lax.broadcasted_iota(jnp.int32, (2 * tq, tq), 0)
        r = jnp.where(r >= tq, r - tq, r)
        c = lax.broadcasted_iota(jnp.int32, (2 * tq, tq), 1)
        rel = (past_len + r - jnp.abs(r - c)).astype(F32)
        for h in range(n_heads):
            cols = slice(h * HEAD_W, (h + 1) * HEAD_W)
            qq = _stack_q(q_ref[:, cols])
            s = _dot_t(qq, kn_ref[:, cols]) + slope_ref[h] * rel
            _online_update(s, vn_ref[:, cols], m_ref.at[h], l_ref.at[h], acc_ref.at[h])
            o_ref[:, cols] = _finish_head(m_ref.at[h], l_ref.at[h], acc_ref.at[h], lam_ref[0],
                                          g_ref[...], out_scale, tq).astype(o_ref.dtype)


def _attn_sample(q, kb, vb, cache_k, cache_v, slopes, lam, g_subln, out_scale, row0):
    n_streams, past_len, aw = cache_k.shape
    tq = CHUNK
    tk = min(ATT_TKC, past_len)
    n_heads = aw // HEAD_W
    blk0 = row0 // tq
    kern = functools.partial(_attn_sample_kernel, out_scale=out_scale, past_len=past_len)
    smem = pl.BlockSpec(memory_space=pltpu.SMEM)
    new = pl.BlockSpec((tq, aw), lambda s, j: (blk0 + s, 0))
    return pl.pallas_call(
        kern,
        grid=(n_streams, past_len // tk),
        in_specs=[smem, smem, new, new, new,
                  pl.BlockSpec((1, tk, aw), lambda s, j: (s, j, 0)),
                  pl.BlockSpec((1, tk, aw), lambda s, j: (s, j, 0)),
                  pl.BlockSpec((1, HEAD_W), lambda s, j: (0, 0))],
        out_specs=pl.BlockSpec((tq, aw), lambda s, j: (s, 0)),
        out_shape=jax.ShapeDtypeStruct((n_streams * tq, aw), BF16),
        scratch_shapes=[pltpu.VMEM((n_heads, 2 * tq, 1), F32), pltpu.VMEM((n_heads, 2 * tq, 1), F32),
                        pltpu.VMEM((n_heads, 2 * tq, HEAD_W), F32)],
        compiler_params=_cparams(("parallel", "arbitrary")),
        name="attn_sample",
    )(slopes, lam, q, kb, vb, cache_k, cache_v, g_subln[None].astype(F32))


def _outproj_kernel(ys_ref, oa_ref, x_ref, wglu_ref, gs_ref, wtop_ref, wbot_ref, gf_ref,
                    wrh_ref, wrl_ref, br_ref, x1_ref, h2_ref, lg_ref):
    y = ys_ref[...]
    y = 0.5 * y * (1.0 + jnp.tanh(math.sqrt(2.0 / math.pi) * (y + 0.044715 * (y * y * y))))
    z = _dot(y.astype(BF16), wglu_ref[...])
    y = y * (1.0 / (1.0 + jnp.exp(-z)))
    ms = jnp.mean(y * y, axis=-1, keepdims=True)
    y = y * lax.rsqrt(ms + EPS) * gs_ref[...]
    x1 = x_ref[...] + _dot(y.astype(BF16), wtop_ref[...]) + _dot(oa_ref[...], wbot_ref[...])
    x1_ref[...] = x1
    ms = jnp.mean(x1 * x1, axis=-1, keepdims=True)
    h2 = x1 * lax.rsqrt(ms + EPS) * gf_ref[...]
    h2_ref[...] = h2.astype(BF16)
    hi, lo = _split_bf16(h2)
    wrh = wrh_ref[...]
    lg_ref[...] = _dot(hi, wrh) + _dot(lo, wrh) + _dot(hi, wrl_ref[...]) + br_ref[...]


def _outproj(ys, oa, x, w_glu, g_ssm, w_out, g_ffn, w_router, b_router):
    t, d = x.shape
    sw = ys.shape[1]
    n_exp = w_router.shape[1]
    lw = max(128, n_exp)
    wr = jnp.zeros((d, lw), F32).at[:, :n_exp].set(w_router)
    wrh, wrl = _split_bf16(wr)
    br = jnp.zeros((1, lw), F32).at[0, :n_exp].set(b_router)
    w_out_bf = w_out.astype(BF16)
    tm = TOK_TILE
    row = lambda i: (i, 0)
    fixed = lambda i: (0, 0)
    return pl.pallas_call(
        _outproj_kernel,
        grid=(t // tm,),
        in_specs=[pl.BlockSpec((tm, sw), row), pl.BlockSpec((tm, d - sw), row), pl.BlockSpec((tm, d), row),
                  pl.BlockSpec((sw, sw), fixed), pl.BlockSpec((1, sw), fixed),
                  pl.BlockSpec((sw, d), fixed), pl.BlockSpec((d - sw, d), fixed), pl.BlockSpec((1, d), fixed),
                  pl.BlockSpec((d, lw), fixed), pl.BlockSpec((d, lw), fixed), pl.BlockSpec((1, lw), fixed)],
        out_specs=[pl.BlockSpec((tm, d), row), pl.BlockSpec((tm, d), row), pl.BlockSpec((tm, lw), row)],
        out_shape=[jax.ShapeDtypeStruct((t, d), F32), jax.ShapeDtypeStruct((t, d), BF16),
                   jax.ShapeDtypeStruct((t, lw), F32)],
        compiler_params=_cparams(("parallel",)),
        name="outproj",
    )(ys, oa, x, w_glu.astype(BF16), g_ssm[None].astype(F32), w_out_bf[:sw], w_out_bf[sw:],
      g_ffn[None].astype(F32), wrh, wrl, br)


def _moe_kernel(te_ref, nt_ref, xs_ref, gate_ref, wgu_ref, bgu_ref, wd_ref, bd_ref, o_ref,
                wgu_bf, wd_bf):
    i = pl.program_id(0)
    used = i < nt_ref[0]
    prev = te_ref[jnp.maximum(i - 1, 0)]
    new_expert = jnp.logical_or(i == 0, te_ref[i] != prev)

    @pl.when(jnp.logical_and(used, new_expert))
    def _():
        wgu_bf[...] = wgu_ref[0].astype(BF16)
        wd_bf[...] = wd_ref[0].astype(BF16)

    @pl.when(used)
    def _():
        dff = wd_bf.shape[0]
        gu = _dot(xs_ref[...], wgu_bf[...]) + bgu_ref[0]
        x_glu = jnp.minimum(gu[:, :dff], SWIGLU_LIMIT)
        x_lin = jnp.clip(gu[:, dff:], -SWIGLU_LIMIT, SWIGLU_LIMIT)
        hdn = x_glu * (1.0 / (1.0 + jnp.exp(-SWIGLU_ALPHA * x_glu))) * (x_lin + 1.0)
        out = _dot(hdn.astype(BF16), wd_bf[...]) + bd_ref[0]
        o_ref[...] = out * gate_ref[...]

    @pl.when(jnp.logical_not(used))
    def _():
        o_ref[...] = jnp.zeros(o_ref.shape, o_ref.dtype)


def _moe_rows(xs, row_gate, tile_expert, n_tiles_used, w_gate_up, b_gate_up, w_down, b_down):
    n_rows, d = xs.shape
    n_exp, _, dgu = w_gate_up.shape
    dff = w_down.shape[1]
    tm = MOE_TILE
    by_e = lambda i, te, nt: (te[i], 0, 0)
    row = lambda i, te, nt: (i, 0)
    return pl.pallas_call(
        _moe_kernel,
        grid_spec=pltpu.PrefetchScalarGridSpec(
            num_scalar_prefetch=2,
            grid=(n_rows // tm,),
            in_specs=[pl.BlockSpec((tm, d), row), pl.BlockSpec((tm, 1), row),
                      pl.BlockSpec((1, d, dgu), by_e), pl.BlockSpec((1, 1, dgu), by_e),
                      pl.BlockSpec((1, dff, d), by_e), pl.BlockSpec((1, 1, d), by_e)],
            out_specs=pl.BlockSpec((tm, d), row),
            scratch_shapes=[pltpu.VMEM((d, dgu), BF16), pltpu.VMEM((dff, d), BF16)]),
        out_shape=jax.ShapeDtypeStruct((n_rows, d), F32),
        compiler_params=_cparams(("arbitrary",)),
        name="moe",
    )(tile_expert, n_tiles_used, xs, row_gate[:, None], w_gate_up, b_gate_up[:, None, :],
      w_down, b_down[:, None, :])


def _moe(h2, logits, w_gate_up, b_gate_up, w_down, b_down):
    t, d = h2.shape
    n_exp = w_gate_up.shape[0]
    tm = MOE_TILE
    top_val, top_idx = lax.top_k(logits, TOP_K)
    gates = jax.nn.softmax(top_val, axis=-1)
    n_assign = t * TOP_K
    e_flat = top_idx.reshape(-1).astype(jnp.int32)
    order = jnp.argsort(e_flat)
    e_s = e_flat[order]
    counts = jnp.bincount(e_flat, length=n_exp).astype(jnp.int32)
    starts = jnp.cumsum(counts) - counts
    padded = (counts + tm - 1) // tm * tm
    pends = jnp.cumsum(padded)
    pstarts = pends - padded
    rows_s = pstarts[e_s] + (jnp.arange(n_assign, dtype=jnp.int32) - starts[e_s])
    n_tiles = -(-n_assign // tm) + n_exp
    n_rows = n_tiles * tm
    row_tok = jnp.full((n_rows,), t, jnp.int32).at[rows_s].set(order // TOP_K)
    row_gate = jnp.zeros((n_rows,), F32).at[rows_s].set(gates.reshape(-1)[order])
    pos = jnp.zeros((n_assign,), jnp.int32).at[order].set(rows_s)
    tile_start = jnp.arange(n_tiles, dtype=jnp.int32) * tm
    tile_expert = jnp.minimum(jnp.searchsorted(pends, tile_start, side='right'), n_exp - 1).astype(jnp.int32)
    n_used = (pends[-1] // tm).astype(jnp.int32)[None]
    tile_expert = jnp.where(tile_start < pends[-1], tile_expert, tile_expert[jnp.maximum(n_used[0] - 1, 0)])
    h2_pad = jnp.concatenate([h2, jnp.zeros((1, d), h2.dtype)], axis=0)
    xs = h2_pad[row_tok]
    out_rows = _moe_rows(xs, row_gate, tile_expert, n_used, w_gate_up, b_gate_up, w_down, b_down)
    return out_rows[pos].reshape(t, TOP_K, d).sum(axis=1)


def _lambda_init(layer):
    return 0.8 - 0.6 * math.exp(-0.3 * layer)


def kernel(x_prompt, x_sample, cache_k, cache_v, state_ssm_re, state_ssm_im, g_mix, w_in, ssm_a_re, ssm_a_im, ssm_log_dt, ssm_b_re, ssm_b_im, ssm_c_re, ssm_c_im, ssm_d, w_glu, g_ssm_out, g_q, g_k, lambda_q1, lambda_k1, lambda_q2, lambda_k2, g_subln, w_out, g_ffn, w_router, b_router, w_gate_up, b_gate_up, w_down, b_down):
    batch, seq, d = x_prompt.shape
    n_streams, dec_seq, _ = x_sample.shape
    depth = w_in.shape[0]
    past_len = cache_k.shape[2]
    n_heads = cache_k.shape[3]
    aw = n_heads * HEAD_W
    n_groups, n_state = ssm_a_re.shape[1:]
    nq = n_groups // SSM_QG
    assert dec_seq == CHUNK and seq % SSM_ROWS == 0 and n_streams % 8 == 0 and n_state == SSM_STATE
    tp = batch * seq
    ts = n_streams * dec_seq
    n_prompt_sc = tp // SSM_ROWS
    sc_per_seq = seq // SSM_ROWS
    n_sample_sc = ts // SSM_ROWS
    slopes = jnp.asarray([2.0 ** (-8.0 * (h + 1) / n_heads) for h in range(n_heads)], F32)

    x = jnp.concatenate([x_prompt.reshape(tp, d), x_sample.reshape(ts, d)], axis=0)
    outs = {name: [] for name in ("kp", "vp", "srp", "sip", "ks", "vs", "srs", "sis")}
    for l in range(depth):
        u, q, k, v, kb, vb = _inproj(x, g_mix[l], w_in[l].astype(BF16), g_q[l], g_k[l])

        tables = _ssm_tables(ssm_a_re[l], ssm_a_im[l], ssm_log_dt[l], ssm_b_re[l], ssm_b_im[l],
                             ssm_c_re[l], ssm_c_im[l], ssm_d[l])

        def state_lanes(z):
            return z.reshape(n_sample_sc, 8, nq, SSM_QW).transpose(0, 2, 1, 3)

        h0_s = jnp.concatenate([state_lanes(state_ssm_re[l]), state_lanes(state_ssm_im[l])], axis=-1)
        h0_all = jnp.concatenate([jnp.zeros((n_prompt_sc,) + h0_s.shape[1:], F32), h0_s], axis=0)
        ys, fin = _ssm(u, tables, h0_all, n_prompt_sc, sc_per_seq)

        lam_init = _lambda_init(l)
        lam = (jnp.exp(jnp.sum(lambda_q1[l].astype(F32) * lambda_k1[l].astype(F32)))
               - jnp.exp(jnp.sum(lambda_q2[l].astype(F32) * lambda_k2[l].astype(F32))) + lam_init)[None]
        out_scale = 1.0 - lam_init
        o_p = _attn_prompt(q, kb, vb, slopes, lam, g_subln[l], out_scale, batch, seq)
        o_s = _attn_sample(q, kb, vb, cache_k[l].reshape(n_streams, past_len, aw),
                           cache_v[l].reshape(n_streams, past_len, aw), slopes, lam, g_subln[l],
                           out_scale, tp)
        oa = jnp.concatenate([o_p, o_s], axis=0)

        x1, h2, logits = _outproj(ys, oa, x, w_glu[l], g_ssm_out[l], w_out[l], g_ffn[l],
                                  w_router[l], b_router[l])
        n_exp = w_router.shape[2]
        x = x1 + _moe(h2, logits[:, :n_exp], w_gate_up[l], b_gate_up[l], w_down[l], b_down[l])

        outs["kp"].append(k[:tp].reshape(batch, seq, n_heads, HEAD_W))
        outs["vp"].append(v[:tp].reshape(batch, seq, n_heads, HEAD_W))
        outs["ks"].append(k[tp:].reshape(n_streams, dec_seq, n_heads, HEAD_W))
        outs["vs"].append(v[tp:].reshape(n_streams, dec_seq, n_heads, HEAD_W))
        fin_p = fin[:n_prompt_sc].reshape(batch, sc_per_seq, nq, 8, 2, SSM_QW)[:, -1, :, -1]
        outs["srp"].append(fin_p[:, :, 0].reshape(batch, n_groups, n_state))
        outs["sip"].append(fin_p[:, :, 1].reshape(batch, n_groups, n_state))
        fin_s = fin[n_prompt_sc:].reshape(n_sample_sc, nq, 8, 2, SSM_QW).transpose(0, 2, 3, 1, 4)
        outs["srs"].append(fin_s[:, :, 0].reshape(n_streams, n_groups, n_state))
        outs["sis"].append(fin_s[:, :, 1].reshape(n_streams, n_groups, n_state))

    st = {name: jnp.stack(vals) for name, vals in outs.items()}
    return (x[:tp].reshape(batch, seq, d), x[tp:].reshape(n_streams, dec_seq, d),
            st["kp"], st["vp"], st["srp"], st["sip"], st["ks"], st["vs"], st["srs"], st["sis"])
```

```python
import functools
import math

import jax
import jax.numpy as jnp
from jax import lax
from jax.experimental import pallas as pl
from jax.experimental.pallas import tpu as pltpu

F32 = jnp.float32
BF16 = jnp.bfloat16

CHUNK = 64
HEAD_DIM = 64
HEAD_W = 2 * HEAD_DIM
SSM_GROUP = 16
SSM_STATE = 64
SSM_QG = 8
SSM_QW = SSM_QG * SSM_STATE
TOP_K = 4
SWIGLU_ALPHA = 1.702
SWIGLU_LIMIT = 7.0
EPS = 1e-6
NEG_INF = -1e30

TOK_TILE = 512
SSM_ROWS = 8 * CHUNK
ATT_TQ = 512
ATT_TKC = 2048
MOE_TILE = 256
VMEM_LIMIT = 56 * 1024 * 1024


def _cparams(sem):
    return pltpu.CompilerParams(dimension_semantics=sem, vmem_limit_bytes=VMEM_LIMIT)


def _dot(a, b):
    return jnp.dot(a, b, preferred_element_type=F32)


def _dot_t(a, b):
    return lax.dot_general(a, b, (((1,), (1,)), ((), ())), preferred_element_type=F32)


def _split_bf16(x):
    hi = x.astype(BF16)
    lo = (x - hi.astype(F32)).astype(BF16)
    return hi, lo


def _inproj_kernel(x_ref, g_ref, w_ref, gq_ref, gk_ref, seg_ref, *rest, n_prompt_tiles, n_prev):
    u_ref, q_ref, kb_ref, vb_ref, kp_ref, vp_ref, ks_ref, vs_ref = rest[n_prev:]
    i = pl.program_id(0)
    x = x_ref[...]
    ms = jnp.mean(x * x, axis=-1, keepdims=True)
    h = (x * lax.rsqrt(ms + EPS) * g_ref[...]).astype(BF16)
    proj = _dot(h, w_ref[...])
    w = u_ref.shape[-1]
    seg = seg_ref[...]

    def head_norm(z, g):
        hi, lo = _split_bf16(z * z)
        ms_ = _dot(hi, seg) + _dot(lo, seg)
        return z * lax.rsqrt(ms_ + EPS) * g

    u_ref[...] = proj[:, :w]
    qn = head_norm(proj[:, w:2 * w], gq_ref[...])
    q_ref[...] = (qn * (HEAD_DIM ** -0.5)).astype(BF16)
    kn = head_norm(proj[:, 2 * w:3 * w], gk_ref[...])
    kb_ref[...] = kn.astype(BF16)
    vv = proj[:, 3 * w:]
    vb_ref[...] = vv.astype(BF16)

    def emit(k_out, v_out):
        for hd in range(k_out.shape[1]):
            k_out[:, hd, :] = kn[:, hd * HEAD_W:(hd + 1) * HEAD_W]
            v_out[:, hd, :] = vv[:, hd * HEAD_W:(hd + 1) * HEAD_W]

    @pl.when(i < n_prompt_tiles)
    def _():
        emit(kp_ref, vp_ref)

    @pl.when(i >= n_prompt_tiles)
    def _():
        emit(ks_ref, vs_ref)


def _inproj(x, g_mix, w_in_bf, g_q, g_k, layer, depth, n_prompt_tiles, prev):
    t, d = x.shape
    aw = w_in_bf.shape[1] // 4
    nh = aw // HEAD_W
    nrep = aw // HEAD_DIM
    gq = jnp.tile(g_q.astype(F32), nrep)[None]
    gk = jnp.tile(g_k.astype(F32), nrep)[None]
    ids = jnp.arange(aw) // HEAD_DIM
    seg = jnp.where(ids[:, None] == ids[None, :], 1.0 / HEAD_DIM, 0.0).astype(BF16)
    tm = TOK_TILE
    n_tiles = t // tm
    n_sample_tiles = n_tiles - n_prompt_tiles
    row = lambda i: (i, 0)
    fixed = lambda i: (0, 0)
    p_blk = pl.BlockSpec((None, tm, nh, HEAD_W),
                         lambda i: (layer * n_prompt_tiles + jnp.minimum(i, n_prompt_tiles - 1), 0, 0, 0))
    s_blk = pl.BlockSpec((None, tm, nh, HEAD_W),
                         lambda i: (layer * n_sample_tiles + jnp.maximum(i - n_prompt_tiles, 0), 0, 0, 0))
    p_shape = jax.ShapeDtypeStruct((depth * n_prompt_tiles, tm, nh, HEAD_W), F32)
    s_shape = jax.ShapeDtypeStruct((depth * n_sample_tiles, tm, nh, HEAD_W), F32)
    outs = [jax.ShapeDtypeStruct((t, aw), dt) for dt in (F32, BF16, BF16, BF16)] + [p_shape, p_shape, s_shape, s_shape]
    prev = () if prev is None else tuple(prev)
    n_in = 6
    kern = functools.partial(_inproj_kernel, n_prompt_tiles=n_prompt_tiles, n_prev=len(prev))
    return pl.pallas_call(
        kern,
        grid=(n_tiles,),
        in_specs=[pl.BlockSpec((tm, d), row), pl.BlockSpec((1, d), fixed),
                  pl.BlockSpec(w_in_bf.shape, fixed), pl.BlockSpec((1, aw), fixed),
                  pl.BlockSpec((1, aw), fixed), pl.BlockSpec((aw, aw), fixed)]
                 + [pl.BlockSpec(memory_space=pl.ANY)] * len(prev),
        out_specs=[pl.BlockSpec((tm, aw), row)] * 4 + [p_blk, p_blk, s_blk, s_blk],
        out_shape=outs,
        input_output_aliases={n_in + j: 4 + j for j in range(len(prev))},
        compiler_params=_cparams(("arbitrary",)),
        name="inproj",
    )(x, g_mix[None].astype(F32), w_in_bf, gq, gk, seg, *prev)


def _ssm_kernel(u_ref, wb_ref, wc_ref, ab_ref, a64_ref, pw_ref, d_ref, h0_ref,
                y_ref, fin_ref, uperm_ref, st_ref, carry_ref, *, n_prompt_sc, sc_per_seq):
    sc = pl.program_id(0)
    qb = pl.program_id(1)
    nsteps = CHUNK
    w = SSM_QW

    for t in range(nsteps):
        uperm_ref[t * 8:(t + 1) * 8, :] = u_ref[pl.ds(t, 8, stride=nsteps), :]
    up = uperm_ref[...]
    st_ref[...] = _dot(up.astype(BF16), wb_ref[0])

    ab = ab_ref[0]
    ar = jnp.broadcast_to(ab[:, :w], (8, w))
    ai = jnp.broadcast_to(ab[:, w:], (8, w))

    def scan_step(t, carry):
        hr, hi = carry
        r0 = pl.multiple_of(t * 8, 8)
        br = st_ref[pl.ds(r0, 8), :w]
        bi = st_ref[pl.ds(r0, 8), w:]
        nr = ar * hr - ai * hi + br
        ni = ar * hi + ai * hr + bi
        st_ref[pl.ds(r0, 8), :w] = nr
        st_ref[pl.ds(r0, 8), w:] = ni
        return nr, ni

    zero = jnp.zeros((8, w), F32)
    er, ei = lax.fori_loop(0, nsteps, scan_step, (zero, zero))

    a64 = a64_ref[0]
    a64r = a64[:, :w]
    a64i = a64[:, w:]
    is_sample = sc >= n_prompt_sc

    @pl.when(jnp.logical_or(is_sample, sc % sc_per_seq == 0))
    def _():
        carry_ref[qb] = jnp.zeros(carry_ref.shape[1:], F32)

    cin = carry_ref[qb]
    cr = cin[:, :w]
    ci = cin[:, w:]
    rows = lax.broadcasted_iota(jnp.int32, (8, w), 0)
    sr = jnp.zeros((8, w), F32)
    si = jnp.zeros((8, w), F32)
    for j in range(8):
        sr = jnp.where(rows == j, cr, sr)
        si = jnp.where(rows == j, ci, si)
        ejr = er[j:j + 1]
        eji = ei[j:j + 1]
        cr, ci = a64r * cr - a64i * ci + ejr, a64r * ci + a64i * cr + eji
    carry_ref[qb] = jnp.concatenate([cr, ci], axis=1)
    h0 = h0_ref[0, 0]
    given = (jnp.zeros((8, w), jnp.int32) + is_sample.astype(jnp.int32)) > 0
    sr = jnp.where(given, h0[:, :w], sr)
    si = jnp.where(given, h0[:, w:], si)
    fr = a64r * sr - a64i * si + er
    fi = a64r * si + a64i * sr + ei
    fin_ref[0, 0] = jnp.concatenate([fr, fi], axis=1)

    def fix_step(t, _):
        r0 = pl.multiple_of(t * 8, 8)
        p = pw_ref[0, pl.ds(t, 1), :]
        pr = p[:, :w]
        pi = p[:, w:]
        st_ref[pl.ds(r0, 8), :w] = st_ref[pl.ds(r0, 8), :w] + (pr * sr - pi * si)
        st_ref[pl.ds(r0, 8), w:] = st_ref[pl.ds(r0, 8), w:] + (pr * si + pi * sr)
        return 0

    lax.fori_loop(0, nsteps, fix_step, 0)

    y = _dot(st_ref[...].astype(BF16), wc_ref[0]) + up * d_ref[0]
    for t in range(nsteps):
        y_ref[pl.ds(t, 8, stride=nsteps), :] = y[t * 8:(t + 1) * 8, :]


def _ssm_tables(a_re, a_im, log_dt, b_re, b_im, c_re, c_im, d_skip):
    g, n = a_re.shape
    c = b_re.shape[-1]
    nq = g // SSM_QG
    dt = jnp.exp(log_dt)[:, None]
    za_re, za_im = dt * a_re, dt * a_im
    mag = jnp.exp(za_re)
    ab_re, ab_im = mag * jnp.cos(za_im), mag * jnp.sin(za_im)
    den = a_re * a_re + a_im * a_im
    n_re, n_im = ab_re - 1.0, ab_im
    f_re = (n_re * a_re + n_im * a_im) / den
    f_im = (n_im * a_re - n_re * a_im) / den
    bb_re = f_re[..., None] * b_re - f_im[..., None] * b_im
    bb_im = f_re[..., None] * b_im + f_im[..., None] * b_re
    eye = jnp.eye(SSM_QG, dtype=F32)

    def in_w(bb):
        bq = bb.reshape(nq, SSM_QG, n, c)
        return jnp.einsum('qgnc,gh->qgchn', bq, eye).reshape(nq, SSM_QG * c, SSM_QG * n)

    def out_w(cc):
        cq = cc.reshape(nq, SSM_QG, c, n)
        return jnp.einsum('qgcn,gh->qgnhc', cq, eye).reshape(nq, SSM_QG * n, SSM_QG * c)

    wb = jnp.concatenate([in_w(bb_re), in_w(bb_im)], axis=2).astype(BF16)
    wc = jnp.concatenate([out_w(c_re), out_w(-c_im)], axis=1).astype(BF16)

    def lanes(z):
        return jnp.moveaxis(z.reshape(z.shape[:-2] + (nq, SSM_QG * n)), -2, 0)

    def power(k):
        m = jnp.exp(k * za_re)
        return m * jnp.cos(k * za_im), m * jnp.sin(k * za_im)

    ab = jnp.concatenate([lanes(ab_re), lanes(ab_im)], axis=-1)[:, None]
    p64 = power(float(CHUNK))
    a64 = jnp.concatenate([lanes(p64[0]), lanes(p64[1])], axis=-1)[:, None]
    ks = jnp.arange(1, CHUNK + 1, dtype=F32)[:, None, None]
    pk = power(ks)
    pw = jnp.concatenate([lanes(pk[0]), lanes(pk[1])], axis=-1)
    dq = d_skip.reshape(nq, 1, SSM_QG * c).astype(F32)
    return wb, wc, ab, a64, pw, dq


def _ssm(u, tables, h0_all, n_prompt_sc, sc_per_seq):
    wb, wc, ab, a64, pw, dq = tables
    t, cw = u.shape
    nq = wb.shape[0]
    n_sc = t // SSM_ROWS
    sw = 2 * SSM_QW
    kern = functools.partial(_ssm_kernel, n_prompt_sc=n_prompt_sc, sc_per_seq=sc_per_seq)
    per_q = lambda s, q: (q, 0, 0)
    return pl.pallas_call(
        kern,
        grid=(n_sc, nq),
        in_specs=[pl.BlockSpec((SSM_ROWS, 128), lambda s, q: (s, q)),
                  pl.BlockSpec((1,) + wb.shape[1:], per_q), pl.BlockSpec((1,) + wc.shape[1:], per_q),
                  pl.BlockSpec((1, 1, sw), per_q), pl.BlockSpec((1, 1, sw), per_q),
                  pl.BlockSpec((1, CHUNK, sw), per_q), pl.BlockSpec((1, 1, 128), per_q),
                  pl.BlockSpec((1, 1, 8, sw), lambda s, q: (s, q, 0, 0))],
        out_specs=[pl.BlockSpec((SSM_ROWS, 128), lambda s, q: (s, q)),
                   pl.BlockSpec((1, 1, 8, sw), lambda s, q: (s, q, 0, 0))],
        out_shape=[jax.ShapeDtypeStruct((t, cw), F32),
                   jax.ShapeDtypeStruct((n_sc, nq, 8, sw), F32)],
        scratch_shapes=[pltpu.VMEM((SSM_ROWS, 128), F32), pltpu.VMEM((SSM_ROWS, sw), F32),
                        pltpu.VMEM((nq, 1, sw), F32)],
        compiler_params=_cparams(("arbitrary", "arbitrary")),
        name="ssm",
    )(u, wb, wc, ab, a64, pw, dq, h0_all)


def _stack_q(q):
    lane = lax.broadcasted_iota(jnp.int32, q.shape, 1)
    zero = jnp.zeros_like(q)
    return jnp.concatenate([jnp.where(lane < HEAD_DIM, q, zero), jnp.where(lane >= HEAD_DIM, q, zero)], axis=0)


def _online_update(s, v, m_ref, l_ref, acc_ref):
    m_old = m_ref[...]
    m_new = jnp.maximum(m_old, jnp.max(s, axis=-1, keepdims=True))
    alpha = jnp.exp(m_old - m_new)
    p = jnp.exp(s - m_new)
    l_ref[...] = alpha * l_ref[...] + jnp.sum(p, axis=-1, keepdims=True)
    acc_ref[...] = alpha * acc_ref[...] + _dot(p.astype(BF16), v)
    m_ref[...] = m_new


def _finish_head(m_ref, l_ref, acc_ref, lam, g, out_scale, tq):
    acc = acc_ref[...]
    l = l_ref[...]
    o = acc[:tq] / l[:tq] - lam * (acc[tq:] / l[tq:])
    ms = jnp.mean(o * o, axis=-1, keepdims=True)
    return o * lax.rsqrt(ms + EPS) * g * out_scale


def _attn_prompt_kernel(slope_ref, lam_ref, q_ref, k_ref, v_ref, g_ref, o_ref,
                        m_ref, l_ref, acc_ref, *, out_scale):
    h = pl.program_id(1)
    i = pl.program_id(2)
    tq = q_ref.shape[0]
    tk = tq
    slope = slope_ref[h]
    qq = _stack_q(q_ref[...])
    m_ref[...] = jnp.full(m_ref.shape, NEG_INF, F32)
    l_ref[...] = jnp.zeros(l_ref.shape, F32)
    acc_ref[...] = jnp.zeros(acc_ref.shape, F32)

    def past(j, _):
        k0 = pl.multiple_of(j * tk, tk)
        k = k_ref[pl.ds(k0, tk), :]
        v = v_ref[pl.ds(k0, tk), :]
        kpos = (k0 + lax.broadcasted_iota(jnp.int32, (1, tk), 1)).astype(F32)
        s = _dot_t(qq, k) + slope * kpos
        _online_update(s, v, m_ref, l_ref, acc_ref)
        return 0

    lax.fori_loop(0, i, past, 0)

    k0 = pl.multiple_of(i * tk, tk)
    k = k_ref[pl.ds(k0, tk), :]
    v = v_ref[pl.ds(k0, tk), :]
    r = lax.broadcasted_iota(jnp.int32, (2 * tq, tk), 0)
    r = jnp.where(r >= tq, r - tq, r)
    c = lax.broadcasted_iota(jnp.int32, (2 * tq, tk), 1)
    bias = slope * (k0 + r - jnp.abs(r - c)).astype(F32)
    visible = (c >> 6) <= (r >> 6)
    s = jnp.where(visible, _dot_t(qq, k) + bias, NEG_INF)
    _online_update(s, v, m_ref, l_ref, acc_ref)
    o_ref[...] = _finish_head(m_ref, l_ref, acc_ref, lam_ref[0], g_ref[...], out_scale, tq).astype(o_ref.dtype)


def _attn_prompt(q, kb, vb, slopes, lam, g_subln, out_scale, batch, seq):
    n_heads = q.shape[1] // HEAD_W
    tq = ATT_TQ
    nq = seq // tq
    kern = functools.partial(_attn_prompt_kernel, out_scale=out_scale)
    smem = pl.BlockSpec(memory_space=pltpu.SMEM)
    return pl.pallas_call(
        kern,
        grid=(batch, n_heads, nq),
        in_specs=[smem, smem,
                  pl.BlockSpec((tq, HEAD_W), lambda b, h, i: (b * nq + i, h)),
                  pl.BlockSpec((seq, HEAD_W), lambda b, h, i: (b, h)),
                  pl.BlockSpec((seq, HEAD_W), lambda b, h, i: (b, h)),
                  pl.BlockSpec((1, HEAD_W), lambda b, h, i: (0, 0))],
        out_specs=pl.BlockSpec((tq, HEAD_W), lambda b, h, i: (b * nq + i, h)),
        out_shape=jax.ShapeDtypeStruct((batch * seq, q.shape[1]), BF16),
        scratch_shapes=[pltpu.VMEM((2 * tq, 1), F32), pltpu.VMEM((2 * tq, 1), F32),
                        pltpu.VMEM((2 * tq, HEAD_W), F32)],
        compiler_params=_cparams(("parallel", "parallel", "arbitrary")),
        name="attn_prompt",
    )(slopes, lam, q, kb, vb, g_subln[None].astype(F32))


def _attn_sample_kernel(slope_ref, lam_ref, q_ref, kn_ref, vn_ref, kc_ref, vc_ref, g_ref, o_ref,
                        m_ref, l_ref, acc_ref, *, out_scale, past_len):
    j = pl.program_id(1)
    nj = pl.num_programs(1)
    tq = q_ref.shape[0]
    tk = kc_ref.shape[0]
    n_heads = q_ref.shape[1] // HEAD_W

    @pl.when(j == 0)
    def _():
        m_ref[...] = jnp.full(m_ref.shape, NEG_INF, F32)
        l_ref[...] = jnp.zeros(l_ref.shape, F32)
        acc_ref[...] = jnp.zeros(acc_ref.shape, F32)

    kpos = (j * tk + lax.broadcasted_iota(jnp.int32, (1, tk), 1)).astype(F32)
    for h in range(n_heads):
        cols = slice(h * HEAD_W, (h + 1) * HEAD_W)
        qq = _stack_q(q_ref[:, cols])
        k = kc_ref[:, h, :].astype(BF16)
        v = vc_ref[:, h, :].astype(BF16)
        s = _dot_t(qq, k) + slope_ref[h] * kpos
        _online_update(s, v, m_ref.at[h], l_ref.at[h], acc_ref.at[h])

    @pl.when(j == nj - 1)
    def _():
        r = lax.broadcasted_iota(jnp.int32, (2 * tq, tq), 0)
        r = jnp.where(r >= tq, r - tq, r)
        c = lax.broadcasted_iota(jnp.int32, (2 * tq, tq), 1)
        rel = (past_len + r - jnp.abs(r - c)).astype(F32)
        for h in range(n_heads):
            cols = slice(h * HEAD_W, (h + 1) * HEAD_W)
            qq = _stack_q(q_ref[:, cols])
            s = _dot_t(qq, kn_ref[:, cols]) + slope_ref[h] * rel
            _online_update(s, vn_ref[:, cols], m_ref.at[h], l_ref.at[h], acc_ref.at[h])
            o_ref[:, cols] = _finish_head(m_ref.at[h], l_ref.at[h], acc_ref.at[h], lam_ref[0],
                                          g_ref[...], out_scale, tq).astype(o_ref.dtype)


def _attn_sample(q, kb, vb, cache_k, cache_v, layer, slopes, lam, g_subln, out_scale, row0):
    _, n_streams, past_len, n_heads, _ = cache_k.shape
    aw = n_heads * HEAD_W
    tq = CHUNK
    tk = min(ATT_TKC, past_len)
    blk0 = row0 // tq
    kern = functools.partial(_attn_sample_kernel, out_scale=out_scale, past_len=past_len)
    smem = pl.BlockSpec(memory_space=pltpu.SMEM)
    new = pl.BlockSpec((tq, aw), lambda s, j: (blk0 + s, 0))
    past = pl.BlockSpec((None, None, tk, n_heads, HEAD_W), lambda s, j: (layer, s, j, 0, 0))
    return pl.pallas_call(
        kern,
        grid=(n_streams, past_len // tk),
        in_specs=[smem, smem, new, new, new, past, past,
                  pl.BlockSpec((1, HEAD_W), lambda s, j: (0, 0))],
        out_specs=pl.BlockSpec((tq, aw), lambda s, j: (s, 0)),
        out_shape=jax.ShapeDtypeStruct((n_streams * tq, aw), BF16),
        scratch_shapes=[pltpu.VMEM((n_heads, 2 * tq, 1), F32), pltpu.VMEM((n_heads, 2 * tq, 1), F32),
                        pltpu.VMEM((n_heads, 2 * tq, HEAD_W), F32)],
        compiler_params=_cparams(("parallel", "arbitrary")),
        name="attn_sample",
    )(slopes, lam, q, kb, vb, cache_k, cache_v, g_subln[None].astype(F32))


def _outproj_kernel(ys_ref, oa_ref, x_ref, wglu_ref, gs_ref, wtop_ref, wbot_ref, gf_ref,
                    wrh_ref, wrl_ref, br_ref, x1_ref, h2_ref, lg_ref):
    y = ys_ref[...]
    y = 0.5 * y * (1.0 + jnp.tanh(math.sqrt(2.0 / math.pi) * (y + 0.044715 * (y * y * y))))
    z = _dot(y.astype(BF16), wglu_ref[...])
    y = y * (1.0 / (1.0 + jnp.exp(-z)))
    ms = jnp.mean(y * y, axis=-1, keepdims=True)
    y = y * lax.rsqrt(ms + EPS) * gs_ref[...]
    x1 = x_ref[...] + _dot(y.astype(BF16), wtop_ref[...]) + _dot(oa_ref[...], wbot_ref[...])
    x1_ref[...] = x1
    ms = jnp.mean(x1 * x1, axis=-1, keepdims=True)
    h2 = x1 * lax.rsqrt(ms + EPS) * gf_ref[...]
    for j in range(h2_ref.shape[1]):
        h2_ref[:, j, :] = h2[:, j * 128:(j + 1) * 128]
    hi, lo = _split_bf16(h2)
    wrh = wrh_ref[...]
    lg_ref[...] = _dot(hi, wrh) + _dot(lo, wrh) + _dot(hi, wrl_ref[...]) + br_ref[...]


def _outproj(ys, oa, x, w_glu, g_ssm, w_out, g_ffn, w_router, b_router):
    t, d = x.shape
    sw = ys.shape[1]
    n_exp = w_router.shape[1]
    lw = max(128, n_exp)
    wr = jnp.zeros((d, lw), F32).at[:, :n_exp].set(w_router)
    wrh, wrl = _split_bf16(wr)
    br = jnp.zeros((1, lw), F32).at[0, :n_exp].set(b_router)
    w_out_bf = w_out.astype(BF16)
    tm = TOK_TILE
    row = lambda i: (i, 0)
    fixed = lambda i: (0, 0)
    return pl.pallas_call(
        _outproj_kernel,
        grid=(t // tm,),
        in_specs=[pl.BlockSpec((tm, sw), row), pl.BlockSpec((tm, d - sw), row), pl.BlockSpec((tm, d), row),
                  pl.BlockSpec((sw, sw), fixed), pl.BlockSpec((1, sw), fixed),
                  pl.BlockSpec((sw, d), fixed), pl.BlockSpec((d - sw, d), fixed), pl.BlockSpec((1, d), fixed),
                  pl.BlockSpec((d, lw), fixed), pl.BlockSpec((d, lw), fixed), pl.BlockSpec((1, lw), fixed)],
        out_specs=[pl.BlockSpec((tm, d), row), pl.BlockSpec((tm, d // 128, 128), lambda i: (i, 0, 0)),
                   pl.BlockSpec((tm, lw), row)],
        out_shape=[jax.ShapeDtypeStruct((t, d), F32), jax.ShapeDtypeStruct((t, d // 128, 128), F32),
                   jax.ShapeDtypeStruct((t, lw), F32)],
        compiler_params=_cparams(("parallel",)),
        name="outproj",
    )(ys, oa, x, w_glu.astype(BF16), g_ssm[None].astype(F32), w_out_bf[:sw], w_out_bf[sw:],
      g_ffn[None].astype(F32), wrh, wrl, br)


def _moe_kernel(te_ref, nt_ref, src_ref, nxt_ref, dprev_ref, h2_hbm, wgu_ref, bgu_ref, wd_ref, bd_ref,
                out_hbm, xbuf, obuf, wgu_bf, wd_bf, gsem, ssem):
    i = pl.program_id(0)
    nt = nt_ref[0]
    slot = i % 2
    other = 1 - slot
    tm = xbuf.shape[1]

    def start_gather(idx_ref, s):
        def one(r, _):
            pltpu.make_async_copy(h2_hbm.at[idx_ref[0, 0, r]], xbuf.at[s, r], gsem.at[s]).start()
            return 0
        lax.fori_loop(0, tm, one, 0, unroll=8)

    def wait_gather(s):
        pltpu.make_async_copy(xbuf.at[s], xbuf.at[s], gsem.at[s]).wait()

    def start_scatter(idx_ref, s):
        def one(r, _):
            pltpu.make_async_copy(obuf.at[s, r], out_hbm.at[idx_ref[0, 0, r]], ssem.at[0]).start()
            return 0
        lax.fori_loop(0, tm, one, 0, unroll=8)

    def wait_scatter(s):
        pltpu.make_async_copy(obuf.at[s], obuf.at[s], ssem.at[0]).wait()

    @pl.when(i == 0)
    def _():
        obuf[...] = jnp.zeros(obuf.shape, obuf.dtype)
        start_gather(src_ref, 0)

    @pl.when(i < nt)
    def _():
        wait_gather(slot)

    prev = te_ref[jnp.maximum(i - 1, 0)]
    new_expert = jnp.logical_or(i == 0, te_ref[i] != prev)

    @pl.when(jnp.logical_and(i < nt, new_expert))
    def _():
        wgu_bf[...] = wgu_ref[0].astype(BF16)
        wd_bf[...] = wd_ref[0].astype(BF16)

    @pl.when(i < nt)
    def _():
        start_gather(nxt_ref, other)
        start_scatter(dprev_ref, other)
        dff = wd_bf.shape[0]
        nblk = xbuf.shape[2]
        xin = xbuf.at[slot]
        x = jnp.concatenate([xin[:, j, :] for j in range(nblk)], axis=1).astype(BF16)
        gu = _dot(x, wgu_bf[...]) + bgu_ref[0]
        x_glu = jnp.minimum(gu[:, :dff], SWIGLU_LIMIT)
        x_lin = jnp.clip(gu[:, dff:], -SWIGLU_LIMIT, SWIGLU_LIMIT)
        hdn = x_glu * (1.0 / (1.0 + jnp.exp(-SWIGLU_ALPHA * x_glu))) * (x_lin + 1.0)
        out = _dot(hdn.astype(BF16), wd_bf[...]) + bd_ref[0]
        res = obuf.at[slot]
        for j in range(nblk):
            res[:, j, :] = out[:, j * 128:(j + 1) * 128]
        wait_scatter(other)

    @pl.when(i == nt)
    def _():
        wait_gather(slot)
        start_scatter(dprev_ref, other)
        wait_scatter(other)


def _moe_rows(h2, route, layer, w_gate_up, b_gate_up, w_down, b_down):
    tile_expert, n_used, src, dst = route
    t, nblk, _ = h2.shape
    d = nblk * 128
    depth, n_exp, _, dgu = w_gate_up.shape
    dff = w_down.shape[2]
    tm = MOE_TILE
    n_tiles = tile_expert.shape[0]
    by_e = lambda i, te, nt: (layer * n_exp + te[i], 0, 0)
    smem = lambda f: pl.BlockSpec((1, 1, tm), f, memory_space=pltpu.SMEM)
    any_ = pl.BlockSpec(memory_space=pl.ANY)
    return pl.pallas_call(
        _moe_kernel,
        grid_spec=pltpu.PrefetchScalarGridSpec(
            num_scalar_prefetch=2,
            grid=(n_tiles,),
            in_specs=[smem(lambda i, te, nt: (i, 0, 0)), smem(lambda i, te, nt: (i + 1, 0, 0)),
                      smem(lambda i, te, nt: (i, 0, 0)), any_,
                      pl.BlockSpec((1, d, dgu), by_e), pl.BlockSpec((1, 1, dgu), by_e),
                      pl.BlockSpec((1, dff, d), by_e), pl.BlockSpec((1, 1, d), by_e)],
            out_specs=any_,
            scratch_shapes=[pltpu.VMEM((2, tm, nblk, 128), F32), pltpu.VMEM((2, tm, nblk, 128), F32),
                            pltpu.VMEM((d, dgu), BF16), pltpu.VMEM((dff, d), BF16),
                            pltpu.SemaphoreType.DMA((2,)), pltpu.SemaphoreType.DMA((1,))]),
        out_shape=jax.ShapeDtypeStruct((TOP_K * t + tm, nblk, 128), F32),
        compiler_params=_cparams(("arbitrary",)),
        name="moe",
    )(tile_expert, n_used, src, src, dst, h2, w_gate_up.reshape(depth * n_exp, d, dgu),
      b_gate_up.reshape(depth * n_exp, 1, dgu), w_down.reshape(depth * n_exp, dff, d),
      b_down.reshape(depth * n_exp, 1, d))


def _moe_route(logits, n_exp):
    t = logits.shape[0]
    tm = MOE_TILE
    top_val, top_idx = lax.top_k(logits, TOP_K)
    gates = jax.nn.softmax(top_val, axis=-1)
    n_assign = t * TOP_K
    e_flat = top_idx.T.reshape(-1).astype(jnp.int32)
    order = jnp.argsort(e_flat).astype(jnp.int32)
    experts = jnp.arange(n_exp, dtype=jnp.int32)
    counts = jnp.sum((e_flat[:, None] == experts[None, :]).astype(jnp.int32), axis=0)
    starts = jnp.cumsum(counts) - counts
    tiles_e = (counts + tm - 1) // tm
    tile_end = jnp.cumsum(tiles_e)
    tile_beg = tile_end - tiles_e
    n_used = tile_end[-1]
    n_tiles = n_assign // tm + n_exp + 1
    tile = jnp.arange(n_tiles, dtype=jnp.int32)
    tile_c = jnp.minimum(tile, n_used - 1)
    te = jnp.sum((tile_end[None, :] <= tile_c[:, None]).astype(jnp.int32), axis=1)
    te = jnp.minimum(te, n_exp - 1)
    first = starts[te] + (tile - tile_beg[te]) * tm
    n_valid = jnp.where(tile < n_used, jnp.clip(counts[te] - (tile - tile_beg[te]) * tm, 0, tm), 0)
    r = jnp.arange(tm, dtype=jnp.int32)
    valid = r[None, :] < n_valid[:, None]
    a = order[jnp.clip(first[:, None] + r[None, :], 0, n_assign - 1)]
    src = jnp.where(valid, a % t, 0)
    dst = jnp.where(valid, a, n_assign + r[None, :])
    spare = jnp.broadcast_to(n_assign + r[None, :], (1, tm))
    src = jnp.concatenate([src, jnp.zeros((1, tm), jnp.int32)], axis=0)[:, None, :]
    dst = jnp.concatenate([spare, dst], axis=0)[:, None, :]
    return gates, (te.astype(jnp.int32), n_used.astype(jnp.int32)[None], src, dst)


def _combine_kernel(x1_ref, g_ref, o0_ref, o1_ref, o2_ref, o3_ref, x2_ref):
    g = g_ref[...]
    for j in range(o0_ref.shape[1]):
        cols = slice(j * 128, (j + 1) * 128)
        acc = x1_ref[:, cols]
        for k, o_ref in enumerate((o0_ref, o1_ref, o2_ref, o3_ref)):
            acc = acc + g[:, k:k + 1] * o_ref[:, j, :]
        x2_ref[:, cols] = acc


def _combine(x1, gates, out_rows):
    t, d = x1.shape
    tm = TOK_TILE
    nt = t // tm
    rows = lambda k: pl.BlockSpec((tm, d // 128, 128), lambda i: (k * nt + i, 0, 0))
    return pl.pallas_call(
        _combine_kernel,
        grid=(nt,),
        in_specs=[pl.BlockSpec((tm, d), lambda i: (i, 0)), pl.BlockSpec((tm, TOP_K), lambda i: (i, 0)),
                  rows(0), rows(1), rows(2), rows(3)],
        out_specs=pl.BlockSpec((tm, d), lambda i: (i, 0)),
        out_shape=jax.ShapeDtypeStruct((t, d), F32),
        compiler_params=_cparams(("parallel",)),
        name="combine",
    )(x1, gates, out_rows, out_rows, out_rows, out_rows)


def _lambda_init(layer):
    return 0.8 - 0.6 * math.exp(-0.3 * layer)


def kernel(x_prompt, x_sample, cache_k, cache_v, state_ssm_re, state_ssm_im, g_mix, w_in, ssm_a_re, ssm_a_im, ssm_log_dt, ssm_b_re, ssm_b_im, ssm_c_re, ssm_c_im, ssm_d, w_glu, g_ssm_out, g_q, g_k, lambda_q1, lambda_k1, lambda_q2, lambda_k2, g_subln, w_out, g_ffn, w_router, b_router, w_gate_up, b_gate_up, w_down, b_down):
    batch, seq, d = x_prompt.shape
    n_streams, dec_seq, _ = x_sample.shape
    depth = w_in.shape[0]
    past_len = cache_k.shape[2]
    n_heads = cache_k.shape[3]
    aw = n_heads * HEAD_W
    n_groups, n_state = ssm_a_re.shape[1:]
    nq = n_groups // SSM_QG
    assert dec_seq == CHUNK and seq % SSM_ROWS == 0 and n_streams % 8 == 0 and n_state == SSM_STATE
    tp = batch * seq
    ts = n_streams * dec_seq
    n_prompt_sc = tp // SSM_ROWS
    sc_per_seq = seq // SSM_ROWS
    n_sample_sc = ts // SSM_ROWS
    slopes = jnp.asarray([2.0 ** (-8.0 * (h + 1) / n_heads) for h in range(n_heads)], F32)

    x = jnp.concatenate([x_prompt.reshape(tp, d), x_sample.reshape(ts, d)], axis=0)
    outs = {name: [] for name in ("srp", "sip", "srs", "sis")}
    kv_out = None
    n_exp = w_router.shape[2]
    for l in range(depth):
        u, q, kb, vb, *kv_out = _inproj(x, g_mix[l], w_in[l].astype(BF16), g_q[l], g_k[l], l, depth,
                                        tp // TOK_TILE, kv_out)

        tables = _ssm_tables(ssm_a_re[l], ssm_a_im[l], ssm_log_dt[l], ssm_b_re[l], ssm_b_im[l],
                             ssm_c_re[l], ssm_c_im[l], ssm_d[l])

        def state_lanes(z):
            return z.reshape(n_sample_sc, 8, nq, SSM_QW).transpose(0, 2, 1, 3)

        h0_s = jnp.concatenate([state_lanes(state_ssm_re[l]), state_lanes(state_ssm_im[l])], axis=-1)
        h0_all = jnp.concatenate([jnp.zeros((n_prompt_sc,) + h0_s.shape[1:], F32), h0_s], axis=0)
        ys, fin = _ssm(u, tables, h0_all, n_prompt_sc, sc_per_seq)

        lam_init = _lambda_init(l)
        lam = (jnp.exp(jnp.sum(lambda_q1[l].astype(F32) * lambda_k1[l].astype(F32)))
               - jnp.exp(jnp.sum(lambda_q2[l].astype(F32) * lambda_k2[l].astype(F32))) + lam_init)[None]
        out_scale = 1.0 - lam_init
        o_p = _attn_prompt(q, kb, vb, slopes, lam, g_subln[l], out_scale, batch, seq)
        o_s = _attn_sample(q, kb, vb, cache_k, cache_v, l, slopes, lam, g_subln[l], out_scale, tp)
        oa = jnp.concatenate([o_p, o_s], axis=0)

        x1, h2, logits = _outproj(ys, oa, x, w_glu[l], g_ssm_out[l], w_out[l], g_ffn[l],
                                  w_router[l], b_router[l])
        gates, route = _moe_route(logits[:, :n_exp], n_exp)
        x = _combine(x1, gates, _moe_rows(h2, route, l, w_gate_up, b_gate_up, w_down, b_down))

        fin_p = fin[:n_prompt_sc].reshape(batch, sc_per_seq, nq, 8, 2, SSM_QW)[:, -1, :, -1]
        outs["srp"].append(fin_p[:, :, 0].reshape(batch, n_groups, n_state))
        outs["sip"].append(fin_p[:, :, 1].reshape(batch, n_groups, n_state))
        fin_s = fin[n_prompt_sc:].reshape(n_sample_sc, nq, 8, 2, SSM_QW).transpose(0, 2, 3, 1, 4)
        outs["srs"].append(fin_s[:, :, 0].reshape(n_streams, n_groups, n_state))
        outs["sis"].append(fin_s[:, :, 1].reshape(n_streams, n_groups, n_state))

    st = {name: jnp.stack(vals) for name, vals in outs.items()}
    kp, vp, ks, vs = kv_out
    p_shape = (depth, batch, seq, n_heads, HEAD_W)
    s_shape = (depth, n_streams, dec_seq, n_heads, HEAD_W)
    return (x[:tp].reshape(batch, seq, d), x[tp:].reshape(n_streams, dec_seq, d),
            kp.reshape(p_shape), vp.reshape(p_shape), st["srp"], st["sip"],
            ks.reshape(s_shape), vs.reshape(s_shape), st["srs"], st["sis"])
```

```python
import functools
import math

import jax
import jax.numpy as jnp
from jax import lax
from jax.experimental import pallas as pl
from jax.experimental.pallas import tpu as pltpu

F32 = jnp.float32
BF16 = jnp.bfloat16

CHUNK = 64
HEAD_DIM = 64
HEAD_W = 2 * HEAD_DIM
SSM_GROUP = 16
SSM_STATE = 64
SSM_QG = 8
SSM_QW = SSM_QG * SSM_STATE
TOP_K = 4
SWIGLU_ALPHA = 1.702
SWIGLU_LIMIT = 7.0
EPS = 1e-6
NEG_INF = -1e30

TOK_TILE = 512
SSM_ROWS = 8 * CHUNK
ATT_TQ = 512
ATT_TKC = 2048
MOE_TILE = 256
VMEM_LIMIT = 56 * 1024 * 1024


def _cparams(sem):
    return pltpu.CompilerParams(dimension_semantics=sem, vmem_limit_bytes=VMEM_LIMIT)


def _dot(a, b):
    return jnp.dot(a, b, preferred_element_type=F32)


def _dot_t(a, b):
    return lax.dot_general(a, b, (((1,), (1,)), ((), ())), preferred_element_type=F32)


def _split_bf16(x):
    hi = x.astype(BF16)
    lo = (x - hi.astype(F32)).astype(BF16)
    return hi, lo


def _inproj_kernel(x_ref, g_ref, w_ref, gq_ref, gk_ref, seg_ref, *rest, n_prompt_tiles, n_prev):
    u_ref, q_ref, kb_ref, vb_ref, kp_ref, vp_ref, ks_ref, vs_ref = rest[n_prev:]
    i = pl.program_id(0)
    x = x_ref[...]
    ms = jnp.mean(x * x, axis=-1, keepdims=True)
    h = (x * lax.rsqrt(ms + EPS) * g_ref[...]).astype(BF16)
    proj = _dot(h, w_ref[...])
    w = u_ref.shape[-1]
    seg = seg_ref[...]

    def head_norm(z, g):
        hi, lo = _split_bf16(z * z)
        ms_ = _dot(hi, seg) + _dot(lo, seg)
        return z * lax.rsqrt(ms_ + EPS) * g

    u_ref[...] = proj[:, :w]
    qn = head_norm(proj[:, w:2 * w], gq_ref[...])
    q_ref[...] = (qn * (HEAD_DIM ** -0.5)).astype(BF16)
    kn = head_norm(proj[:, 2 * w:3 * w], gk_ref[...])
    kb_ref[...] = kn.astype(BF16)
    vv = proj[:, 3 * w:]
    vb_ref[...] = vv.astype(BF16)

    def emit(k_out, v_out):
        tm = kn.shape[0]
        nh = k_out.shape[0] // tm
        for hd in range(nh):
            k_out[pl.ds(hd, tm, stride=nh), :] = kn[:, hd * HEAD_W:(hd + 1) * HEAD_W]
            v_out[pl.ds(hd, tm, stride=nh), :] = vv[:, hd * HEAD_W:(hd + 1) * HEAD_W]

    @pl.when(i < n_prompt_tiles)
    def _():
        emit(kp_ref, vp_ref)

    @pl.when(i >= n_prompt_tiles)
    def _():
        emit(ks_ref, vs_ref)


def _inproj(x, g_mix, w_in_bf, g_q, g_k, layer, depth, n_prompt_tiles, prev):
    t, d = x.shape
    aw = w_in_bf.shape[1] // 4
    nh = aw // HEAD_W
    nrep = aw // HEAD_DIM
    gq = jnp.tile(g_q.astype(F32), nrep)[None]
    gk = jnp.tile(g_k.astype(F32), nrep)[None]
    ids = jnp.arange(aw) // HEAD_DIM
    seg = jnp.where(ids[:, None] == ids[None, :], 1.0 / HEAD_DIM, 0.0).astype(BF16)
    tm = TOK_TILE
    n_tiles = t // tm
    n_sample_tiles = n_tiles - n_prompt_tiles
    row = lambda i: (i, 0)
    fixed = lambda i: (0, 0)
    p_blk = pl.BlockSpec((None, tm * nh, HEAD_W),
                         lambda i: (layer * n_prompt_tiles + jnp.minimum(i, n_prompt_tiles - 1), 0, 0))
    s_blk = pl.BlockSpec((None, tm * nh, HEAD_W),
                         lambda i: (layer * n_sample_tiles + jnp.maximum(i - n_prompt_tiles, 0), 0, 0))
    p_shape = jax.ShapeDtypeStruct((depth * n_prompt_tiles, tm * nh, HEAD_W), F32)
    s_shape = jax.ShapeDtypeStruct((depth * n_sample_tiles, tm * nh, HEAD_W), F32)
    outs = [jax.ShapeDtypeStruct((t, aw), dt) for dt in (F32, BF16, BF16, BF16)] + [p_shape, p_shape, s_shape, s_shape]
    prev = () if prev is None else tuple(prev)
    n_in = 6
    kern = functools.partial(_inproj_kernel, n_prompt_tiles=n_prompt_tiles, n_prev=len(prev))
    return pl.pallas_call(
        kern,
        grid=(n_tiles,),
        in_specs=[pl.BlockSpec((tm, d), row), pl.BlockSpec((1, d), fixed),
                  pl.BlockSpec(w_in_bf.shape, fixed), pl.BlockSpec((1, aw), fixed),
                  pl.BlockSpec((1, aw), fixed), pl.BlockSpec((aw, aw), fixed)]
                 + [pl.BlockSpec(memory_space=pl.ANY)] * len(prev),
        out_specs=[pl.BlockSpec((tm, aw), row)] * 4 + [p_blk, p_blk, s_blk, s_blk],
        out_shape=outs,
        input_output_aliases={n_in + j: 4 + j for j in range(len(prev))},
        compiler_params=_cparams(("arbitrary",)),
        name="inproj",
    )(x, g_mix[None].astype(F32), w_in_bf, gq, gk, seg, *prev)


def _ssm_kernel(u_ref, wb_ref, wc_ref, ab_ref, a64_ref, pw_ref, d_ref, h0_ref,
                y_ref, fin_ref, uperm_ref, st_ref, carry_ref, *, n_prompt_sc, sc_per_seq):
    sc = pl.program_id(0)
    qb = pl.program_id(1)
    nsteps = CHUNK
    w = SSM_QW

    for t in range(nsteps):
        uperm_ref[t * 8:(t + 1) * 8, :] = u_ref[pl.ds(t, 8, stride=nsteps), :]
    up = uperm_ref[...]
    st_ref[...] = _dot(up.astype(BF16), wb_ref[0])

    ab = ab_ref[0]
    ar = jnp.broadcast_to(ab[:, :w], (8, w))
    ai = jnp.broadcast_to(ab[:, w:], (8, w))

    def scan_step(t, carry):
        hr, hi = carry
        r0 = pl.multiple_of(t * 8, 8)
        br = st_ref[pl.ds(r0, 8), :w]
        bi = st_ref[pl.ds(r0, 8), w:]
        nr = ar * hr - ai * hi + br
        ni = ar * hi + ai * hr + bi
        st_ref[pl.ds(r0, 8), :w] = nr
        st_ref[pl.ds(r0, 8), w:] = ni
        return nr, ni

    zero = jnp.zeros((8, w), F32)
    er, ei = lax.fori_loop(0, nsteps, scan_step, (zero, zero))

    a64 = a64_ref[0]
    a64r = a64[:, :w]
    a64i = a64[:, w:]
    is_sample = sc >= n_prompt_sc

    @pl.when(jnp.logical_or(is_sample, sc % sc_per_seq == 0))
    def _():
        carry_ref[qb] = jnp.zeros(carry_ref.shape[1:], F32)

    cin = carry_ref[qb]
    cr = cin[:, :w]
    ci = cin[:, w:]
    rows = lax.broadcasted_iota(jnp.int32, (8, w), 0)
    sr = jnp.zeros((8, w), F32)
    si = jnp.zeros((8, w), F32)
    for j in range(8):
        sr = jnp.where(rows == j, cr, sr)
        si = jnp.where(rows == j, ci, si)
        ejr = er[j:j + 1]
        eji = ei[j:j + 1]
        cr, ci = a64r * cr - a64i * ci + ejr, a64r * ci + a64i * cr + eji
    carry_ref[qb] = jnp.concatenate([cr, ci], axis=1)
    h0 = h0_ref[0, 0]
    given = (jnp.zeros((8, w), jnp.int32) + is_sample.astype(jnp.int32)) > 0
    sr = jnp.where(given, h0[:, :w], sr)
    si = jnp.where(given, h0[:, w:], si)
    fr = a64r * sr - a64i * si + er
    fi = a64r * si + a64i * sr + ei
    fin_ref[0, 0] = jnp.concatenate([fr, fi], axis=1)

    def fix_step(t, _):
        r0 = pl.multiple_of(t * 8, 8)
        p = pw_ref[0, pl.ds(t, 1), :]
        pr = p[:, :w]
        pi = p[:, w:]
        st_ref[pl.ds(r0, 8), :w] = st_ref[pl.ds(r0, 8), :w] + (pr * sr - pi * si)
        st_ref[pl.ds(r0, 8), w:] = st_ref[pl.ds(r0, 8), w:] + (pr * si + pi * sr)
        return 0

    lax.fori_loop(0, nsteps, fix_step, 0)

    y = _dot(st_ref[...].astype(BF16), wc_ref[0]) + up * d_ref[0]
    for t in range(nsteps):
        y_ref[pl.ds(t, 8, stride=nsteps), :] = y[t * 8:(t + 1) * 8, :]


def _ssm_tables(a_re, a_im, log_dt, b_re, b_im, c_re, c_im, d_skip):
    g, n = a_re.shape
    c = b_re.shape[-1]
    nq = g // SSM_QG
    dt = jnp.exp(log_dt)[:, None]
    za_re, za_im = dt * a_re, dt * a_im
    mag = jnp.exp(za_re)
    ab_re, ab_im = mag * jnp.cos(za_im), mag * jnp.sin(za_im)
    den = a_re * a_re + a_im * a_im
    n_re, n_im = ab_re - 1.0, ab_im
    f_re = (n_re * a_re + n_im * a_im) / den
    f_im = (n_im * a_re - n_re * a_im) / den
    bb_re = f_re[..., None] * b_re - f_im[..., None] * b_im
    bb_im = f_re[..., None] * b_im + f_im[..., None] * b_re
    eye = jnp.eye(SSM_QG, dtype=F32)

    def in_w(bb):
        bq = bb.reshape(nq, SSM_QG, n, c)
        return jnp.einsum('qgnc,gh->qgchn', bq, eye).reshape(nq, SSM_QG * c, SSM_QG * n)

    def out_w(cc):
        cq = cc.reshape(nq, SSM_QG, c, n)
        return jnp.einsum('qgcn,gh->qgnhc', cq, eye).reshape(nq, SSM_QG * n, SSM_QG * c)

    wb = jnp.concatenate([in_w(bb_re), in_w(bb_im)], axis=2).astype(BF16)
    wc = jnp.concatenate([out_w(c_re), out_w(-c_im)], axis=1).astype(BF16)

    def lanes(z):
        return jnp.moveaxis(z.reshape(z.shape[:-2] + (nq, SSM_QG * n)), -2, 0)

    def power(k):
        m = jnp.exp(k * za_re)
        return m * jnp.cos(k * za_im), m * jnp.sin(k * za_im)

    ab = jnp.concatenate([lanes(ab_re), lanes(ab_im)], axis=-1)[:, None]
    p64 = power(float(CHUNK))
    a64 = jnp.concatenate([lanes(p64[0]), lanes(p64[1])], axis=-1)[:, None]
    ks = jnp.arange(1, CHUNK + 1, dtype=F32)[:, None, None]
    pk = power(ks)
    pw = jnp.concatenate([lanes(pk[0]), lanes(pk[1])], axis=-1)
    dq = d_skip.reshape(nq, 1, SSM_QG * c).astype(F32)
    return wb, wc, ab, a64, pw, dq


def _ssm(u, tables, h0_all, n_prompt_sc, sc_per_seq):
    wb, wc, ab, a64, pw, dq = tables
    t, cw = u.shape
    nq = wb.shape[0]
    n_sc = t // SSM_ROWS
    sw = 2 * SSM_QW
    kern = functools.partial(_ssm_kernel, n_prompt_sc=n_prompt_sc, sc_per_seq=sc_per_seq)
    per_q = lambda s, q: (q, 0, 0)
    return pl.pallas_call(
        kern,
        grid=(n_sc, nq),
        in_specs=[pl.BlockSpec((SSM_ROWS, 128), lambda s, q: (s, q)),
                  pl.BlockSpec((1,) + wb.shape[1:], per_q), pl.BlockSpec((1,) + wc.shape[1:], per_q),
                  pl.BlockSpec((1, 1, sw), per_q), pl.BlockSpec((1, 1, sw), per_q),
                  pl.BlockSpec((1, CHUNK, sw), per_q), pl.BlockSpec((1, 1, 128), per_q),
                  pl.BlockSpec((1, 1, 8, sw), lambda s, q: (s, q, 0, 0))],
        out_specs=[pl.BlockSpec((SSM_ROWS, 128), lambda s, q: (s, q)),
                   pl.BlockSpec((1, 1, 8, sw), lambda s, q: (s, q, 0, 0))],
        out_shape=[jax.ShapeDtypeStruct((t, cw), F32),
                   jax.ShapeDtypeStruct((n_sc, nq, 8, sw), F32)],
        scratch_shapes=[pltpu.VMEM((SSM_ROWS, 128), F32), pltpu.VMEM((SSM_ROWS, sw), F32),
                        pltpu.VMEM((nq, 1, sw), F32)],
        compiler_params=_cparams(("arbitrary", "arbitrary")),
        name="ssm",
    )(u, wb, wc, ab, a64, pw, dq, h0_all)


def _stack_q(q):
    lane = lax.broadcasted_iota(jnp.int32, q.shape, 1)
    zero = jnp.zeros_like(q)
    return jnp.concatenate([jnp.where(lane < HEAD_DIM, q, zero), jnp.where(lane >= HEAD_DIM, q, zero)], axis=0)


def _online_update(s, v, m_ref, l_ref, acc_ref, fixed_max=False):
    if fixed_max:
        p = jnp.exp(s)
        l_ref[...] = l_ref[...] + jnp.sum(p, axis=-1, keepdims=True)
        acc_ref[...] = acc_ref[...] + _dot(p.astype(BF16), v)
        return
    m_old = m_ref[...]
    m_new = jnp.maximum(m_old, jnp.max(s, axis=-1, keepdims=True))
    alpha = jnp.exp(m_old - m_new)
    p = jnp.exp(s - m_new)
    l_ref[...] = alpha * l_ref[...] + jnp.sum(p, axis=-1, keepdims=True)
    acc_ref[...] = alpha * acc_ref[...] + _dot(p.astype(BF16), v)
    m_ref[...] = m_new


SCORE_BOUND_SCALE = 1.02 * HEAD_DIM ** 0.5
FIXED_MAX_LIMIT = 30.0


def _finish_head(m_ref, l_ref, acc_ref, lam, g, out_scale, tq):
    acc = acc_ref[...]
    l = l_ref[...]
    o = acc[:tq] / l[:tq] - lam * (acc[tq:] / l[tq:])
    ms = jnp.mean(o * o, axis=-1, keepdims=True)
    return o * lax.rsqrt(ms + EPS) * g * out_scale


def _attn_prompt_kernel(slope_ref, lam_ref, bound_ref, q_ref, k_ref, v_ref, g_ref, o_ref,
                        m_ref, l_ref, acc_ref, *, out_scale, fixed_max):
    h = pl.program_id(1)
    i = pl.program_id(2)
    tq = q_ref.shape[0]
    tk = tq
    slope = slope_ref[h]
    shift = bound_ref[0] if fixed_max else 0.0
    qq = _stack_q(q_ref[...])
    m_ref[...] = jnp.full(m_ref.shape, NEG_INF, F32)
    l_ref[...] = jnp.zeros(l_ref.shape, F32)
    acc_ref[...] = jnp.zeros(acc_ref.shape, F32)
    rq = lax.broadcasted_iota(jnp.int32, (2 * tq, 1), 0)
    rq = jnp.where(rq >= tq, rq - tq, rq)
    row_term = slope * (i * tq + rq).astype(F32) + shift

    def past_block(j):
        k0 = pl.multiple_of(j * tk, tk)
        k = k_ref[pl.ds(k0, tk), :]
        v = v_ref[pl.ds(k0, tk), :]
        kpos = (k0 + lax.broadcasted_iota(jnp.int32, (1, tk), 1)).astype(F32)
        s = (_dot_t(qq, k) + slope * kpos) - row_term
        _online_update(s, v, m_ref, l_ref, acc_ref, fixed_max)

    def pair(jj, _):
        past_block(2 * jj)
        past_block(2 * jj + 1)
        return 0

    lax.fori_loop(0, i >> 1, pair, 0)

    @pl.when((i & 1) == 1)
    def _():
        past_block(i - 1)

    k0 = pl.multiple_of(i * tk, tk)
    k = k_ref[pl.ds(k0, tk), :]
    v = v_ref[pl.ds(k0, tk), :]
    r = lax.broadcasted_iota(jnp.int32, (2 * tq, tk), 0)
    r = jnp.where(r >= tq, r - tq, r)
    c = lax.broadcasted_iota(jnp.int32, (2 * tq, tk), 1)
    bias = -slope * jnp.abs(r - c).astype(F32) - shift
    visible = (c >> 6) <= (r >> 6)
    s = jnp.where(visible, _dot_t(qq, k) + bias, NEG_INF)
    _online_update(s, v, m_ref, l_ref, acc_ref, fixed_max)
    o_ref[...] = _finish_head(m_ref, l_ref, acc_ref, lam_ref[0], g_ref[...], out_scale, tq).astype(o_ref.dtype)


def _attn_prompt(q, kb, vb, slopes, lam, bound, g_subln, out_scale, batch, seq):
    n_heads = q.shape[1] // HEAD_W
    tq = ATT_TQ
    nq = seq // tq
    smem = pl.BlockSpec(memory_space=pltpu.SMEM)

    def call(fixed_max):
        return pl.pallas_call(
            functools.partial(_attn_prompt_kernel, out_scale=out_scale, fixed_max=fixed_max),
            grid=(batch, n_heads, nq),
            in_specs=[smem, smem, smem,
                      pl.BlockSpec((tq, HEAD_W), lambda b, h, i: (b * nq + i, h)),
                      pl.BlockSpec((seq, HEAD_W), lambda b, h, i: (b, h)),
                      pl.BlockSpec((seq, HEAD_W), lambda b, h, i: (b, h)),
                      pl.BlockSpec((1, HEAD_W), lambda b, h, i: (0, 0))],
            out_specs=pl.BlockSpec((tq, HEAD_W), lambda b, h, i: (b * nq + i, h)),
            out_shape=jax.ShapeDtypeStruct((batch * seq, q.shape[1]), BF16),
            scratch_shapes=[pltpu.VMEM((2 * tq, 1), F32), pltpu.VMEM((2 * tq, 1), F32),
                            pltpu.VMEM((2 * tq, HEAD_W), F32)],
            compiler_params=_cparams(("parallel", "parallel", "arbitrary")),
            name="attn_prompt_fixed" if fixed_max else "attn_prompt",
        )(slopes, lam, bound, q, kb, vb, g_subln[None].astype(F32))

    return lax.cond(bound[0] <= FIXED_MAX_LIMIT, lambda: call(True), lambda: call(False))


def _attn_sample_kernel(slope_ref, lam_ref, q_ref, kn_ref, vn_ref, kc_ref, vc_ref, g_ref, o_ref,
                        m_ref, l_ref, acc_ref, *, out_scale, past_len):
    j = pl.program_id(1)
    nj = pl.num_programs(1)
    tq = q_ref.shape[0]
    n_heads = q_ref.shape[1] // HEAD_W
    tk = kc_ref.shape[0] // n_heads

    @pl.when(j == 0)
    def _():
        m_ref[...] = jnp.full(m_ref.shape, NEG_INF, F32)
        l_ref[...] = jnp.zeros(l_ref.shape, F32)
        acc_ref[...] = jnp.zeros(acc_ref.shape, F32)

    rq = lax.broadcasted_iota(jnp.int32, (2 * tq, 1), 0)
    rq = jnp.where(rq >= tq, rq - tq, rq)
    qpos = (past_len + rq).astype(F32)
    kpos = (j * tk + lax.broadcasted_iota(jnp.int32, (1, tk), 1)).astype(F32)
    for h in range(n_heads):
        cols = slice(h * HEAD_W, (h + 1) * HEAD_W)
        qq = _stack_q(q_ref[:, cols])
        k = kc_ref[pl.ds(h, tk, stride=n_heads), :].astype(BF16)
        v = vc_ref[pl.ds(h, tk, stride=n_heads), :].astype(BF16)
        slope = slope_ref[h]
        s = (_dot_t(qq, k) + slope * kpos) - slope * qpos
        _online_update(s, v, m_ref.at[h], l_ref.at[h], acc_ref.at[h])

    @pl.when(j == nj - 1)
    def _():
        r = lax.broadcasted_iota(jnp.int32, (2 * tq, tq), 0)
        r = jnp.where(r >= tq, r - tq, r)
        c = lax.broadcasted_iota(jnp.int32, (2 * tq, tq), 1)
        dist = jnp.abs(r - c).astype(F32)
        for h in range(n_heads):
            cols = slice(h * HEAD_W, (h + 1) * HEAD_W)
            qq = _stack_q(q_ref[:, cols])
            s = _dot_t(qq, kn_ref[:, cols]) - slope_ref[h] * dist
            _online_update(s, vn_ref[:, cols], m_ref.at[h], l_ref.at[h], acc_ref.at[h])
            o_ref[:, cols] = _finish_head(m_ref.at[h], l_ref.at[h], acc_ref.at[h], lam_ref[0],
                                          g_ref[...], out_scale, tq).astype(o_ref.dtype)


def _attn_sample(q, kb, vb, cache_k, cache_v, layer, slopes, lam, g_subln, out_scale, row0):
    depth, n_streams, past_len, n_heads, _ = cache_k.shape
    aw = n_heads * HEAD_W
    tq = CHUNK
    tk = min(ATT_TKC, past_len)
    blk0 = row0 // tq
    smem = pl.BlockSpec(memory_space=pltpu.SMEM)
    new = pl.BlockSpec((tq, aw), lambda s, j: (blk0 + s, 0))
    past = pl.BlockSpec((None, None, tk * n_heads, HEAD_W), lambda s, j: (layer, s, j, 0))
    ck = cache_k.reshape(depth, n_streams, past_len * n_heads, HEAD_W)
    cv = cache_v.reshape(depth, n_streams, past_len * n_heads, HEAD_W)

    return pl.pallas_call(
        functools.partial(_attn_sample_kernel, out_scale=out_scale, past_len=past_len),
        grid=(n_streams, past_len // tk),
        in_specs=[smem, smem, new, new, new, past, past,
                  pl.BlockSpec((1, HEAD_W), lambda s, j: (0, 0))],
        out_specs=pl.BlockSpec((tq, aw), lambda s, j: (s, 0)),
        out_shape=jax.ShapeDtypeStruct((n_streams * tq, aw), BF16),
        scratch_shapes=[pltpu.VMEM((n_heads, 2 * tq, 1), F32), pltpu.VMEM((n_heads, 2 * tq, 1), F32),
                        pltpu.VMEM((n_heads, 2 * tq, HEAD_W), F32)],
        compiler_params=_cparams(("parallel", "arbitrary")),
        name="attn_sample",
    )(slopes, lam, q, kb, vb, ck, cv, g_subln[None].astype(F32))


def _outproj_kernel(ys_ref, oa_ref, x_ref, wglu_ref, gs_ref, wtop_ref, wbot_ref, gf_ref,
                    wrh_ref, wrl_ref, br_ref, x1_ref, h2_ref, lg_ref):
    y = ys_ref[...]
    y = 0.5 * y * (1.0 + jnp.tanh(math.sqrt(2.0 / math.pi) * (y + 0.044715 * (y * y * y))))
    z = _dot(y.astype(BF16), wglu_ref[...])
    y = y * (1.0 / (1.0 + jnp.exp(-z)))
    ms = jnp.mean(y * y, axis=-1, keepdims=True)
    y = y * lax.rsqrt(ms + EPS) * gs_ref[...]
    x1 = x_ref[...] + _dot(y.astype(BF16), wtop_ref[...]) + _dot(oa_ref[...], wbot_ref[...])
    x1_ref[...] = x1
    ms = jnp.mean(x1 * x1, axis=-1, keepdims=True)
    h2 = x1 * lax.rsqrt(ms + EPS) * gf_ref[...]
    tm, d = h2.shape
    nblk = d // 128
    for j in range(nblk):
        h2_ref[pl.ds(j, tm, stride=nblk), :] = h2[:, j * 128:(j + 1) * 128]
    hi, lo = _split_bf16(h2)
    wrh = wrh_ref[...]
    lg_ref[...] = _dot(hi, wrh) + _dot(lo, wrh) + _dot(hi, wrl_ref[...]) + br_ref[...]


def _outproj(ys, oa, x, w_glu, g_ssm, w_out, g_ffn, w_router, b_router):
    t, d = x.shape
    sw = ys.shape[1]
    n_exp = w_router.shape[1]
    lw = max(128, n_exp)
    wr = jnp.zeros((d, lw), F32).at[:, :n_exp].set(w_router)
    wrh, wrl = _split_bf16(wr)
    br = jnp.zeros((1, lw), F32).at[0, :n_exp].set(b_router)
    w_out_bf = w_out.astype(BF16)
    tm = TOK_TILE
    row = lambda i: (i, 0)
    fixed = lambda i: (0, 0)
    return pl.pallas_call(
        _outproj_kernel,
        grid=(t // tm,),
        in_specs=[pl.BlockSpec((tm, sw), row), pl.BlockSpec((tm, d - sw), row), pl.BlockSpec((tm, d), row),
                  pl.BlockSpec((sw, sw), fixed), pl.BlockSpec((1, sw), fixed),
                  pl.BlockSpec((sw, d), fixed), pl.BlockSpec((d - sw, d), fixed), pl.BlockSpec((1, d), fixed),
                  pl.BlockSpec((d, lw), fixed), pl.BlockSpec((d, lw), fixed), pl.BlockSpec((1, lw), fixed)],
        out_specs=[pl.BlockSpec((tm, d), row), pl.BlockSpec((tm * (d // 128), 128), row),
                   pl.BlockSpec((tm, lw), row)],
        out_shape=[jax.ShapeDtypeStruct((t, d), F32), jax.ShapeDtypeStruct((t * (d // 128), 128), F32),
                   jax.ShapeDtypeStruct((t, lw), F32)],
        compiler_params=_cparams(("parallel",)),
        name="outproj",
    )(ys, oa, x, w_glu.astype(BF16), g_ssm[None].astype(F32), w_out_bf[:sw], w_out_bf[sw:],
      g_ffn[None].astype(F32), wrh, wrl, br)


def _moe_kernel(te_ref, nt_ref, src_ref, nxt_ref, dprev_ref, h2_hbm, wgu_ref, bgu_ref, wd_ref, bd_ref,
                out_hbm, xbuf, obuf, wgu_bf, wd_bf, gsem, ssem):
    i = pl.program_id(0)
    nt = nt_ref[0]
    slot = i % 2
    other = 1 - slot
    tm = src_ref.shape[2]
    nblk = xbuf.shape[1] // tm

    def token_rows(ref, first_row):
        return ref.at[pl.ds(pl.multiple_of(first_row, nblk), nblk), :]

    def start_gather(idx_ref, s):
        def one(r, _):
            pltpu.make_async_copy(token_rows(h2_hbm, idx_ref[0, 0, r]), token_rows(xbuf.at[s], r * nblk),
                                  gsem.at[s]).start()
            return 0
        lax.fori_loop(0, tm, one, 0, unroll=8)

    def wait_gather(s):
        pltpu.make_async_copy(xbuf.at[s], xbuf.at[s], gsem.at[s]).wait()

    def start_scatter(idx_ref, s):
        def one(r, _):
            pltpu.make_async_copy(token_rows(obuf.at[s], r * nblk), token_rows(out_hbm, idx_ref[0, 0, r]),
                                  ssem.at[0]).start()
            return 0
        lax.fori_loop(0, tm, one, 0, unroll=8)

    def wait_scatter(s):
        pltpu.make_async_copy(obuf.at[s], obuf.at[s], ssem.at[0]).wait()

    @pl.when(i == 0)
    def _():
        obuf[...] = jnp.zeros(obuf.shape, obuf.dtype)
        start_gather(src_ref, 0)

    @pl.when(i < nt)
    def _():
        wait_gather(slot)

    prev = te_ref[jnp.maximum(i - 1, 0)]
    new_expert = jnp.logical_or(i == 0, te_ref[i] != prev)

    @pl.when(jnp.logical_and(i < nt, new_expert))
    def _():
        wgu_bf[...] = wgu_ref[0].astype(BF16)
        wd_bf[...] = wd_ref[0].astype(BF16)

    @pl.when(i < nt)
    def _():
        start_gather(nxt_ref, other)
        start_scatter(dprev_ref, other)
        dff = wd_bf.shape[0]
        xin = xbuf.at[slot]
        x = jnp.concatenate([xin[pl.ds(j, tm, stride=nblk), :].astype(BF16) for j in range(nblk)], axis=1)
        gu = _dot(x, wgu_bf[...]) + bgu_ref[0]
        x_glu = jnp.minimum(gu[:, :dff], SWIGLU_LIMIT)
        x_lin = jnp.clip(gu[:, dff:], -SWIGLU_LIMIT, SWIGLU_LIMIT)
        hdn = x_glu * (1.0 / (1.0 + jnp.exp(-SWIGLU_ALPHA * x_glu))) * (x_lin + 1.0)
        out = _dot(hdn.astype(BF16), wd_bf[...]) + bd_ref[0]
        res = obuf.at[slot]
        for j in range(nblk):
            res[pl.ds(j, tm, stride=nblk), :] = out[:, j * 128:(j + 1) * 128]
        wait_scatter(other)

    @pl.when(i == nt)
    def _():
        wait_gather(slot)
        start_scatter(dprev_ref, other)
        wait_scatter(other)


def _moe_rows(h2, route, layer, w_gate_up, b_gate_up, w_down, b_down):
    tile_expert, n_used, src, dst = route
    depth, n_exp, d, dgu = w_gate_up.shape
    nblk = d // 128
    t = h2.shape[0] // nblk
    dff = w_down.shape[2]
    tm = MOE_TILE
    n_tiles = tile_expert.shape[0]
    by_e = lambda i, te, nt: (layer * n_exp + te[i], 0, 0)
    smem = lambda f: pl.BlockSpec((1, 1, tm), f, memory_space=pltpu.SMEM)
    any_ = pl.BlockSpec(memory_space=pl.ANY)
    return pl.pallas_call(
        _moe_kernel,
        grid_spec=pltpu.PrefetchScalarGridSpec(
            num_scalar_prefetch=2,
            grid=(n_tiles,),
            in_specs=[smem(lambda i, te, nt: (i, 0, 0)), smem(lambda i, te, nt: (i + 1, 0, 0)),
                      smem(lambda i, te, nt: (i, 0, 0)), any_,
                      pl.BlockSpec((1, d, dgu), by_e), pl.BlockSpec((1, 1, dgu), by_e),
                      pl.BlockSpec((1, dff, d), by_e), pl.BlockSpec((1, 1, d), by_e)],
            out_specs=any_,
            scratch_shapes=[pltpu.VMEM((2, tm * nblk, 128), F32), pltpu.VMEM((2, tm * nblk, 128), F32),
                            pltpu.VMEM((d, dgu), BF16), pltpu.VMEM((dff, d), BF16),
                            pltpu.SemaphoreType.DMA((2,)), pltpu.SemaphoreType.DMA((1,))]),
        out_shape=jax.ShapeDtypeStruct(((TOP_K * t + tm) * nblk, 128), F32),
        compiler_params=_cparams(("arbitrary",)),
        name="moe",
    )(tile_expert, n_used, src, src, dst, h2, w_gate_up.reshape(depth * n_exp, d, dgu),
      b_gate_up.reshape(depth * n_exp, 1, dgu), w_down.reshape(depth * n_exp, dff, d),
      b_down.reshape(depth * n_exp, 1, d))


def _moe_route(logits, n_exp, nblk):
    t = logits.shape[0]
    tm = MOE_TILE
    top_val, top_idx = lax.top_k(logits, TOP_K)
    gates = jax.nn.softmax(top_val, axis=-1)
    n_assign = t * TOP_K
    e_flat = top_idx.T.reshape(-1).astype(jnp.int32)
    order = jnp.argsort(e_flat).astype(jnp.int32)
    experts = jnp.arange(n_exp, dtype=jnp.int32)
    counts = jnp.sum((e_flat[:, None] == experts[None, :]).astype(jnp.int32), axis=0)
    starts = jnp.cumsum(counts) - counts
    tiles_e = (counts + tm - 1) // tm
    tile_end = jnp.cumsum(tiles_e)
    tile_beg = tile_end - tiles_e
    n_used = tile_end[-1]
    n_tiles = n_assign // tm + n_exp + 1
    tile = jnp.arange(n_tiles, dtype=jnp.int32)
    tile_c = jnp.minimum(tile, n_used - 1)
    te = jnp.sum((tile_end[None, :] <= tile_c[:, None]).astype(jnp.int32), axis=1)
    te = jnp.minimum(te, n_exp - 1)
    first = starts[te] + (tile - tile_beg[te]) * tm
    n_valid = jnp.where(tile < n_used, jnp.clip(counts[te] - (tile - tile_beg[te]) * tm, 0, tm), 0)
    r = jnp.arange(tm, dtype=jnp.int32)
    valid = r[None, :] < n_valid[:, None]
    a = order[jnp.clip(first[:, None] + r[None, :], 0, n_assign - 1)]
    src = jnp.where(valid, a % t, 0)
    dst = jnp.where(valid, a, n_assign + r[None, :])
    spare = jnp.broadcast_to(n_assign + r[None, :], (1, tm))
    src = jnp.concatenate([src, jnp.zeros((1, tm), jnp.int32)], axis=0)[:, None, :] * nblk
    dst = jnp.concatenate([spare, dst], axis=0)[:, None, :] * nblk
    return gates, (te.astype(jnp.int32), n_used.astype(jnp.int32)[None], src, dst)


def _combine_kernel(x1_ref, g_ref, o0_ref, o1_ref, o2_ref, o3_ref, x2_ref):
    g = g_ref[...]
    tm, d = x1_ref.shape
    nblk = d // 128
    for j in range(nblk):
        cols = slice(j * 128, (j + 1) * 128)
        acc = x1_ref[:, cols]
        for k, o_ref in enumerate((o0_ref, o1_ref, o2_ref, o3_ref)):
            acc = acc + g[:, k:k + 1] * o_ref[pl.ds(j, tm, stride=nblk), :]
        x2_ref[:, cols] = acc


def _combine(x1, gates, out_rows):
    t, d = x1.shape
    tm = TOK_TILE
    nt = t // tm
    rows = lambda k: pl.BlockSpec((tm * (d // 128), 128), lambda i: (k * nt + i, 0))
    return pl.pallas_call(
        _combine_kernel,
        grid=(nt,),
        in_specs=[pl.BlockSpec((tm, d), lambda i: (i, 0)), pl.BlockSpec((tm, TOP_K), lambda i: (i, 0)),
                  rows(0), rows(1), rows(2), rows(3)],
        out_specs=pl.BlockSpec((tm, d), lambda i: (i, 0)),
        out_shape=jax.ShapeDtypeStruct((t, d), F32),
        compiler_params=_cparams(("parallel",)),
        name="combine",
    )(x1, gates, out_rows, out_rows, out_rows, out_rows)


def _lambda_init(layer):
    return 0.8 - 0.6 * math.exp(-0.3 * layer)


def kernel(x_prompt, x_sample, cache_k, cache_v, state_ssm_re, state_ssm_im, g_mix, w_in, ssm_a_re, ssm_a_im, ssm_log_dt, ssm_b_re, ssm_b_im, ssm_c_re, ssm_c_im, ssm_d, w_glu, g_ssm_out, g_q, g_k, lambda_q1, lambda_k1, lambda_q2, lambda_k2, g_subln, w_out, g_ffn, w_router, b_router, w_gate_up, b_gate_up, w_down, b_down):
    batch, seq, d = x_prompt.shape
    n_streams, dec_seq, _ = x_sample.shape
    depth = w_in.shape[0]
    past_len = cache_k.shape[2]
    n_heads = cache_k.shape[3]
    aw = n_heads * HEAD_W
    n_groups, n_state = ssm_a_re.shape[1:]
    nq = n_groups // SSM_QG
    assert dec_seq == CHUNK and seq % SSM_ROWS == 0 and n_streams % 8 == 0 and n_state == SSM_STATE
    tp = batch * seq
    ts = n_streams * dec_seq
    n_prompt_sc = tp // SSM_ROWS
    sc_per_seq = seq // SSM_ROWS
    n_sample_sc = ts // SSM_ROWS
    slopes = jnp.asarray([2.0 ** (-8.0 * (h + 1) / n_heads) for h in range(n_heads)], F32)

    x = jnp.concatenate([x_prompt.reshape(tp, d), x_sample.reshape(ts, d)], axis=0)
    outs = {name: [] for name in ("srp", "sip", "srs", "sis")}
    kv_out = None
    n_exp = w_router.shape[2]
    for l in range(depth):
        u, q, kb, vb, *kv_out = _inproj(x, g_mix[l], w_in[l].astype(BF16), g_q[l], g_k[l], l, depth,
                                        tp // TOK_TILE, kv_out)

        tables = _ssm_tables(ssm_a_re[l], ssm_a_im[l], ssm_log_dt[l], ssm_b_re[l], ssm_b_im[l],
                             ssm_c_re[l], ssm_c_im[l], ssm_d[l])

        def state_lanes(z):
            return z.reshape(n_sample_sc, 8, nq, SSM_QW).transpose(0, 2, 1, 3)

        h0_s = jnp.concatenate([state_lanes(state_ssm_re[l]), state_lanes(state_ssm_im[l])], axis=-1)
        h0_all = jnp.concatenate([jnp.zeros((n_prompt_sc,) + h0_s.shape[1:], F32), h0_s], axis=0)
        ys, fin = _ssm(u, tables, h0_all, n_prompt_sc, sc_per_seq)

        lam_init = _lambda_init(l)
        lam = (jnp.exp(jnp.sum(lambda_q1[l].astype(F32) * lambda_k1[l].astype(F32)))
               - jnp.exp(jnp.sum(lambda_q2[l].astype(F32) * lambda_k2[l].astype(F32))) + lam_init)[None]
        out_scale = 1.0 - lam_init
        bound = (SCORE_BOUND_SCALE * jnp.max(jnp.abs(g_q[l].astype(F32)))
                 * jnp.max(jnp.abs(g_k[l].astype(F32))))[None]
        o_p = _attn_prompt(q, kb, vb, slopes, lam, bound, g_subln[l], out_scale, batch, seq)
        o_s = _attn_sample(q, kb, vb, cache_k, cache_v, l, slopes, lam, g_subln[l], out_scale, tp)
        oa = jnp.concatenate([o_p, o_s], axis=0)

        x1, h2, logits = _outproj(ys, oa, x, w_glu[l], g_ssm_out[l], w_out[l], g_ffn[l],
                                  w_router[l], b_router[l])
        gates, route = _moe_route(logits[:, :n_exp], n_exp, d // 128)
        x = _combine(x1, gates, _moe_rows(h2, route, l, w_gate_up, b_gate_up, w_down, b_down))

        fin_p = fin[:n_prompt_sc].reshape(batch, sc_per_seq, nq, 8, 2, SSM_QW)[:, -1, :, -1]
        outs["srp"].append(fin_p[:, :, 0].reshape(batch, n_groups, n_state))
        outs["sip"].append(fin_p[:, :, 1].reshape(batch, n_groups, n_state))
        fin_s = fin[n_prompt_sc:].reshape(n_sample_sc, nq, 8, 2, SSM_QW).transpose(0, 2, 3, 1, 4)
        outs["srs"].append(fin_s[:, :, 0].reshape(n_streams, n_groups, n_state))
        outs["sis"].append(fin_s[:, :, 1].reshape(n_streams, n_groups, n_state))

    st = {name: jnp.stack(vals) for name, vals in outs.items()}
    kp, vp, ks, vs = kv_out
    p_shape = (depth, batch, seq, n_heads, HEAD_W)
    s_shape = (depth, n_streams, dec_seq, n_heads, HEAD_W)
    return (x[:tp].reshape(batch, seq, d), x[tp:].reshape(n_streams, dec_seq, d),
            kp.reshape(p_shape), vp.reshape(p_shape), st["srp"], st["sip"],
            ks.reshape(s_shape), vs.reshape(s_shape), st["srs"], st["sis"])
```

```python
import functools
import math

import jax
import jax.numpy as jnp
from jax import lax
from jax.experimental import pallas as pl
from jax.experimental.pallas import tpu as pltpu

F32 = jnp.float32
BF16 = jnp.bfloat16

CHUNK = 64
HEAD_DIM = 64
HEAD_W = 2 * HEAD_DIM
SSM_GROUP = 16
SSM_STATE = 64
SSM_QG = 8
SSM_QW = SSM_QG * SSM_STATE
TOP_K = 4
SWIGLU_ALPHA = 1.702
SWIGLU_LIMIT = 7.0
EPS = 1e-6
NEG_INF = -1e30

TOK_TILE = 512
SSM_ROWS = 8 * CHUNK
ATT_TQ = 512
ATT_TKC = 2048
MOE_TILE = 256
VMEM_LIMIT = 56 * 1024 * 1024


def _cparams(sem):
    return pltpu.CompilerParams(dimension_semantics=sem, vmem_limit_bytes=VMEM_LIMIT)


def _dot(a, b):
    return jnp.dot(a, b, preferred_element_type=F32)


def _dot_t(a, b):
    return lax.dot_general(a, b, (((1,), (1,)), ((), ())), preferred_element_type=F32)


def _split_bf16(x):
    hi = x.astype(BF16)
    lo = (x - hi.astype(F32)).astype(BF16)
    return hi, lo


def _inproj_kernel(x_ref, g_ref, w_ref, gq_ref, gk_ref, seg_ref, *rest, n_prompt_tiles, n_prev):
    u_ref, q_ref, kb_ref, vb_ref, kp_ref, vp_ref, ks_ref, vs_ref = rest[n_prev:]
    i = pl.program_id(0)
    x = x_ref[...]
    ms = jnp.mean(x * x, axis=-1, keepdims=True)
    h = (x * lax.rsqrt(ms + EPS) * g_ref[...]).astype(BF16)
    proj = _dot(h, w_ref[...])
    w = u_ref.shape[-1]
    seg = seg_ref[...]

    def head_norm(z, g):
        hi, lo = _split_bf16(z * z)
        ms_ = _dot(hi, seg) + _dot(lo, seg)
        return z * lax.rsqrt(ms_ + EPS) * g

    u_ref[...] = proj[:, :w]
    qn = head_norm(proj[:, w:2 * w], gq_ref[...])
    q_ref[...] = (qn * (HEAD_DIM ** -0.5)).astype(BF16)
    kn = head_norm(proj[:, 2 * w:3 * w], gk_ref[...])
    kb_ref[...] = kn.astype(BF16)
    vv = proj[:, 3 * w:]
    vb_ref[...] = vv.astype(BF16)

    def emit(k_out, v_out):
        tm = kn.shape[0]
        nh = k_out.shape[0] // tm
        for hd in range(nh):
            k_out[pl.ds(hd, tm, stride=nh), :] = kn[:, hd * HEAD_W:(hd + 1) * HEAD_W]
            v_out[pl.ds(hd, tm, stride=nh), :] = vv[:, hd * HEAD_W:(hd + 1) * HEAD_W]

    @pl.when(i < n_prompt_tiles)
    def _():
        emit(kp_ref, vp_ref)

    @pl.when(i >= n_prompt_tiles)
    def _():
        emit(ks_ref, vs_ref)


def _inproj(x, g_mix, w_in_bf, g_q, g_k, layer, depth, n_prompt_tiles, prev):
    t, d = x.shape
    aw = w_in_bf.shape[1] // 4
    nh = aw // HEAD_W
    nrep = aw // HEAD_DIM
    gq = jnp.tile(g_q.astype(F32), nrep)[None]
    gk = jnp.tile(g_k.astype(F32), nrep)[None]
    ids = jnp.arange(aw) // HEAD_DIM
    seg = jnp.where(ids[:, None] == ids[None, :], 1.0 / HEAD_DIM, 0.0).astype(BF16)
    tm = TOK_TILE
    n_tiles = t // tm
    n_sample_tiles = n_tiles - n_prompt_tiles
    row = lambda i: (i, 0)
    fixed = lambda i: (0, 0)
    p_blk = pl.BlockSpec((None, tm * nh, HEAD_W),
                         lambda i: (layer * n_prompt_tiles + jnp.minimum(i, n_prompt_tiles - 1), 0, 0))
    s_blk = pl.BlockSpec((None, tm * nh, HEAD_W),
                         lambda i: (layer * n_sample_tiles + jnp.maximum(i - n_prompt_tiles, 0), 0, 0))
    p_shape = jax.ShapeDtypeStruct((depth * n_prompt_tiles, tm * nh, HEAD_W), F32)
    s_shape = jax.ShapeDtypeStruct((depth * n_sample_tiles, tm * nh, HEAD_W), F32)
    outs = [jax.ShapeDtypeStruct((t, aw), dt) for dt in (F32, BF16, BF16, BF16)] + [p_shape, p_shape, s_shape, s_shape]
    prev = () if prev is None else tuple(prev)
    n_in = 6
    kern = functools.partial(_inproj_kernel, n_prompt_tiles=n_prompt_tiles, n_prev=len(prev))
    return pl.pallas_call(
        kern,
        grid=(n_tiles,),
        in_specs=[pl.BlockSpec((tm, d), row), pl.BlockSpec((1, d), fixed),
                  pl.BlockSpec(w_in_bf.shape, fixed), pl.BlockSpec((1, aw), fixed),
                  pl.BlockSpec((1, aw), fixed), pl.BlockSpec((aw, aw), fixed)]
                 + [pl.BlockSpec(memory_space=pl.ANY)] * len(prev),
        out_specs=[pl.BlockSpec((tm, aw), row)] * 4 + [p_blk, p_blk, s_blk, s_blk],
        out_shape=outs,
        input_output_aliases={n_in + j: 4 + j for j in range(len(prev))},
        compiler_params=_cparams(("arbitrary",)),
        name="inproj",
    )(x, g_mix[None].astype(F32), w_in_bf, gq, gk, seg, *prev)


def _ssm_kernel(u_ref, wb_ref, wc_ref, ab_ref, a64_ref, pw_ref, d_ref, h0_ref,
                y_ref, fin_ref, uperm_ref, st_ref, carry_ref, *, n_prompt_sc, sc_per_seq):
    sc = pl.program_id(0)
    qb = pl.program_id(1)
    nsteps = CHUNK
    w = SSM_QW

    for t in range(nsteps):
        uperm_ref[t * 8:(t + 1) * 8, :] = u_ref[pl.ds(t, 8, stride=nsteps), :]
    up = uperm_ref[...]
    st_ref[...] = _dot(up.astype(BF16), wb_ref[0])

    ab = ab_ref[0]
    ar = jnp.broadcast_to(ab[:, :w], (8, w))
    ai = jnp.broadcast_to(ab[:, w:], (8, w))

    def scan_step(t, carry):
        hr, hi = carry
        r0 = pl.multiple_of(t * 8, 8)
        br = st_ref[pl.ds(r0, 8), :w]
        bi = st_ref[pl.ds(r0, 8), w:]
        nr = ar * hr - ai * hi + br
        ni = ar * hi + ai * hr + bi
        st_ref[pl.ds(r0, 8), :w] = nr
        st_ref[pl.ds(r0, 8), w:] = ni
        return nr, ni

    zero = jnp.zeros((8, w), F32)
    er, ei = lax.fori_loop(0, nsteps, scan_step, (zero, zero), unroll=4)

    a64 = a64_ref[0]
    a64r = a64[:, :w]
    a64i = a64[:, w:]
    is_sample = sc >= n_prompt_sc

    @pl.when(jnp.logical_or(is_sample, sc % sc_per_seq == 0))
    def _():
        carry_ref[qb] = jnp.zeros(carry_ref.shape[1:], F32)

    cin = carry_ref[qb]
    cr = cin[:, :w]
    ci = cin[:, w:]
    rows = lax.broadcasted_iota(jnp.int32, (8, w), 0)
    sr = jnp.zeros((8, w), F32)
    si = jnp.zeros((8, w), F32)
    for j in range(8):
        sr = jnp.where(rows == j, cr, sr)
        si = jnp.where(rows == j, ci, si)
        ejr = er[j:j + 1]
        eji = ei[j:j + 1]
        cr, ci = a64r * cr - a64i * ci + ejr, a64r * ci + a64i * cr + eji
    carry_ref[qb] = jnp.concatenate([cr, ci], axis=1)
    h0 = h0_ref[0, 0]
    given = (jnp.zeros((8, w), jnp.int32) + is_sample.astype(jnp.int32)) > 0
    sr = jnp.where(given, h0[:, :w], sr)
    si = jnp.where(given, h0[:, w:], si)
    fr = a64r * sr - a64i * si + er
    fi = a64r * si + a64i * sr + ei
    fin_ref[0, 0] = jnp.concatenate([fr, fi], axis=1)

    def fix_step(t, _):
        r0 = pl.multiple_of(t * 8, 8)
        p = pw_ref[0, pl.ds(t, 1), :]
        pr = p[:, :w]
        pi = p[:, w:]
        st_ref[pl.ds(r0, 8), :w] = st_ref[pl.ds(r0, 8), :w] + (pr * sr - pi * si)
        st_ref[pl.ds(r0, 8), w:] = st_ref[pl.ds(r0, 8), w:] + (pr * si + pi * sr)
        return 0

    lax.fori_loop(0, nsteps, fix_step, 0, unroll=4)

    y = _dot(st_ref[...].astype(BF16), wc_ref[0]) + up * d_ref[0]
    for t in range(nsteps):
        y_ref[pl.ds(t, 8, stride=nsteps), :] = y[t * 8:(t + 1) * 8, :]


def _ssm_tables(a_re, a_im, log_dt, b_re, b_im, c_re, c_im, d_skip):
    g, n = a_re.shape
    c = b_re.shape[-1]
    nq = g // SSM_QG
    dt = jnp.exp(log_dt)[:, None]
    za_re, za_im = dt * a_re, dt * a_im
    mag = jnp.exp(za_re)
    ab_re, ab_im = mag * jnp.cos(za_im), mag * jnp.sin(za_im)
    den = a_re * a_re + a_im * a_im
    n_re, n_im = ab_re - 1.0, ab_im
    f_re = (n_re * a_re + n_im * a_im) / den
    f_im = (n_im * a_re - n_re * a_im) / den
    bb_re = f_re[..., None] * b_re - f_im[..., None] * b_im
    bb_im = f_re[..., None] * b_im + f_im[..., None] * b_re
    eye = jnp.eye(SSM_QG, dtype=F32)

    def in_w(bb):
        bq = bb.reshape(nq, SSM_QG, n, c)
        return jnp.einsum('qgnc,gh->qgchn', bq, eye).reshape(nq, SSM_QG * c, SSM_QG * n)

    def out_w(cc):
        cq = cc.reshape(nq, SSM_QG, c, n)
        return jnp.einsum('qgcn,gh->qgnhc', cq, eye).reshape(nq, SSM_QG * n, SSM_QG * c)

    wb = jnp.concatenate([in_w(bb_re), in_w(bb_im)], axis=2).astype(BF16)
    wc = jnp.concatenate([out_w(c_re), out_w(-c_im)], axis=1).astype(BF16)

    def lanes(z):
        return jnp.moveaxis(z.reshape(z.shape[:-2] + (nq, SSM_QG * n)), -2, 0)

    def power(k):
        m = jnp.exp(k * za_re)
        return m * jnp.cos(k * za_im), m * jnp.sin(k * za_im)

    ab = jnp.concatenate([lanes(ab_re), lanes(ab_im)], axis=-1)[:, None]
    p64 = power(float(CHUNK))
    a64 = jnp.concatenate([lanes(p64[0]), lanes(p64[1])], axis=-1)[:, None]
    ks = jnp.arange(1, CHUNK + 1, dtype=F32)[:, None, None]
    pk = power(ks)
    pw = jnp.concatenate([lanes(pk[0]), lanes(pk[1])], axis=-1)
    dq = d_skip.reshape(nq, 1, SSM_QG * c).astype(F32)
    return wb, wc, ab, a64, pw, dq


def _ssm(u, tables, h0_all, n_prompt_sc, sc_per_seq):
    wb, wc, ab, a64, pw, dq = tables
    t, cw = u.shape
    nq = wb.shape[0]
    n_sc = t // SSM_ROWS
    sw = 2 * SSM_QW
    kern = functools.partial(_ssm_kernel, n_prompt_sc=n_prompt_sc, sc_per_seq=sc_per_seq)
    per_q = lambda s, q: (q, 0, 0)
    return pl.pallas_call(
        kern,
        grid=(n_sc, nq),
        in_specs=[pl.BlockSpec((SSM_ROWS, 128), lambda s, q: (s, q)),
                  pl.BlockSpec((1,) + wb.shape[1:], per_q), pl.BlockSpec((1,) + wc.shape[1:], per_q),
                  pl.BlockSpec((1, 1, sw), per_q), pl.BlockSpec((1, 1, sw), per_q),
                  pl.BlockSpec((1, CHUNK, sw), per_q), pl.BlockSpec((1, 1, 128), per_q),
                  pl.BlockSpec((1, 1, 8, sw), lambda s, q: (s, q, 0, 0))],
        out_specs=[pl.BlockSpec((SSM_ROWS, 128), lambda s, q: (s, q)),
                   pl.BlockSpec((1, 1, 8, sw), lambda s, q: (s, q, 0, 0))],
        out_shape=[jax.ShapeDtypeStruct((t, cw), F32),
                   jax.ShapeDtypeStruct((n_sc, nq, 8, sw), F32)],
        scratch_shapes=[pltpu.VMEM((SSM_ROWS, 128), F32), pltpu.VMEM((SSM_ROWS, sw), F32),
                        pltpu.VMEM((nq, 1, sw), F32)],
        compiler_params=_cparams(("arbitrary", "arbitrary")),
        name="ssm",
    )(u, wb, wc, ab, a64, pw, dq, h0_all)


def _stack_q(q):
    lane = lax.broadcasted_iota(jnp.int32, q.shape, 1)
    zero = jnp.zeros_like(q)
    return jnp.concatenate([jnp.where(lane < HEAD_DIM, q, zero), jnp.where(lane >= HEAD_DIM, q, zero)], axis=0)


def _online_update(s, v, m_ref, l_ref, acc_ref, fixed_max=False):
    if fixed_max:
        acc_ref[...] = acc_ref[...] + _dot(jnp.exp(s).astype(BF16), v)
        return
    m_old = m_ref[...]
    m_new = jnp.maximum(m_old, jnp.max(s, axis=-1, keepdims=True))
    alpha = jnp.exp(m_old - m_new)
    p = jnp.exp(s - m_new)
    l_ref[...] = alpha * l_ref[...] + jnp.sum(p, axis=-1, keepdims=True)
    acc_ref[...] = alpha * acc_ref[...] + _dot(p.astype(BF16), v)
    m_ref[...] = m_new


SCORE_BOUND_SCALE = 1.02 * HEAD_DIM ** 0.5
FIXED_MAX_LIMIT = 30.0


def _finish_head(m_ref, l_ref, acc_ref, lam, g, out_scale, tq):
    acc = acc_ref[...]
    if acc.shape[1] > HEAD_W:
        l = acc[:, HEAD_W:HEAD_W + 1]
        acc = acc[:, :HEAD_W]
    else:
        l = l_ref[...]
    o = acc[:tq] / l[:tq] - lam * (acc[tq:] / l[tq:])
    ms = jnp.mean(o * o, axis=-1, keepdims=True)
    return o * lax.rsqrt(ms + EPS) * g * out_scale


def _attn_prompt_kernel(slope_ref, lam_ref, bound_ref, q_ref, k_ref, v_ref, bias_ref, g_ref, o_ref,
                        m_ref, l_ref, acc_ref, *, out_scale, fixed_max):
    h = pl.program_id(1)
    i = pl.program_id(2)
    tq = q_ref.shape[0]
    tk = tq
    hk = tk // 2
    slope = slope_ref[h]
    shift = bound_ref[0] if fixed_max else 0.0
    qq = _stack_q(q_ref[...])
    m_ref[...] = jnp.full(m_ref.shape, NEG_INF, F32)
    l_ref[...] = jnp.zeros(l_ref.shape, F32)
    acc_ref[...] = jnp.zeros(acc_ref.shape, F32)
    lane = lax.broadcasted_iota(jnp.int32, (hk, HEAD_W), 1)
    ones_col = jnp.where(lane == 0, 1.0, 0.0).astype(BF16)

    def step(j, _):
        which = jnp.where(j == i, 1, 0)
        offset = slope * ((i - j) * tq).astype(F32) + shift
        for half in range(2):
            k0 = pl.multiple_of(j * tk + half * hk, hk)
            tile = bias_ref[which, :, half * hk:(half + 1) * hk] - offset
            s = _dot_t(qq, k_ref[pl.ds(k0, hk), :]) + jnp.concatenate([tile, tile], axis=0)
            v = v_ref[pl.ds(k0, hk), :]
            if fixed_max:
                v = jnp.concatenate([v, ones_col], axis=1)
            _online_update(s, v, m_ref, l_ref, acc_ref, fixed_max)
        return 0

    lax.fori_loop(0, i + 1, step, 0)
    o_ref[...] = _finish_head(m_ref, l_ref, acc_ref, lam_ref[0], g_ref[...], out_scale, tq).astype(o_ref.dtype)


def _attn_prompt(q, kb, vb, slopes, lam, bound, g_subln, out_scale, batch, seq):
    n_heads = q.shape[1] // HEAD_W
    tq = ATT_TQ
    nq = seq // tq
    smem = pl.BlockSpec(memory_space=pltpu.SMEM)
    pos = jnp.arange(tq, dtype=jnp.int32)
    rel = (pos[:, None] - pos[None, :]).astype(F32)
    visible = (pos[None, :] // CHUNK) <= (pos[:, None] // CHUNK)
    sl = slopes[:, None, None]
    bias = jnp.stack([-sl * rel[None], jnp.where(visible[None], -sl * jnp.abs(rel)[None], NEG_INF)], axis=1)

    def call(fixed_max):
        return pl.pallas_call(
            functools.partial(_attn_prompt_kernel, out_scale=out_scale, fixed_max=fixed_max),
            grid=(batch, n_heads, nq),
            in_specs=[smem, smem, smem,
                      pl.BlockSpec((tq, HEAD_W), lambda b, h, i: (b * nq + i, h)),
                      pl.BlockSpec((seq, HEAD_W), lambda b, h, i: (b, h)),
                      pl.BlockSpec((seq, HEAD_W), lambda b, h, i: (b, h)),
                      pl.BlockSpec((None, 2, tq, tq), lambda b, h, i: (h, 0, 0, 0)),
                      pl.BlockSpec((1, HEAD_W), lambda b, h, i: (0, 0))],
            out_specs=pl.BlockSpec((tq, HEAD_W), lambda b, h, i: (b * nq + i, h)),
            out_shape=jax.ShapeDtypeStruct((batch * seq, q.shape[1]), BF16),
            scratch_shapes=[pltpu.VMEM((2 * tq, 1), F32), pltpu.VMEM((2 * tq, 1), F32),
                            pltpu.VMEM((2 * tq, 2 * HEAD_W if fixed_max else HEAD_W), F32)],
            compiler_params=_cparams(("parallel", "parallel", "arbitrary")),
            name="attn_prompt_fixed" if fixed_max else "attn_prompt",
        )(slopes, lam, bound, q, kb, vb, bias, g_subln[None].astype(F32))

    return lax.cond(bound[0] <= FIXED_MAX_LIMIT, lambda: call(True), lambda: call(False))


def _attn_sample_kernel(slope_ref, lam_ref, q_ref, kn_ref, vn_ref, kc_ref, vc_ref, g_ref, o_ref,
                        m_ref, l_ref, acc_ref, *, out_scale, past_len):
    j = pl.program_id(1)
    nj = pl.num_programs(1)
    tq = q_ref.shape[0]
    n_heads = q_ref.shape[1] // HEAD_W
    tk = kc_ref.shape[0] // n_heads

    @pl.when(j == 0)
    def _():
        m_ref[...] = jnp.full(m_ref.shape, NEG_INF, F32)
        l_ref[...] = jnp.zeros(l_ref.shape, F32)
        acc_ref[...] = jnp.zeros(acc_ref.shape, F32)

    rq = lax.broadcasted_iota(jnp.int32, (2 * tq, 1), 0)
    rq = jnp.where(rq >= tq, rq - tq, rq)
    qpos = (past_len + rq).astype(F32)
    kpos = (j * tk + lax.broadcasted_iota(jnp.int32, (1, tk), 1)).astype(F32)
    for h in range(n_heads):
        cols = slice(h * HEAD_W, (h + 1) * HEAD_W)
        qq = _stack_q(q_ref[:, cols])
        k = kc_ref[pl.ds(h, tk, stride=n_heads), :].astype(BF16)
        v = vc_ref[pl.ds(h, tk, stride=n_heads), :].astype(BF16)
        slope = slope_ref[h]
        s = (_dot_t(qq, k) + slope * kpos) - slope * qpos
        _online_update(s, v, m_ref.at[h], l_ref.at[h], acc_ref.at[h])

    @pl.when(j == nj - 1)
    def _():
        r = lax.broadcasted_iota(jnp.int32, (2 * tq, tq), 0)
        r = jnp.where(r >= tq, r - tq, r)
        c = lax.broadcasted_iota(jnp.int32, (2 * tq, tq), 1)
        dist = jnp.abs(r - c).astype(F32)
        for h in range(n_heads):
            cols = slice(h * HEAD_W, (h + 1) * HEAD_W)
            qq = _stack_q(q_ref[:, cols])
            s = _dot_t(qq, kn_ref[:, cols]) - slope_ref[h] * dist
            _online_update(s, vn_ref[:, cols], m_ref.at[h], l_ref.at[h], acc_ref.at[h])
            o_ref[:, cols] = _finish_head(m_ref.at[h], l_ref.at[h], acc_ref.at[h], lam_ref[0],
                                          g_ref[...], out_scale, tq).astype(o_ref.dtype)


def _attn_sample(q, kb, vb, cache_k, cache_v, layer, slopes, lam, g_subln, out_scale, row0):
    depth, n_streams, past_len, n_heads, _ = cache_k.shape
    aw = n_heads * HEAD_W
    tq = CHUNK
    tk = min(ATT_TKC, past_len)
    blk0 = row0 // tq
    smem = pl.BlockSpec(memory_space=pltpu.SMEM)
    new = pl.BlockSpec((tq, aw), lambda s, j: (blk0 + s, 0))
    past = pl.BlockSpec((None, None, tk * n_heads, HEAD_W), lambda s, j: (layer, s, j, 0))
    ck = cache_k.reshape(depth, n_streams, past_len * n_heads, HEAD_W)
    cv = cache_v.reshape(depth, n_streams, past_len * n_heads, HEAD_W)

    return pl.pallas_call(
        functools.partial(_attn_sample_kernel, out_scale=out_scale, past_len=past_len),
        grid=(n_streams, past_len // tk),
        in_specs=[smem, smem, new, new, new, past, past,
                  pl.BlockSpec((1, HEAD_W), lambda s, j: (0, 0))],
        out_specs=pl.BlockSpec((tq, aw), lambda s, j: (s, 0)),
        out_shape=jax.ShapeDtypeStruct((n_streams * tq, aw), BF16),
        scratch_shapes=[pltpu.VMEM((n_heads, 2 * tq, 1), F32), pltpu.VMEM((n_heads, 2 * tq, 1), F32),
                        pltpu.VMEM((n_heads, 2 * tq, HEAD_W), F32)],
        compiler_params=_cparams(("parallel", "arbitrary")),
        name="attn_sample",
    )(slopes, lam, q, kb, vb, ck, cv, g_subln[None].astype(F32))


def _outproj_kernel(ys_ref, oa_ref, x_ref, wglu_ref, gs_ref, wtop_ref, wbot_ref, gf_ref,
                    wrh_ref, wrl_ref, br_ref, x1_ref, h2_ref, lg_ref):
    y = ys_ref[...]
    y = 0.5 * y * (1.0 + jnp.tanh(math.sqrt(2.0 / math.pi) * (y + 0.044715 * (y * y * y))))
    z = _dot(y.astype(BF16), wglu_ref[...])
    y = y * (1.0 / (1.0 + jnp.exp(-z)))
    ms = jnp.mean(y * y, axis=-1, keepdims=True)
    y = y * lax.rsqrt(ms + EPS) * gs_ref[...]
    x1 = x_ref[...] + _dot(y.astype(BF16), wtop_ref[...]) + _dot(oa_ref[...], wbot_ref[...])
    x1_ref[...] = x1
    ms = jnp.mean(x1 * x1, axis=-1, keepdims=True)
    h2 = x1 * lax.rsqrt(ms + EPS) * gf_ref[...]
    tm, d = h2.shape
    nblk = d // 128
    for j in range(nblk):
        h2_ref[pl.ds(j, tm, stride=nblk), :] = h2[:, j * 128:(j + 1) * 128]
    hi, lo = _split_bf16(h2)
    wrh = wrh_ref[...]
    lg_ref[...] = _dot(hi, wrh) + _dot(lo, wrh) + _dot(hi, wrl_ref[...]) + br_ref[...]


def _outproj(ys, oa, x, w_glu, g_ssm, w_out, g_ffn, w_router, b_router):
    t, d = x.shape
    sw = ys.shape[1]
    n_exp = w_router.shape[1]
    lw = max(128, n_exp)
    wr = jnp.zeros((d, lw), F32).at[:, :n_exp].set(w_router)
    wrh, wrl = _split_bf16(wr)
    br = jnp.zeros((1, lw), F32).at[0, :n_exp].set(b_router)
    w_out_bf = w_out.astype(BF16)
    tm = TOK_TILE
    row = lambda i: (i, 0)
    fixed = lambda i: (0, 0)
    return pl.pallas_call(
        _outproj_kernel,
        grid=(t // tm,),
        in_specs=[pl.BlockSpec((tm, sw), row), pl.BlockSpec((tm, d - sw), row), pl.BlockSpec((tm, d), row),
                  pl.BlockSpec((sw, sw), fixed), pl.BlockSpec((1, sw), fixed),
                  pl.BlockSpec((sw, d), fixed), pl.BlockSpec((d - sw, d), fixed), pl.BlockSpec((1, d), fixed),
                  pl.BlockSpec((d, lw), fixed), pl.BlockSpec((d, lw), fixed), pl.BlockSpec((1, lw), fixed)],
        out_specs=[pl.BlockSpec((tm, d), row), pl.BlockSpec((tm * (d // 128), 128), row),
                   pl.BlockSpec((tm, lw), row)],
        out_shape=[jax.ShapeDtypeStruct((t, d), F32), jax.ShapeDtypeStruct((t * (d // 128), 128), F32),
                   jax.ShapeDtypeStruct((t, lw), F32)],
        compiler_params=_cparams(("parallel",)),
        name="outproj",
    )(ys, oa, x, w_glu.astype(BF16), g_ssm[None].astype(F32), w_out_bf[:sw], w_out_bf[sw:],
      g_ffn[None].astype(F32), wrh, wrl, br)


def _moe_kernel(te_ref, nt_ref, src_ref, nxt_ref, dprev_ref, h2_hbm, wgu_ref, bgu_ref, wd_ref, bd_ref,
                out_hbm, xbuf, obuf, wgu_bf, wd_bf, gsem, ssem):
    i = pl.program_id(0)
    nt = nt_ref[0]
    slot = i % 2
    other = 1 - slot
    tm = src_ref.shape[2]
    nblk = xbuf.shape[1] // tm

    def token_rows(ref, first_row):
        return ref.at[pl.ds(pl.multiple_of(first_row, nblk), nblk), :]

    def start_gather(idx_ref, s):
        def two(rr, _):
            for p in range(2):
                r = 2 * rr + p
                pltpu.make_async_copy(token_rows(h2_hbm, idx_ref[0, 0, r]), token_rows(xbuf.at[s], r * nblk),
                                      gsem.at[s]).start(priority=p)
            return 0
        lax.fori_loop(0, tm // 2, two, 0, unroll=4)

    def wait_gather(s):
        pltpu.make_async_copy(xbuf.at[s], xbuf.at[s], gsem.at[s]).wait()

    def start_scatter(idx_ref, s):
        def two(rr, _):
            for p in range(2):
                r = 2 * rr + p
                pltpu.make_async_copy(token_rows(obuf.at[s], r * nblk), token_rows(out_hbm, idx_ref[0, 0, r]),
                                      ssem.at[0]).start(priority=p)
            return 0
        lax.fori_loop(0, tm // 2, two, 0, unroll=4)

    def wait_scatter(s):
        pltpu.make_async_copy(obuf.at[s], obuf.at[s], ssem.at[0]).wait()

    @pl.when(i == 0)
    def _():
        obuf[...] = jnp.zeros(obuf.shape, obuf.dtype)
        start_gather(src_ref, 0)

    @pl.when(i < nt)
    def _():
        wait_gather(slot)

    prev = te_ref[jnp.maximum(i - 1, 0)]
    new_expert = jnp.logical_or(i == 0, te_ref[i] != prev)

    @pl.when(jnp.logical_and(i < nt, new_expert))
    def _():
        wgu_bf[...] = wgu_ref[0].astype(BF16)
        wd_bf[...] = wd_ref[0].astype(BF16)

    @pl.when(i < nt)
    def _():
        start_gather(nxt_ref, other)
        start_scatter(dprev_ref, other)
        dff = wd_bf.shape[0]
        xin = xbuf.at[slot]
        x = jnp.concatenate([xin[pl.ds(j, tm, stride=nblk), :].astype(BF16) for j in range(nblk)], axis=1)
        gu = _dot(x, wgu_bf[...]) + bgu_ref[0]
        x_glu = jnp.minimum(gu[:, :dff], SWIGLU_LIMIT)
        x_lin = jnp.clip(gu[:, dff:], -SWIGLU_LIMIT, SWIGLU_LIMIT)
        hdn = x_glu * (1.0 / (1.0 + jnp.exp(-SWIGLU_ALPHA * x_glu))) * (x_lin + 1.0)
        out = _dot(hdn.astype(BF16), wd_bf[...]) + bd_ref[0]
        res = obuf.at[slot]
        for j in range(nblk):
            res[pl.ds(j, tm, stride=nblk), :] = out[:, j * 128:(j + 1) * 128]
        wait_scatter(other)

    @pl.when(i == nt)
    def _():
        wait_gather(slot)
        start_scatter(dprev_ref, other)
        wait_scatter(other)


def _moe_rows(h2, route, layer, w_gate_up, b_gate_up, w_down, b_down):
    tile_expert, n_used, src, dst = route
    depth, n_exp, d, dgu = w_gate_up.shape
    nblk = d // 128
    t = h2.shape[0] // nblk
    dff = w_down.shape[2]
    tm = MOE_TILE
    n_tiles = tile_expert.shape[0]
    by_e = lambda i, te, nt: (layer * n_exp + te[i], 0, 0)
    smem = lambda f: pl.BlockSpec((1, 1, tm), f, memory_space=pltpu.SMEM)
    any_ = pl.BlockSpec(memory_space=pl.ANY)
    return pl.pallas_call(
        _moe_kernel,
        grid_spec=pltpu.PrefetchScalarGridSpec(
            num_scalar_prefetch=2,
            grid=(n_tiles,),
            in_specs=[smem(lambda i, te, nt: (i, 0, 0)), smem(lambda i, te, nt: (i + 1, 0, 0)),
                      smem(lambda i, te, nt: (i, 0, 0)), any_,
                      pl.BlockSpec((1, d, dgu), by_e), pl.BlockSpec((1, 1, dgu), by_e),
                      pl.BlockSpec((1, dff, d), by_e), pl.BlockSpec((1, 1, d), by_e)],
            out_specs=any_,
            scratch_shapes=[pltpu.VMEM((2, tm * nblk, 128), F32), pltpu.VMEM((2, tm * nblk, 128), F32),
                            pltpu.VMEM((d, dgu), BF16), pltpu.VMEM((dff, d), BF16),
                            pltpu.SemaphoreType.DMA((2,)), pltpu.SemaphoreType.DMA((1,))]),
        out_shape=jax.ShapeDtypeStruct(((TOP_K * t + tm) * nblk, 128), F32),
        compiler_params=_cparams(("arbitrary",)),
        name="moe",
    )(tile_expert, n_used, src, src, dst, h2, w_gate_up.reshape(depth * n_exp, d, dgu),
      b_gate_up.reshape(depth * n_exp, 1, dgu), w_down.reshape(depth * n_exp, dff, d),
      b_down.reshape(depth * n_exp, 1, d))


def _moe_route(logits, n_exp, nblk):
    t = logits.shape[0]
    tm = MOE_TILE
    top_val, top_idx = lax.top_k(logits, TOP_K)
    gates = jax.nn.softmax(top_val, axis=-1)
    n_assign = t * TOP_K
    e_flat = top_idx.T.reshape(-1).astype(jnp.int32)
    order = jnp.argsort(e_flat).astype(jnp.int32)
    experts = jnp.arange(n_exp, dtype=jnp.int32)
    counts = jnp.sum((e_flat[:, None] == experts[None, :]).astype(jnp.int32), axis=0)
    starts = jnp.cumsum(counts) - counts
    tiles_e = (counts + tm - 1) // tm
    tile_end = jnp.cumsum(tiles_e)
    tile_beg = tile_end - tiles_e
    n_used = tile_end[-1]
    n_tiles = n_assign // tm + n_exp + 1
    tile = jnp.arange(n_tiles, dtype=jnp.int32)
    tile_c = jnp.minimum(tile, n_used - 1)
    te = jnp.sum((tile_end[None, :] <= tile_c[:, None]).astype(jnp.int32), axis=1)
    te = jnp.minimum(te, n_exp - 1)
    first = starts[te] + (tile - tile_beg[te]) * tm
    n_valid = jnp.where(tile < n_used, jnp.clip(counts[te] - (tile - tile_beg[te]) * tm, 0, tm), 0)
    r = jnp.arange(tm, dtype=jnp.int32)
    valid = r[None, :] < n_valid[:, None]
    a = order[jnp.clip(first[:, None] + r[None, :], 0, n_assign - 1)]
    src = jnp.where(valid, a % t, 0)
    dst = jnp.where(valid, a, n_assign + r[None, :])
    spare = jnp.broadcast_to(n_assign + r[None, :], (1, tm))
    src = jnp.concatenate([src, jnp.zeros((1, tm), jnp.int32)], axis=0)[:, None, :] * nblk
    dst = jnp.concatenate([spare, dst], axis=0)[:, None, :] * nblk
    return gates, (te.astype(jnp.int32), n_used.astype(jnp.int32)[None], src, dst)


def _combine_kernel(x1_ref, g_ref, o0_ref, o1_ref, o2_ref, o3_ref, x2_ref):
    g = g_ref[...]
    tm, d = x1_ref.shape
    nblk = d // 128
    for j in range(nblk):
        cols = slice(j * 128, (j + 1) * 128)
        acc = x1_ref[:, cols]
        for k, o_ref in enumerate((o0_ref, o1_ref, o2_ref, o3_ref)):
            acc = acc + g[:, k:k + 1] * o_ref[pl.ds(j, tm, stride=nblk), :]
        x2_ref[:, cols] = acc


def _combine(x1, gates, out_rows):
    t, d = x1.shape
    tm = TOK_TILE
    nt = t // tm
    rows = lambda k: pl.BlockSpec((tm * (d // 128), 128), lambda i: (k * nt + i, 0))
    return pl.pallas_call(
        _combine_kernel,
        grid=(nt,),
        in_specs=[pl.BlockSpec((tm, d), lambda i: (i, 0)), pl.BlockSpec((tm, TOP_K), lambda i: (i, 0)),
                  rows(0), rows(1), rows(2), rows(3)],
        out_specs=pl.BlockSpec((tm, d), lambda i: (i, 0)),
        out_shape=jax.ShapeDtypeStruct((t, d), F32),
        compiler_params=_cparams(("parallel",)),
        name="combine",
    )(x1, gates, out_rows, out_rows, out_rows, out_rows)


def _lambda_init(layer):
    return 0.8 - 0.6 * math.exp(-0.3 * layer)


def kernel(x_prompt, x_sample, cache_k, cache_v, state_ssm_re, state_ssm_im, g_mix, w_in, ssm_a_re, ssm_a_im, ssm_log_dt, ssm_b_re, ssm_b_im, ssm_c_re, ssm_c_im, ssm_d, w_glu, g_ssm_out, g_q, g_k, lambda_q1, lambda_k1, lambda_q2, lambda_k2, g_subln, w_out, g_ffn, w_router, b_router, w_gate_up, b_gate_up, w_down, b_down):
    batch, seq, d = x_prompt.shape
    n_streams, dec_seq, _ = x_sample.shape
    depth = w_in.shape[0]
    past_len = cache_k.shape[2]
    n_heads = cache_k.shape[3]
    aw = n_heads * HEAD_W
    n_groups, n_state = ssm_a_re.shape[1:]
    nq = n_groups // SSM_QG
    assert dec_seq == CHUNK and seq % SSM_ROWS == 0 and n_streams % 8 == 0 and n_state == SSM_STATE
    tp = batch * seq
    ts = n_streams * dec_seq
    n_prompt_sc = tp // SSM_ROWS
    sc_per_seq = seq // SSM_ROWS
    n_sample_sc = ts // SSM_ROWS
    slopes = jnp.asarray([2.0 ** (-8.0 * (h + 1) / n_heads) for h in range(n_heads)], F32)

    x = jnp.concatenate([x_prompt.reshape(tp, d), x_sample.reshape(ts, d)], axis=0)
    outs = {name: [] for name in ("srp", "sip", "srs", "sis")}
    kv_out = None
    n_exp = w_router.shape[2]
    for l in range(depth):
        u, q, kb, vb, *kv_out = _inproj(x, g_mix[l], w_in[l].astype(BF16), g_q[l], g_k[l], l, depth,
                                        tp // TOK_TILE, kv_out)

        tables = _ssm_tables(ssm_a_re[l], ssm_a_im[l], ssm_log_dt[l], ssm_b_re[l], ssm_b_im[l],
                             ssm_c_re[l], ssm_c_im[l], ssm_d[l])

        def state_lanes(z):
            return z.reshape(n_sample_sc, 8, nq, SSM_QW).transpose(0, 2, 1, 3)

        h0_s = jnp.concatenate([state_lanes(state_ssm_re[l]), state_lanes(state_ssm_im[l])], axis=-1)
        h0_all = jnp.concatenate([jnp.zeros((n_prompt_sc,) + h0_s.shape[1:], F32), h0_s], axis=0)
        ys, fin = _ssm(u, tables, h0_all, n_prompt_sc, sc_per_seq)

        lam_init = _lambda_init(l)
        lam = (jnp.exp(jnp.sum(lambda_q1[l].astype(F32) * lambda_k1[l].astype(F32)))
               - jnp.exp(jnp.sum(lambda_q2[l].astype(F32) * lambda_k2[l].astype(F32))) + lam_init)[None]
        out_scale = 1.0 - lam_init
        bound = (SCORE_BOUND_SCALE * jnp.max(jnp.abs(g_q[l].astype(F32)))
                 * jnp.max(jnp.abs(g_k[l].astype(F32))))[None]
        o_p = _attn_prompt(q, kb, vb, slopes, lam, bound, g_subln[l], out_scale, batch, seq)
        o_s = _attn_sample(q, kb, vb, cache_k, cache_v, l, slopes, lam, g_subln[l], out_scale, tp)
        oa = jnp.concatenate([o_p, o_s], axis=0)

        x1, h2, logits = _outproj(ys, oa, x, w_glu[l], g_ssm_out[l], w_out[l], g_ffn[l],
                                  w_router[l], b_router[l])
        gates, route = _moe_route(logits[:, :n_exp], n_exp, d // 128)
        x = _combine(x1, gates, _moe_rows(h2, route, l, w_gate_up, b_gate_up, w_down, b_down))

        fin_p = fin[:n_prompt_sc].reshape(batch, sc_per_seq, nq, 8, 2, SSM_QW)[:, -1, :, -1]
        outs["srp"].append(fin_p[:, :, 0].reshape(batch, n_groups, n_state))
        outs["sip"].append(fin_p[:, :, 1].reshape(batch, n_groups, n_state))
        fin_s = fin[n_prompt_sc:].reshape(n_sample_sc, nq, 8, 2, SSM_QW).transpose(0, 2, 3, 1, 4)
        outs["srs"].append(fin_s[:, :, 0].reshape(n_streams, n_groups, n_state))
        outs["sis"].append(fin_s[:, :, 1].reshape(n_streams, n_groups, n_state))

    st = {name: jnp.stack(vals) for name, vals in outs.items()}
    kp, vp, ks, vs = kv_out
    p_shape = (depth, batch, seq, n_heads, HEAD_W)
    s_shape = (depth, n_streams, dec_seq, n_heads, HEAD_W)
    return (x[:tp].reshape(batch, seq, d), x[tp:].reshape(n_streams, dec_seq, d),
            kp.reshape(p_shape), vp.reshape(p_shape), st["srp"], st["sip"],
            ks.reshape(s_shape), vs.reshape(s_shape), st["srs"], st["sis"])
```

```python
import functools
import math

import jax
import jax.numpy as jnp
from jax import lax
from jax.experimental import pallas as pl
from jax.experimental.pallas import tpu as pltpu

F32 = jnp.float32
BF16 = jnp.bfloat16

CHUNK = 64
HEAD_DIM = 64
HEAD_W = 2 * HEAD_DIM
SSM_GROUP = 16
SSM_STATE = 64
SSM_QG = 8
SSM_QW = SSM_QG * SSM_STATE
TOP_K = 4
SWIGLU_ALPHA = 1.702
SWIGLU_LIMIT = 7.0
EPS = 1e-6
NEG_INF = -1e30

TOK_TILE = 512
SSM_ROWS = 8 * CHUNK
ATT_TQ = 512
ATT_TKC = 2048
MOE_TILE = 256
VMEM_LIMIT = 56 * 1024 * 1024


def _cparams(sem):
    return pltpu.CompilerParams(dimension_semantics=sem, vmem_limit_bytes=VMEM_LIMIT)


def _dot(a, b):
    return jnp.dot(a, b, preferred_element_type=F32)


def _dot_t(a, b):
    return lax.dot_general(a, b, (((1,), (1,)), ((), ())), preferred_element_type=F32)


def _split_bf16(x):
    hi = x.astype(BF16)
    lo = (x - hi.astype(F32)).astype(BF16)
    return hi, lo


def _inproj_kernel(x_ref, g_ref, w_ref, gq_ref, gk_ref, seg_ref, *rest, n_prompt_tiles, n_prev):
    u_ref, q_ref, kb_ref, vb_ref, kp_ref, vp_ref, ks_ref, vs_ref = rest[n_prev:]
    i = pl.program_id(0)
    x = x_ref[...]
    ms = jnp.mean(x * x, axis=-1, keepdims=True)
    h = (x * lax.rsqrt(ms + EPS) * g_ref[...]).astype(BF16)
    proj = _dot(h, w_ref[...])
    w = u_ref.shape[-1]
    seg = seg_ref[...]

    def head_norm(z, g):
        hi, lo = _split_bf16(z * z)
        ms_ = _dot(hi, seg) + _dot(lo, seg)
        return z * lax.rsqrt(ms_ + EPS) * g

    u_ref[...] = proj[:, :w]
    qn = head_norm(proj[:, w:2 * w], gq_ref[...])
    q_ref[...] = (qn * (HEAD_DIM ** -0.5)).astype(BF16)
    kn = head_norm(proj[:, 2 * w:3 * w], gk_ref[...])
    kb_ref[...] = kn.astype(BF16)
    vv = proj[:, 3 * w:]
    vb_ref[...] = vv.astype(BF16)

    def emit(k_out, v_out):
        tm = kn.shape[0]
        nh = k_out.shape[0] // tm
        for hd in range(nh):
            k_out[pl.ds(hd, tm, stride=nh), :] = kn[:, hd * HEAD_W:(hd + 1) * HEAD_W]
            v_out[pl.ds(hd, tm, stride=nh), :] = vv[:, hd * HEAD_W:(hd + 1) * HEAD_W]

    @pl.when(i < n_prompt_tiles)
    def _():
        emit(kp_ref, vp_ref)

    @pl.when(i >= n_prompt_tiles)
    def _():
        emit(ks_ref, vs_ref)


def _inproj(x, g_mix, w_in_bf, g_q, g_k, layer, depth, n_prompt_tiles, prev):
    t, d = x.shape
    aw = w_in_bf.shape[1] // 4
    nh = aw // HEAD_W
    nrep = aw // HEAD_DIM
    gq = jnp.tile(g_q.astype(F32), nrep)[None]
    gk = jnp.tile(g_k.astype(F32), nrep)[None]
    ids = jnp.arange(aw) // HEAD_DIM
    seg = jnp.where(ids[:, None] == ids[None, :], 1.0 / HEAD_DIM, 0.0).astype(BF16)
    tm = TOK_TILE
    n_tiles = t // tm
    n_sample_tiles = n_tiles - n_prompt_tiles
    row = lambda i: (i, 0)
    fixed = lambda i: (0, 0)
    p_blk = pl.BlockSpec((None, tm * nh, HEAD_W),
                         lambda i: (layer * n_prompt_tiles + jnp.minimum(i, n_prompt_tiles - 1), 0, 0))
    s_blk = pl.BlockSpec((None, tm * nh, HEAD_W),
                         lambda i: (layer * n_sample_tiles + jnp.maximum(i - n_prompt_tiles, 0), 0, 0))
    p_shape = jax.ShapeDtypeStruct((depth * n_prompt_tiles, tm * nh, HEAD_W), F32)
    s_shape = jax.ShapeDtypeStruct((depth * n_sample_tiles, tm * nh, HEAD_W), F32)
    outs = [jax.ShapeDtypeStruct((t, aw), dt) for dt in (F32, BF16, BF16, BF16)] + [p_shape, p_shape, s_shape, s_shape]
    prev = () if prev is None else tuple(prev)
    n_in = 6
    kern = functools.partial(_inproj_kernel, n_prompt_tiles=n_prompt_tiles, n_prev=len(prev))
    return pl.pallas_call(
        kern,
        grid=(n_tiles,),
        in_specs=[pl.BlockSpec((tm, d), row), pl.BlockSpec((1, d), fixed),
                  pl.BlockSpec(w_in_bf.shape, fixed), pl.BlockSpec((1, aw), fixed),
                  pl.BlockSpec((1, aw), fixed), pl.BlockSpec((aw, aw), fixed)]
                 + [pl.BlockSpec(memory_space=pl.ANY)] * len(prev),
        out_specs=[pl.BlockSpec((tm, aw), row)] * 4 + [p_blk, p_blk, s_blk, s_blk],
        out_shape=outs,
        input_output_aliases={n_in + j: 4 + j for j in range(len(prev))},
        compiler_params=_cparams(("arbitrary",)),
        name="inproj",
    )(x, g_mix[None].astype(F32), w_in_bf, gq, gk, seg, *prev)


def _ssm_kernel(u_ref, wb_ref, wc_ref, ab_ref, a64_ref, pw_ref, d_ref, h0_ref,
                y_ref, fin_ref, uperm_ref, st_ref, carry_ref, *, n_prompt_sc, sc_per_seq):
    sc = pl.program_id(0)
    qb = pl.program_id(1)
    nsteps = CHUNK
    w = SSM_QW

    for t in range(nsteps):
        uperm_ref[t * 8:(t + 1) * 8, :] = u_ref[pl.ds(t, 8, stride=nsteps), :]
    up = uperm_ref[...]
    st_ref[...] = _dot(up.astype(BF16), wb_ref[0])

    ab = ab_ref[0]
    ar = jnp.broadcast_to(ab[:, :w], (8, w))
    ai = jnp.broadcast_to(ab[:, w:], (8, w))

    def scan_step(t, carry):
        hr, hi = carry
        r0 = pl.multiple_of(t * 8, 8)
        br = st_ref[pl.ds(r0, 8), :w]
        bi = st_ref[pl.ds(r0, 8), w:]
        nr = ar * hr - ai * hi + br
        ni = ar * hi + ai * hr + bi
        st_ref[pl.ds(r0, 8), :w] = nr
        st_ref[pl.ds(r0, 8), w:] = ni
        return nr, ni

    zero = jnp.zeros((8, w), F32)
    er, ei = lax.fori_loop(0, nsteps, scan_step, (zero, zero), unroll=4)

    a64 = a64_ref[0]
    a64r = a64[:, :w]
    a64i = a64[:, w:]
    is_sample = sc >= n_prompt_sc

    @pl.when(jnp.logical_or(is_sample, sc % sc_per_seq == 0))
    def _():
        carry_ref[qb] = jnp.zeros(carry_ref.shape[1:], F32)

    cin = carry_ref[qb]
    cr = cin[:, :w]
    ci = cin[:, w:]
    rows = lax.broadcasted_iota(jnp.int32, (8, w), 0)
    sr = jnp.zeros((8, w), F32)
    si = jnp.zeros((8, w), F32)
    for j in range(8):
        sr = jnp.where(rows == j, cr, sr)
        si = jnp.where(rows == j, ci, si)
        ejr = er[j:j + 1]
        eji = ei[j:j + 1]
        cr, ci = a64r * cr - a64i * ci + ejr, a64r * ci + a64i * cr + eji
    carry_ref[qb] = jnp.concatenate([cr, ci], axis=1)
    h0 = h0_ref[0, 0]
    given = (jnp.zeros((8, w), jnp.int32) + is_sample.astype(jnp.int32)) > 0
    sr = jnp.where(given, h0[:, :w], sr)
    si = jnp.where(given, h0[:, w:], si)
    fr = a64r * sr - a64i * si + er
    fi = a64r * si + a64i * sr + ei
    fin_ref[0, 0] = jnp.concatenate([fr, fi], axis=1)

    def fix_step(t, _):
        r0 = pl.multiple_of(t * 8, 8)
        p = pw_ref[0, pl.ds(t, 1), :]
        pr = p[:, :w]
        pi = p[:, w:]
        st_ref[pl.ds(r0, 8), :w] = st_ref[pl.ds(r0, 8), :w] + (pr * sr - pi * si)
        st_ref[pl.ds(r0, 8), w:] = st_ref[pl.ds(r0, 8), w:] + (pr * si + pi * sr)
        return 0

    lax.fori_loop(0, nsteps, fix_step, 0, unroll=4)

    y = _dot(st_ref[...].astype(BF16), wc_ref[0]) + up * d_ref[0]
    for t in range(nsteps):
        y_ref[pl.ds(t, 8, stride=nsteps), :] = y[t * 8:(t + 1) * 8, :]


def _ssm_tables(a_re, a_im, log_dt, b_re, b_im, c_re, c_im, d_skip):
    g, n = a_re.shape
    c = b_re.shape[-1]
    nq = g // SSM_QG
    dt = jnp.exp(log_dt)[:, None]
    za_re, za_im = dt * a_re, dt * a_im
    mag = jnp.exp(za_re)
    ab_re, ab_im = mag * jnp.cos(za_im), mag * jnp.sin(za_im)
    den = a_re * a_re + a_im * a_im
    n_re, n_im = ab_re - 1.0, ab_im
    f_re = (n_re * a_re + n_im * a_im) / den
    f_im = (n_im * a_re - n_re * a_im) / den
    bb_re = f_re[..., None] * b_re - f_im[..., None] * b_im
    bb_im = f_re[..., None] * b_im + f_im[..., None] * b_re
    eye = jnp.eye(SSM_QG, dtype=F32)

    def in_w(bb):
        bq = bb.reshape(nq, SSM_QG, n, c)
        return jnp.einsum('qgnc,gh->qgchn', bq, eye).reshape(nq, SSM_QG * c, SSM_QG * n)

    def out_w(cc):
        cq = cc.reshape(nq, SSM_QG, c, n)
        return jnp.einsum('qgcn,gh->qgnhc', cq, eye).reshape(nq, SSM_QG * n, SSM_QG * c)

    wb = jnp.concatenate([in_w(bb_re), in_w(bb_im)], axis=2).astype(BF16)
    wc = jnp.concatenate([out_w(c_re), out_w(-c_im)], axis=1).astype(BF16)

    def lanes(z):
        return jnp.moveaxis(z.reshape(z.shape[:-2] + (nq, SSM_QG * n)), -2, 0)

    def power(k):
        m = jnp.exp(k * za_re)
        return m * jnp.cos(k * za_im), m * jnp.sin(k * za_im)

    ab = jnp.concatenate([lanes(ab_re), lanes(ab_im)], axis=-1)[:, None]
    p64 = power(float(CHUNK))
    a64 = jnp.concatenate([lanes(p64[0]), lanes(p64[1])], axis=-1)[:, None]
    ks = jnp.arange(1, CHUNK + 1, dtype=F32)[:, None, None]
    pk = power(ks)
    pw = jnp.concatenate([lanes(pk[0]), lanes(pk[1])], axis=-1)
    dq = d_skip.reshape(nq, 1, SSM_QG * c).astype(F32)
    return wb, wc, ab, a64, pw, dq


def _ssm(u, tables, h0_all, n_prompt_sc, sc_per_seq):
    wb, wc, ab, a64, pw, dq = tables
    t, cw = u.shape
    nq = wb.shape[0]
    n_sc = t // SSM_ROWS
    sw = 2 * SSM_QW
    kern = functools.partial(_ssm_kernel, n_prompt_sc=n_prompt_sc, sc_per_seq=sc_per_seq)
    per_q = lambda s, q: (q, 0, 0)
    return pl.pallas_call(
        kern,
        grid=(n_sc, nq),
        in_specs=[pl.BlockSpec((SSM_ROWS, 128), lambda s, q: (s, q)),
                  pl.BlockSpec((1,) + wb.shape[1:], per_q), pl.BlockSpec((1,) + wc.shape[1:], per_q),
                  pl.BlockSpec((1, 1, sw), per_q), pl.BlockSpec((1, 1, sw), per_q),
                  pl.BlockSpec((1, CHUNK, sw), per_q), pl.BlockSpec((1, 1, 128), per_q),
                  pl.BlockSpec((1, 1, 8, sw), lambda s, q: (s, q, 0, 0))],
        out_specs=[pl.BlockSpec((SSM_ROWS, 128), lambda s, q: (s, q)),
                   pl.BlockSpec((1, 1, 8, sw), lambda s, q: (s, q, 0, 0))],
        out_shape=[jax.ShapeDtypeStruct((t, cw), F32),
                   jax.ShapeDtypeStruct((n_sc, nq, 8, sw), F32)],
        scratch_shapes=[pltpu.VMEM((SSM_ROWS, 128), F32), pltpu.VMEM((SSM_ROWS, sw), F32),
                        pltpu.VMEM((nq, 1, sw), F32)],
        compiler_params=_cparams(("arbitrary", "arbitrary")),
        name="ssm",
    )(u, wb, wc, ab, a64, pw, dq, h0_all)


def _stack_q(q):
    lane = lax.broadcasted_iota(jnp.int32, q.shape, 1)
    zero = jnp.zeros_like(q)
    return jnp.concatenate([jnp.where(lane < HEAD_DIM, q, zero), jnp.where(lane >= HEAD_DIM, q, zero)], axis=0)


def _online_update(s, v, m_ref, l_ref, acc_ref, fixed_max=False):
    if fixed_max:
        acc_ref[...] = acc_ref[...] + _dot(jnp.exp(s).astype(BF16), v)
        return
    m_old = m_ref[...]
    m_new = jnp.maximum(m_old, jnp.max(s, axis=-1, keepdims=True))
    alpha = jnp.exp(m_old - m_new)
    p = jnp.exp(s - m_new)
    l_ref[...] = alpha * l_ref[...] + jnp.sum(p, axis=-1, keepdims=True)
    acc_ref[...] = alpha * acc_ref[...] + _dot(p.astype(BF16), v)
    m_ref[...] = m_new


SCORE_BOUND_SCALE = 1.02 * HEAD_DIM ** 0.5
FIXED_MAX_LIMIT = 30.0


def _finish_head(m_ref, l_ref, acc_ref, lam, g, out_scale, tq):
    acc = acc_ref[...]
    if acc.shape[1] > HEAD_W:
        l = acc[:, HEAD_W:HEAD_W + 1]
        acc = acc[:, :HEAD_W]
    else:
        l = l_ref[...]
    o = acc[:tq] / l[:tq] - lam * (acc[tq:] / l[tq:])
    ms = jnp.mean(o * o, axis=-1, keepdims=True)
    return o * lax.rsqrt(ms + EPS) * g * out_scale


def _attn_prompt_kernel(slope_ref, lam_ref, bound_ref, q_ref, k_ref, v_ref, bias_ref, g_ref, o_ref,
                        m_ref, l_ref, acc_ref, *, out_scale, fixed_max):
    h = pl.program_id(1)
    i = pl.program_id(2)
    tq = q_ref.shape[0]
    tk = tq
    hk = tk // 2
    slope = slope_ref[h]
    shift = bound_ref[0] if fixed_max else 0.0
    qq = _stack_q(q_ref[...])
    m_ref[...] = jnp.full(m_ref.shape, NEG_INF, F32)
    l_ref[...] = jnp.zeros(l_ref.shape, F32)
    acc_ref[...] = jnp.zeros(acc_ref.shape, F32)
    lane = lax.broadcasted_iota(jnp.int32, (hk, HEAD_W), 1)
    ones_col = jnp.where(lane == 0, 1.0, 0.0).astype(BF16)

    def key_tile(j):
        which = jnp.where(j == i, 1, 0)
        offset = slope * ((i - j) * tq).astype(F32) + shift
        for half in range(2):
            k0 = pl.multiple_of(j * tk + half * hk, hk)
            tile = bias_ref[which, :, half * hk:(half + 1) * hk] - offset
            s = _dot_t(qq, k_ref[pl.ds(k0, hk), :]) + jnp.concatenate([tile, tile], axis=0)
            v = v_ref[pl.ds(k0, hk), :]
            if fixed_max:
                v = jnp.concatenate([v, ones_col], axis=1)
            _online_update(s, v, m_ref, l_ref, acc_ref, fixed_max)

    def two_tiles(jj, _):
        key_tile(2 * jj)
        key_tile(2 * jj + 1)
        return 0

    n_tiles = i + 1
    lax.fori_loop(0, n_tiles >> 1, two_tiles, 0)

    @pl.when((n_tiles & 1) == 1)
    def _():
        key_tile(i)

    o_ref[...] = _finish_head(m_ref, l_ref, acc_ref, lam_ref[0], g_ref[...], out_scale, tq).astype(o_ref.dtype)


def _attn_prompt(q, kb, vb, slopes, lam, bound, g_subln, out_scale, batch, seq):
    n_heads = q.shape[1] // HEAD_W
    tq = ATT_TQ
    nq = seq // tq
    smem = pl.BlockSpec(memory_space=pltpu.SMEM)
    pos = jnp.arange(tq, dtype=jnp.int32)
    rel = (pos[:, None] - pos[None, :]).astype(F32)
    visible = (pos[None, :] // CHUNK) <= (pos[:, None] // CHUNK)
    sl = slopes[:, None, None]
    bias = jnp.stack([-sl * rel[None], jnp.where(visible[None], -sl * jnp.abs(rel)[None], NEG_INF)], axis=1)

    def call(fixed_max):
        return pl.pallas_call(
            functools.partial(_attn_prompt_kernel, out_scale=out_scale, fixed_max=fixed_max),
            grid=(batch, n_heads, nq),
            in_specs=[smem, smem, smem,
                      pl.BlockSpec((tq, HEAD_W), lambda b, h, i: (b * nq + i, h)),
                      pl.BlockSpec((seq, HEAD_W), lambda b, h, i: (b, h)),
                      pl.BlockSpec((seq, HEAD_W), lambda b, h, i: (b, h)),
                      pl.BlockSpec((None, 2, tq, tq), lambda b, h, i: (h, 0, 0, 0)),
                      pl.BlockSpec((1, HEAD_W), lambda b, h, i: (0, 0))],
            out_specs=pl.BlockSpec((tq, HEAD_W), lambda b, h, i: (b * nq + i, h)),
            out_shape=jax.ShapeDtypeStruct((batch * seq, q.shape[1]), BF16),
            scratch_shapes=[pltpu.VMEM((2 * tq, 1), F32), pltpu.VMEM((2 * tq, 1), F32),
                            pltpu.VMEM((2 * tq, 2 * HEAD_W if fixed_max else HEAD_W), F32)],
            compiler_params=_cparams(("parallel", "parallel", "arbitrary")),
            name="attn_prompt_fixed" if fixed_max else "attn_prompt",
        )(slopes, lam, bound, q, kb, vb, bias, g_subln[None].astype(F32))

    return lax.cond(bound[0] <= FIXED_MAX_LIMIT, lambda: call(True), lambda: call(False))


def _attn_sample_kernel(slope_ref, lam_ref, q_ref, kn_ref, vn_ref, kc_ref, vc_ref, g_ref, o_ref,
                        m_ref, l_ref, acc_ref, *, out_scale, past_len):
    j = pl.program_id(1)
    nj = pl.num_programs(1)
    tq = q_ref.shape[0]
    n_heads = q_ref.shape[1] // HEAD_W
    tk = kc_ref.shape[0] // n_heads

    @pl.when(j == 0)
    def _():
        m_ref[...] = jnp.full(m_ref.shape, NEG_INF, F32)
        l_ref[...] = jnp.zeros(l_ref.shape, F32)
        acc_ref[...] = jnp.zeros(acc_ref.shape, F32)

    rq = lax.broadcasted_iota(jnp.int32, (2 * tq, 1), 0)
    rq = jnp.where(rq >= tq, rq - tq, rq)
    qpos = (past_len + rq).astype(F32)
    kpos = (j * tk + lax.broadcasted_iota(jnp.int32, (1, tk), 1)).astype(F32)
    for h in range(n_heads):
        cols = slice(h * HEAD_W, (h + 1) * HEAD_W)
        qq = _stack_q(q_ref[:, cols])
        k = kc_ref[pl.ds(h, tk, stride=n_heads), :].astype(BF16)
        v = vc_ref[pl.ds(h, tk, stride=n_heads), :].astype(BF16)
        slope = slope_ref[h]
        s = (_dot_t(qq, k) + slope * kpos) - slope * qpos
        _online_update(s, v, m_ref.at[h], l_ref.at[h], acc_ref.at[h])

    @pl.when(j == nj - 1)
    def _():
        r = lax.broadcasted_iota(jnp.int32, (2 * tq, tq), 0)
        r = jnp.where(r >= tq, r - tq, r)
        c = lax.broadcasted_iota(jnp.int32, (2 * tq, tq), 1)
        dist = jnp.abs(r - c).astype(F32)
        for h in range(n_heads):
            cols = slice(h * HEAD_W, (h + 1) * HEAD_W)
            qq = _stack_q(q_ref[:, cols])
            s = _dot_t(qq, kn_ref[:, cols]) - slope_ref[h] * dist
            _online_update(s, vn_ref[:, cols], m_ref.at[h], l_ref.at[h], acc_ref.at[h])
            o_ref[:, cols] = _finish_head(m_ref.at[h], l_ref.at[h], acc_ref.at[h], lam_ref[0],
                                          g_ref[...], out_scale, tq).astype(o_ref.dtype)


def _attn_sample(q, kb, vb, cache_k, cache_v, layer, slopes, lam, g_subln, out_scale, row0):
    depth, n_streams, past_len, n_heads, _ = cache_k.shape
    aw = n_heads * HEAD_W
    tq = CHUNK
    tk = min(ATT_TKC, past_len)
    blk0 = row0 // tq
    smem = pl.BlockSpec(memory_space=pltpu.SMEM)
    new = pl.BlockSpec((tq, aw), lambda s, j: (blk0 + s, 0))
    past = pl.BlockSpec((None, None, tk * n_heads, HEAD_W), lambda s, j: (layer, s, j, 0))
    ck = cache_k.reshape(depth, n_streams, past_len * n_heads, HEAD_W)
    cv = cache_v.reshape(depth, n_streams, past_len * n_heads, HEAD_W)

    return pl.pallas_call(
        functools.partial(_attn_sample_kernel, out_scale=out_scale, past_len=past_len),
        grid=(n_streams, past_len // tk),
        in_specs=[smem, smem, new, new, new, past, past,
                  pl.BlockSpec((1, HEAD_W), lambda s, j: (0, 0))],
        out_specs=pl.BlockSpec((tq, aw), lambda s, j: (s, 0)),
        out_shape=jax.ShapeDtypeStruct((n_streams * tq, aw), BF16),
        scratch_shapes=[pltpu.VMEM((n_heads, 2 * tq, 1), F32), pltpu.VMEM((n_heads, 2 * tq, 1), F32),
                        pltpu.VMEM((n_heads, 2 * tq, HEAD_W), F32)],
        compiler_params=_cparams(("parallel", "arbitrary")),
        name="attn_sample",
    )(slopes, lam, q, kb, vb, ck, cv, g_subln[None].astype(F32))


def _outproj_kernel(ys_ref, oa_ref, x_ref, wglu_ref, gs_ref, wtop_ref, wbot_ref, gf_ref,
                    wrh_ref, wrl_ref, br_ref, x1_ref, h2_ref, rt_ref):
    y = ys_ref[...]
    y = 0.5 * y * (1.0 + jnp.tanh(math.sqrt(2.0 / math.pi) * (y + 0.044715 * (y * y * y))))
    z = _dot(y.astype(BF16), wglu_ref[...])
    y = y * (1.0 / (1.0 + jnp.exp(-z)))
    ms = jnp.mean(y * y, axis=-1, keepdims=True)
    y = y * lax.rsqrt(ms + EPS) * gs_ref[...]
    x1 = x_ref[...] + _dot(y.astype(BF16), wtop_ref[...]) + _dot(oa_ref[...], wbot_ref[...])
    x1_ref[...] = x1
    ms = jnp.mean(x1 * x1, axis=-1, keepdims=True)
    h2 = x1 * lax.rsqrt(ms + EPS) * gf_ref[...]
    tm, d = h2.shape
    nblk = d // 128
    for j in range(nblk):
        h2_ref[pl.ds(j, tm, stride=nblk), :] = h2[:, j * 128:(j + 1) * 128]
    hi, lo = _split_bf16(h2)
    wrh = wrh_ref[...]
    lg = _dot(hi, wrh) + _dot(lo, wrh) + _dot(hi, wrl_ref[...]) + br_ref[...]
    lane = lax.broadcasted_iota(jnp.int32, lg.shape, 1).astype(F32)
    vals, idxs = [], []
    for _ in range(TOP_K):
        m = jnp.max(lg, axis=-1, keepdims=True)
        idx = jnp.min(jnp.where(lg == m, lane, float(lg.shape[1])), axis=-1, keepdims=True)
        vals.append(m)
        idxs.append(idx)
        lg = jnp.where(lane == idx, -3.0e38, lg)
    ex = [jnp.exp(v - vals[0]) for v in vals]
    inv = 1.0 / functools.reduce(lambda a, b: a + b, ex)
    out = jnp.zeros(lg.shape, F32)
    for k in range(TOP_K):
        out = jnp.where(lane == float(k), ex[k] * inv, out)
        out = jnp.where(lane == float(TOP_K + k), idxs[k], out)
    rt_ref[...] = out


def _outproj(ys, oa, x, w_glu, g_ssm, w_out, g_ffn, w_router, b_router):
    t, d = x.shape
    sw = ys.shape[1]
    n_exp = w_router.shape[1]
    lw = max(128, n_exp)
    wr = jnp.zeros((d, lw), F32).at[:, :n_exp].set(w_router)
    wrh, wrl = _split_bf16(wr)
    br = jnp.full((1, lw), NEG_INF, F32).at[0, :n_exp].set(b_router)
    w_out_bf = w_out.astype(BF16)
    tm = TOK_TILE
    row = lambda i: (i, 0)
    fixed = lambda i: (0, 0)
    return pl.pallas_call(
        _outproj_kernel,
        grid=(t // tm,),
        in_specs=[pl.BlockSpec((tm, sw), row), pl.BlockSpec((tm, d - sw), row), pl.BlockSpec((tm, d), row),
                  pl.BlockSpec((sw, sw), fixed), pl.BlockSpec((1, sw), fixed),
                  pl.BlockSpec((sw, d), fixed), pl.BlockSpec((d - sw, d), fixed), pl.BlockSpec((1, d), fixed),
                  pl.BlockSpec((d, lw), fixed), pl.BlockSpec((d, lw), fixed), pl.BlockSpec((1, lw), fixed)],
        out_specs=[pl.BlockSpec((tm, d), row), pl.BlockSpec((tm * (d // 128), 128), row),
                   pl.BlockSpec((tm, lw), row)],
        out_shape=[jax.ShapeDtypeStruct((t, d), F32), jax.ShapeDtypeStruct((t * (d // 128), 128), F32),
                   jax.ShapeDtypeStruct((t, lw), F32)],
        compiler_params=_cparams(("parallel",)),
        name="outproj",
    )(ys, oa, x, w_glu.astype(BF16), g_ssm[None].astype(F32), w_out_bf[:sw], w_out_bf[sw:],
      g_ffn[None].astype(F32), wrh, wrl, br)


def _moe_kernel(te_ref, nt_ref, src_ref, nxt_ref, dprev_ref, h2_hbm, wgu_ref, bgu_ref, wd_ref, bd_ref,
                out_hbm, xbuf, obuf, wgu_bf, wd_bf, gsem, ssem):
    i = pl.program_id(0)
    nt = nt_ref[0]
    slot = i % 2
    other = 1 - slot
    tm = src_ref.shape[2]
    nblk = xbuf.shape[1] // tm

    def token_rows(ref, first_row):
        return ref.at[pl.ds(pl.multiple_of(first_row, nblk), nblk), :]

    def start_gather(idx_ref, s):
        def two(rr, _):
            for p in range(2):
                r = 2 * rr + p
                pltpu.make_async_copy(token_rows(h2_hbm, idx_ref[0, 0, r]), token_rows(xbuf.at[s], r * nblk),
                                      gsem.at[s]).start(priority=p)
            return 0
        lax.fori_loop(0, tm // 2, two, 0, unroll=4)

    def wait_gather(s):
        pltpu.make_async_copy(xbuf.at[s], xbuf.at[s], gsem.at[s]).wait()

    def start_scatter(idx_ref, s, inline=False):
        def one(r, p):
            pltpu.make_async_copy(token_rows(obuf.at[s], r * nblk), token_rows(out_hbm, idx_ref[0, 0, r]),
                                  ssem.at[0]).start(priority=p)

        if inline:
            for r in range(tm):
                one(r, r % 2)
            return

        def two(rr, _):
            for p in range(2):
                one(2 * rr + p, p)
            return 0
        lax.fori_loop(0, tm // 2, two, 0, unroll=4)

    def wait_scatter(s):
        pltpu.make_async_copy(obuf.at[s], obuf.at[s], ssem.at[0]).wait()

    @pl.when(i == 0)
    def _():
        obuf[...] = jnp.zeros(obuf.shape, obuf.dtype)
        start_gather(src_ref, 0)

    @pl.when(i < nt)
    def _():
        wait_gather(slot)

    prev = te_ref[jnp.maximum(i - 1, 0)]
    new_expert = jnp.logical_or(i == 0, te_ref[i] != prev)

    @pl.when(jnp.logical_and(i < nt, new_expert))
    def _():
        wgu_bf[...] = wgu_ref[0].astype(BF16)
        wd_bf[...] = wd_ref[0].astype(BF16)

    @pl.when(i < nt)
    def _():
        start_gather(nxt_ref, other)
        start_scatter(dprev_ref, other, inline=True)
        dff = wd_bf.shape[0]
        xin = xbuf.at[slot]
        x = jnp.concatenate([xin[pl.ds(j, tm, stride=nblk), :].astype(BF16) for j in range(nblk)], axis=1)
        gu = _dot(x, wgu_bf[...]) + bgu_ref[0]
        x_glu = jnp.minimum(gu[:, :dff], SWIGLU_LIMIT)
        x_lin = jnp.clip(gu[:, dff:], -SWIGLU_LIMIT, SWIGLU_LIMIT)
        hdn = x_glu * (1.0 / (1.0 + jnp.exp(-SWIGLU_ALPHA * x_glu))) * (x_lin + 1.0)
        out = _dot(hdn.astype(BF16), wd_bf[...]) + bd_ref[0]
        res = obuf.at[slot]
        for j in range(nblk):
            res[pl.ds(j, tm, stride=nblk), :] = out[:, j * 128:(j + 1) * 128]
        wait_scatter(other)

    @pl.when(i == nt)
    def _():
        wait_gather(slot)
        start_scatter(dprev_ref, other)
        wait_scatter(other)


def _moe_rows(h2, route, layer, w_gate_up, b_gate_up, w_down, b_down):
    tile_expert, n_used, src, dst = route
    depth, n_exp, d, dgu = w_gate_up.shape
    nblk = d // 128
    t = h2.shape[0] // nblk
    dff = w_down.shape[2]
    tm = MOE_TILE
    n_tiles = tile_expert.shape[0]
    by_e = lambda i, te, nt: (layer * n_exp + te[i], 0, 0)
    smem = lambda f: pl.BlockSpec((1, 1, tm), f, memory_space=pltpu.SMEM)
    any_ = pl.BlockSpec(memory_space=pl.ANY)
    return pl.pallas_call(
        _moe_kernel,
        grid_spec=pltpu.PrefetchScalarGridSpec(
            num_scalar_prefetch=2,
            grid=(n_tiles,),
            in_specs=[smem(lambda i, te, nt: (i, 0, 0)), smem(lambda i, te, nt: (i + 1, 0, 0)),
                      smem(lambda i, te, nt: (i, 0, 0)), any_,
                      pl.BlockSpec((1, d, dgu), by_e), pl.BlockSpec((1, 1, dgu), by_e),
                      pl.BlockSpec((1, dff, d), by_e), pl.BlockSpec((1, 1, d), by_e)],
            out_specs=any_,
            scratch_shapes=[pltpu.VMEM((2, tm * nblk, 128), F32), pltpu.VMEM((2, tm * nblk, 128), F32),
                            pltpu.VMEM((d, dgu), BF16), pltpu.VMEM((dff, d), BF16),
                            pltpu.SemaphoreType.DMA((2,)), pltpu.SemaphoreType.DMA((1,))]),
        out_shape=jax.ShapeDtypeStruct(((TOP_K * t + tm) * nblk, 128), F32),
        compiler_params=_cparams(("arbitrary",)),
        name="moe",
    )(tile_expert, n_used, src, src, dst, h2, w_gate_up.reshape(depth * n_exp, d, dgu),
      b_gate_up.reshape(depth * n_exp, 1, dgu), w_down.reshape(depth * n_exp, dff, d),
      b_down.reshape(depth * n_exp, 1, d))


def _moe_route(top_idx, n_exp, nblk):
    t = top_idx.shape[0]
    tm = MOE_TILE
    n_assign = t * TOP_K
    e_flat = top_idx.T.reshape(-1).astype(jnp.int32)
    order = jnp.argsort(e_flat).astype(jnp.int32)
    experts = jnp.arange(n_exp, dtype=jnp.int32)
    counts = jnp.sum((e_flat[:, None] == experts[None, :]).astype(jnp.int32), axis=0)
    starts = jnp.cumsum(counts) - counts
    tiles_e = (counts + tm - 1) // tm
    tile_end = jnp.cumsum(tiles_e)
    tile_beg = tile_end - tiles_e
    n_used = tile_end[-1]
    n_tiles = n_assign // tm + n_exp + 1
    tile = jnp.arange(n_tiles, dtype=jnp.int32)
    tile_c = jnp.minimum(tile, n_used - 1)
    te = jnp.sum((tile_end[None, :] <= tile_c[:, None]).astype(jnp.int32), axis=1)
    te = jnp.minimum(te, n_exp - 1)
    first = starts[te] + (tile - tile_beg[te]) * tm
    n_valid = jnp.where(tile < n_used, jnp.clip(counts[te] - (tile - tile_beg[te]) * tm, 0, tm), 0)
    r = jnp.arange(tm, dtype=jnp.int32)
    valid = r[None, :] < n_valid[:, None]
    a = order[jnp.clip(first[:, None] + r[None, :], 0, n_assign - 1)]
    src = jnp.where(valid, a % t, 0)
    dst = jnp.where(valid, a, n_assign + r[None, :])
    spare = jnp.broadcast_to(n_assign + r[None, :], (1, tm))
    src = jnp.concatenate([src, jnp.zeros((1, tm), jnp.int32)], axis=0)[:, None, :] * nblk
    dst = jnp.concatenate([spare, dst], axis=0)[:, None, :] * nblk
    return te.astype(jnp.int32), n_used.astype(jnp.int32)[None], src, dst


def _combine_kernel(x1_ref, g_ref, o0_ref, o1_ref, o2_ref, o3_ref, x2_ref):
    g = g_ref[...]
    tm, d = x1_ref.shape
    nblk = d // 128
    for j in range(nblk):
        cols = slice(j * 128, (j + 1) * 128)
        acc = x1_ref[:, cols]
        for k, o_ref in enumerate((o0_ref, o1_ref, o2_ref, o3_ref)):
            acc = acc + g[:, k:k + 1] * o_ref[pl.ds(j, tm, stride=nblk), :]
        x2_ref[:, cols] = acc


def _combine(x1, gates, out_rows):
    t, d = x1.shape
    tm = TOK_TILE
    nt = t // tm
    rows = lambda k: pl.BlockSpec((tm * (d // 128), 128), lambda i: (k * nt + i, 0))
    return pl.pallas_call(
        _combine_kernel,
        grid=(nt,),
        in_specs=[pl.BlockSpec((tm, d), lambda i: (i, 0)), pl.BlockSpec((tm, TOP_K), lambda i: (i, 0)),
                  rows(0), rows(1), rows(2), rows(3)],
        out_specs=pl.BlockSpec((tm, d), lambda i: (i, 0)),
        out_shape=jax.ShapeDtypeStruct((t, d), F32),
        compiler_params=_cparams(("parallel",)),
        name="combine",
    )(x1, gates, out_rows, out_rows, out_rows, out_rows)


def _lambda_init(layer):
    return 0.8 - 0.6 * math.exp(-0.3 * layer)


def kernel(x_prompt, x_sample, cache_k, cache_v, state_ssm_re, state_ssm_im, g_mix, w_in, ssm_a_re, ssm_a_im, ssm_log_dt, ssm_b_re, ssm_b_im, ssm_c_re, ssm_c_im, ssm_d, w_glu, g_ssm_out, g_q, g_k, lambda_q1, lambda_k1, lambda_q2, lambda_k2, g_subln, w_out, g_ffn, w_router, b_router, w_gate_up, b_gate_up, w_down, b_down):
    batch, seq, d = x_prompt.shape
    n_streams, dec_seq, _ = x_sample.shape
    depth = w_in.shape[0]
    past_len = cache_k.shape[2]
    n_heads = cache_k.shape[3]
    aw = n_heads * HEAD_W
    n_groups, n_state = ssm_a_re.shape[1:]
    nq = n_groups // SSM_QG
    assert dec_seq == CHUNK and seq % SSM_ROWS == 0 and n_streams % 8 == 0 and n_state == SSM_STATE
    tp = batch * seq
    ts = n_streams * dec_seq
    n_prompt_sc = tp // SSM_ROWS
    sc_per_seq = seq // SSM_ROWS
    n_sample_sc = ts // SSM_ROWS
    slopes = jnp.asarray([2.0 ** (-8.0 * (h + 1) / n_heads) for h in range(n_heads)], F32)

    x = jnp.concatenate([x_prompt.reshape(tp, d), x_sample.reshape(ts, d)], axis=0)
    outs = {name: [] for name in ("srp", "sip", "srs", "sis")}
    kv_out = None
    n_exp = w_router.shape[2]
    for l in range(depth):
        u, q, kb, vb, *kv_out = _inproj(x, g_mix[l], w_in[l].astype(BF16), g_q[l], g_k[l], l, depth,
                                        tp // TOK_TILE, kv_out)

        tables = _ssm_tables(ssm_a_re[l], ssm_a_im[l], ssm_log_dt[l], ssm_b_re[l], ssm_b_im[l],
                             ssm_c_re[l], ssm_c_im[l], ssm_d[l])

        def state_lanes(z):
            return z.reshape(n_sample_sc, 8, nq, SSM_QW).transpose(0, 2, 1, 3)

        h0_s = jnp.concatenate([state_lanes(state_ssm_re[l]), state_lanes(state_ssm_im[l])], axis=-1)
        h0_all = jnp.concatenate([jnp.zeros((n_prompt_sc,) + h0_s.shape[1:], F32), h0_s], axis=0)
        ys, fin = _ssm(u, tables, h0_all, n_prompt_sc, sc_per_seq)

        lam_init = _lambda_init(l)
        lam = (jnp.exp(jnp.sum(lambda_q1[l].astype(F32) * lambda_k1[l].astype(F32)))
               - jnp.exp(jnp.sum(lambda_q2[l].astype(F32) * lambda_k2[l].astype(F32))) + lam_init)[None]
        out_scale = 1.0 - lam_init
        bound = (SCORE_BOUND_SCALE * jnp.max(jnp.abs(g_q[l].astype(F32)))
                 * jnp.max(jnp.abs(g_k[l].astype(F32))))[None]
        o_p = _attn_prompt(q, kb, vb, slopes, lam, bound, g_subln[l], out_scale, batch, seq)
        o_s = _attn_sample(q, kb, vb, cache_k, cache_v, l, slopes, lam, g_subln[l], out_scale, tp)
        oa = jnp.concatenate([o_p, o_s], axis=0)

        x1, h2, routing = _outproj(ys, oa, x, w_glu[l], g_ssm_out[l], w_out[l], g_ffn[l],
                                   w_router[l], b_router[l])
        gates = routing[:, :TOP_K]
        route = _moe_route(routing[:, TOP_K:2 * TOP_K].astype(jnp.int32), n_exp, d // 128)
        x = _combine(x1, gates, _moe_rows(h2, route, l, w_gate_up, b_gate_up, w_down, b_down))

        fin_p = fin[:n_prompt_sc].reshape(batch, sc_per_seq, nq, 8, 2, SSM_QW)[:, -1, :, -1]
        outs["srp"].append(fin_p[:, :, 0].reshape(batch, n_groups, n_state))
        outs["sip"].append(fin_p[:, :, 1].reshape(batch, n_groups, n_state))
        fin_s = fin[n_prompt_sc:].reshape(n_sample_sc, nq, 8, 2, SSM_QW).transpose(0, 2, 3, 1, 4)
        outs["srs"].append(fin_s[:, :, 0].reshape(n_streams, n_groups, n_state))
        outs["sis"].append(fin_s[:, :, 1].reshape(n_streams, n_groups, n_state))

    st = {name: jnp.stack(vals) for name, vals in outs.items()}
    kp, vp, ks, vs = kv_out
    p_shape = (depth, batch, seq, n_heads, HEAD_W)
    s_shape = (depth, n_streams, dec_seq, n_heads, HEAD_W)
    return (x[:tp].reshape(batch, seq, d), x[tp:].reshape(n_streams, dec_seq, d),
            kp.reshape(p_shape), vp.reshape(p_shape), st["srp"], st["sip"],
            ks.reshape(s_shape), vs.reshape(s_shape), st["srs"], st["sis"])
```

```python
import functools
import math

import jax
import jax.numpy as jnp
from jax import lax
from jax.experimental import pallas as pl
from jax.experimental.pallas import tpu as pltpu

F32 = jnp.float32
BF16 = jnp.bfloat16

CHUNK = 64
HEAD_DIM = 64
HEAD_W = 2 * HEAD_DIM
SSM_GROUP = 16
SSM_STATE = 64
SSM_QG = 8
SSM_QW = SSM_QG * SSM_STATE
TOP_K = 4
SWIGLU_ALPHA = 1.702
SWIGLU_LIMIT = 7.0
EPS = 1e-6
NEG_INF = -1e30

TOK_TILE = 512
SSM_ROWS = 8 * CHUNK
ATT_TQ = 512
ATT_TKC = 2048
MOE_TILE = 256
VMEM_LIMIT = 56 * 1024 * 1024


def _cparams(sem):
    return pltpu.CompilerParams(dimension_semantics=sem, vmem_limit_bytes=VMEM_LIMIT)


def _dot(a, b):
    return jnp.dot(a, b, preferred_element_type=F32)


def _dot_t(a, b):
    return lax.dot_general(a, b, (((1,), (1,)), ((), ())), preferred_element_type=F32)


def _split_bf16(x):
    hi = x.astype(BF16)
    lo = (x - hi.astype(F32)).astype(BF16)
    return hi, lo


def _token_tile(x_refs, i, n_prompt_tiles):
    if len(x_refs) == 1:
        return x_refs[0][...]
    is_prompt = (jnp.zeros(x_refs[0].shape, jnp.int32) + (i < n_prompt_tiles).astype(jnp.int32)) > 0
    return jnp.where(is_prompt, x_refs[0][...], x_refs[1][...])


def _token_specs(x_parts, tm, n_prompt_tiles):
    d = x_parts[0].shape[1]
    if len(x_parts) == 1:
        return [pl.BlockSpec((tm, d), lambda i: (i, 0))]
    return [pl.BlockSpec((tm, d), lambda i: (jnp.minimum(i, n_prompt_tiles - 1), 0)),
            pl.BlockSpec((tm, d), lambda i: (jnp.maximum(i - n_prompt_tiles, 0), 0))]


def _inproj_kernel(*refs, n_x, n_prompt_tiles, n_prev):
    x_refs = refs[:n_x]
    g_ref, w_ref, gq_ref, gk_ref, seg_ref = refs[n_x:n_x + 5]
    u_ref, q_ref, kb_ref, vb_ref, kp_ref, vp_ref, ks_ref, vs_ref = refs[n_x + 5 + n_prev:]
    i = pl.program_id(0)
    x = _token_tile(x_refs, i, n_prompt_tiles)
    ms = jnp.mean(x * x, axis=-1, keepdims=True)
    h = (x * lax.rsqrt(ms + EPS) * g_ref[...]).astype(BF16)
    proj = _dot(h, w_ref[...])
    w = u_ref.shape[-1]
    seg = seg_ref[...]

    def head_norm(z, g):
        hi, lo = _split_bf16(z * z)
        ms_ = _dot(hi, seg) + _dot(lo, seg)
        return z * lax.rsqrt(ms_ + EPS) * g

    u_ref[...] = proj[:, :w]
    qn = head_norm(proj[:, w:2 * w], gq_ref[...])
    q_ref[...] = (qn * (HEAD_DIM ** -0.5)).astype(BF16)
    kn = head_norm(proj[:, 2 * w:3 * w], gk_ref[...])
    kb_ref[...] = kn.astype(BF16)
    vv = proj[:, 3 * w:]
    vb_ref[...] = vv.astype(BF16)

    def emit(k_out, v_out):
        tm = kn.shape[0]
        nh = k_out.shape[0] // tm
        for hd in range(nh):
            k_out[pl.ds(hd, tm, stride=nh), :] = kn[:, hd * HEAD_W:(hd + 1) * HEAD_W]
            v_out[pl.ds(hd, tm, stride=nh), :] = vv[:, hd * HEAD_W:(hd + 1) * HEAD_W]

    @pl.when(i < n_prompt_tiles)
    def _():
        emit(kp_ref, vp_ref)

    @pl.when(i >= n_prompt_tiles)
    def _():
        emit(ks_ref, vs_ref)


def _inproj(x_parts, g_mix, w_in_bf, g_q, g_k, layer, depth, n_prompt_tiles, prev):
    t = sum(p.shape[0] for p in x_parts)
    d = x_parts[0].shape[1]
    aw = w_in_bf.shape[1] // 4
    nh = aw // HEAD_W
    nrep = aw // HEAD_DIM
    gq = jnp.tile(g_q.astype(F32), nrep)[None]
    gk = jnp.tile(g_k.astype(F32), nrep)[None]
    ids = jnp.arange(aw) // HEAD_DIM
    seg = jnp.where(ids[:, None] == ids[None, :], 1.0 / HEAD_DIM, 0.0).astype(BF16)
    tm = TOK_TILE
    n_tiles = t // tm
    n_sample_tiles = n_tiles - n_prompt_tiles
    row = lambda i: (i, 0)
    fixed = lambda i: (0, 0)
    p_blk = pl.BlockSpec((None, tm * nh, HEAD_W),
                         lambda i: (layer * n_prompt_tiles + jnp.minimum(i, n_prompt_tiles - 1), 0, 0))
    s_blk = pl.BlockSpec((None, tm * nh, HEAD_W),
                         lambda i: (layer * n_sample_tiles + jnp.maximum(i - n_prompt_tiles, 0), 0, 0))
    p_shape = jax.ShapeDtypeStruct((depth * n_prompt_tiles, tm * nh, HEAD_W), F32)
    s_shape = jax.ShapeDtypeStruct((depth * n_sample_tiles, tm * nh, HEAD_W), F32)
    outs = [jax.ShapeDtypeStruct((t, aw), dt) for dt in (F32, BF16, BF16, BF16)] + [p_shape, p_shape, s_shape, s_shape]
    prev = () if prev is None else tuple(prev)
    n_x = len(x_parts)
    n_in = n_x + 5
    kern = functools.partial(_inproj_kernel, n_x=n_x, n_prompt_tiles=n_prompt_tiles, n_prev=len(prev))
    return pl.pallas_call(
        kern,
        grid=(n_tiles,),
        in_specs=_token_specs(x_parts, tm, n_prompt_tiles)
                 + [pl.BlockSpec((1, d), fixed),
                    pl.BlockSpec(w_in_bf.shape, fixed), pl.BlockSpec((1, aw), fixed),
                    pl.BlockSpec((1, aw), fixed), pl.BlockSpec((aw, aw), fixed)]
                 + [pl.BlockSpec(memory_space=pl.ANY)] * len(prev),
        out_specs=[pl.BlockSpec((tm, aw), row)] * 4 + [p_blk, p_blk, s_blk, s_blk],
        out_shape=outs,
        input_output_aliases={n_in + j: 4 + j for j in range(len(prev))},
        compiler_params=_cparams(("arbitrary",)),
        name="inproj",
    )(*x_parts, g_mix[None].astype(F32), w_in_bf, gq, gk, seg, *prev)


def _ssm_kernel(u_ref, wb_ref, wc_ref, ab_ref, a64_ref, pw_ref, d_ref, h0_ref,
                y_ref, fin_ref, uperm_ref, st_ref, carry_ref, *, n_prompt_sc, sc_per_seq):
    sc = pl.program_id(0)
    qb = pl.program_id(1)
    nsteps = CHUNK
    w = SSM_QW

    for t in range(nsteps):
        uperm_ref[t * 8:(t + 1) * 8, :] = u_ref[pl.ds(t, 8, stride=nsteps), :]
    up = uperm_ref[...]
    st_ref[...] = _dot(up.astype(BF16), wb_ref[0])

    ab = ab_ref[0]
    ar = jnp.broadcast_to(ab[:, :w], (8, w))
    ai = jnp.broadcast_to(ab[:, w:], (8, w))

    def scan_step(t, carry):
        hr, hi = carry
        r0 = pl.multiple_of(t * 8, 8)
        br = st_ref[pl.ds(r0, 8), :w]
        bi = st_ref[pl.ds(r0, 8), w:]
        nr = ar * hr - ai * hi + br
        ni = ar * hi + ai * hr + bi
        st_ref[pl.ds(r0, 8), :w] = nr
        st_ref[pl.ds(r0, 8), w:] = ni
        return nr, ni

    zero = jnp.zeros((8, w), F32)
    er, ei = lax.fori_loop(0, nsteps, scan_step, (zero, zero), unroll=4)

    a64 = a64_ref[0]
    a64r = a64[:, :w]
    a64i = a64[:, w:]
    is_sample = sc >= n_prompt_sc

    @pl.when(jnp.logical_or(is_sample, sc % sc_per_seq == 0))
    def _():
        carry_ref[qb] = jnp.zeros(carry_ref.shape[1:], F32)

    cin = carry_ref[qb]
    cr = cin[:, :w]
    ci = cin[:, w:]
    rows = lax.broadcasted_iota(jnp.int32, (8, w), 0)
    sr = jnp.zeros((8, w), F32)
    si = jnp.zeros((8, w), F32)
    for j in range(8):
        sr = jnp.where(rows == j, cr, sr)
        si = jnp.where(rows == j, ci, si)
        ejr = er[j:j + 1]
        eji = ei[j:j + 1]
        cr, ci = a64r * cr - a64i * ci + ejr, a64r * ci + a64i * cr + eji
    carry_ref[qb] = jnp.concatenate([cr, ci], axis=1)
    h0 = h0_ref[0, 0]
    given = (jnp.zeros((8, w), jnp.int32) + is_sample.astype(jnp.int32)) > 0
    sr = jnp.where(given, h0[:, :w], sr)
    si = jnp.where(given, h0[:, w:], si)
    fr = a64r * sr - a64i * si + er
    fi = a64r * si + a64i * sr + ei
    fin_ref[0, 0] = jnp.concatenate([fr, fi], axis=1)

    def fix_step(t, _):
        r0 = pl.multiple_of(t * 8, 8)
        p = pw_ref[0, pl.ds(t, 1), :]
        pr = p[:, :w]
        pi = p[:, w:]
        st_ref[pl.ds(r0, 8), :w] = st_ref[pl.ds(r0, 8), :w] + (pr * sr - pi * si)
        st_ref[pl.ds(r0, 8), w:] = st_ref[pl.ds(r0, 8), w:] + (pr * si + pi * sr)
        return 0

    lax.fori_loop(0, nsteps, fix_step, 0, unroll=4)

    y = _dot(st_ref[...].astype(BF16), wc_ref[0]) + up * d_ref[0]
    for t in range(nsteps):
        y_ref[pl.ds(t, 8, stride=nsteps), :] = y[t * 8:(t + 1) * 8, :]


def _ssm_tables(a_re, a_im, log_dt, b_re, b_im, c_re, c_im, d_skip):
    g, n = a_re.shape
    c = b_re.shape[-1]
    nq = g // SSM_QG
    dt = jnp.exp(log_dt)[:, None]
    za_re, za_im = dt * a_re, dt * a_im
    mag = jnp.exp(za_re)
    ab_re, ab_im = mag * jnp.cos(za_im), mag * jnp.sin(za_im)
    den = a_re * a_re + a_im * a_im
    n_re, n_im = ab_re - 1.0, ab_im
    f_re = (n_re * a_re + n_im * a_im) / den
    f_im = (n_im * a_re - n_re * a_im) / den
    bb_re = f_re[..., None] * b_re - f_im[..., None] * b_im
    bb_im = f_re[..., None] * b_im + f_im[..., None] * b_re
    eye = jnp.eye(SSM_QG, dtype=F32)

    def in_w(bb):
        bq = bb.reshape(nq, SSM_QG, n, c)
        return jnp.einsum('qgnc,gh->qgchn', bq, eye).reshape(nq, SSM_QG * c, SSM_QG * n)

    def out_w(cc):
        cq = cc.reshape(nq, SSM_QG, c, n)
        return jnp.einsum('qgcn,gh->qgnhc', cq, eye).reshape(nq, SSM_QG * n, SSM_QG * c)

    wb = jnp.concatenate([in_w(bb_re), in_w(bb_im)], axis=2).astype(BF16)
    wc = jnp.concatenate([out_w(c_re), out_w(-c_im)], axis=1).astype(BF16)

    def lanes(z):
        return jnp.moveaxis(z.reshape(z.shape[:-2] + (nq, SSM_QG * n)), -2, 0)

    def power(k):
        m = jnp.exp(k * za_re)
        return m * jnp.cos(k * za_im), m * jnp.sin(k * za_im)

    ab = jnp.concatenate([lanes(ab_re), lanes(ab_im)], axis=-1)[:, None]
    p64 = power(float(CHUNK))
    a64 = jnp.concatenate([lanes(p64[0]), lanes(p64[1])], axis=-1)[:, None]
    ks = jnp.arange(1, CHUNK + 1, dtype=F32)[:, None, None]
    pk = power(ks)
    pw = jnp.concatenate([lanes(pk[0]), lanes(pk[1])], axis=-1)
    dq = d_skip.reshape(nq, 1, SSM_QG * c).astype(F32)
    return wb, wc, ab, a64, pw, dq


def _ssm(u, tables, h0_all, n_prompt_sc, sc_per_seq):
    wb, wc, ab, a64, pw, dq = tables
    t, cw = u.shape
    nq = wb.shape[0]
    n_sc = t // SSM_ROWS
    sw = 2 * SSM_QW
    kern = functools.partial(_ssm_kernel, n_prompt_sc=n_prompt_sc, sc_per_seq=sc_per_seq)
    per_q = lambda s, q: (q, 0, 0)
    return pl.pallas_call(
        kern,
        grid=(n_sc, nq),
        in_specs=[pl.BlockSpec((SSM_ROWS, 128), lambda s, q: (s, q)),
                  pl.BlockSpec((1,) + wb.shape[1:], per_q), pl.BlockSpec((1,) + wc.shape[1:], per_q),
                  pl.BlockSpec((1, 1, sw), per_q), pl.BlockSpec((1, 1, sw), per_q),
                  pl.BlockSpec((1, CHUNK, sw), per_q), pl.BlockSpec((1, 1, 128), per_q),
                  pl.BlockSpec((1, 1, 8, sw), lambda s, q: (s, q, 0, 0))],
        out_specs=[pl.BlockSpec((SSM_ROWS, 128), lambda s, q: (s, q)),
                   pl.BlockSpec((1, 1, 8, sw), lambda s, q: (s, q, 0, 0))],
        out_shape=[jax.ShapeDtypeStruct((t, cw), F32),
                   jax.ShapeDtypeStruct((n_sc, nq, 8, sw), F32)],
        scratch_shapes=[pltpu.VMEM((SSM_ROWS, 128), F32), pltpu.VMEM((SSM_ROWS, sw), F32),
                        pltpu.VMEM((nq, 1, sw), F32)],
        compiler_params=_cparams(("arbitrary", "arbitrary")),
        name="ssm",
    )(u, wb, wc, ab, a64, pw, dq, h0_all)


def _stack_q(q):
    lane = lax.broadcasted_iota(jnp.int32, q.shape, 1)
    zero = jnp.zeros_like(q)
    return jnp.concatenate([jnp.where(lane < HEAD_DIM, q, zero), jnp.where(lane >= HEAD_DIM, q, zero)], axis=0)


def _online_update(s, v, m_ref, l_ref, acc_ref, fixed_max=False):
    if fixed_max:
        acc_ref[...] = acc_ref[...] + _dot(jnp.exp(s).astype(BF16), v)
        return
    m_old = m_ref[...]
    m_new = jnp.maximum(m_old, jnp.max(s, axis=-1, keepdims=True))
    alpha = jnp.exp(m_old - m_new)
    p = jnp.exp(s - m_new)
    l_ref[...] = alpha * l_ref[...] + jnp.sum(p, axis=-1, keepdims=True)
    acc_ref[...] = alpha * acc_ref[...] + _dot(p.astype(BF16), v)
    m_ref[...] = m_new


SCORE_BOUND_SCALE = 1.02 * HEAD_DIM ** 0.5
FIXED_MAX_LIMIT = 30.0


def _finish_head(m_ref, l_ref, acc_ref, lam, g, out_scale, tq):
    acc = acc_ref[...]
    if acc.shape[1] > HEAD_W:
        l = acc[:, HEAD_W:HEAD_W + 1]
        acc = acc[:, :HEAD_W]
    else:
        l = l_ref[...]
    o = acc[:tq] / l[:tq] - lam * (acc[tq:] / l[tq:])
    ms = jnp.mean(o * o, axis=-1, keepdims=True)
    return o * lax.rsqrt(ms + EPS) * g * out_scale


def _attn_prompt_kernel(slope_ref, lam_ref, bound_ref, q_ref, k_ref, v_ref, bias_ref, g_ref, o_ref,
                        m_ref, l_ref, acc_ref, *, out_scale, fixed_max):
    h = pl.program_id(1)
    i = pl.program_id(2)
    tq = q_ref.shape[0]
    tk = tq
    hk = tk // 2
    slope = slope_ref[h]
    shift = bound_ref[0] if fixed_max else 0.0
    qq = _stack_q(q_ref[...])
    m_ref[...] = jnp.full(m_ref.shape, NEG_INF, F32)
    l_ref[...] = jnp.zeros(l_ref.shape, F32)
    acc_ref[...] = jnp.zeros(acc_ref.shape, F32)
    lane = lax.broadcasted_iota(jnp.int32, (hk, HEAD_W), 1)
    ones_col = jnp.where(lane == 0, 1.0, 0.0).astype(BF16)

    def key_tile(j):
        which = jnp.where(j == i, 1, 0)
        offset = slope * ((i - j) * tq).astype(F32) + shift
        for half in range(2):
            k0 = pl.multiple_of(j * tk + half * hk, hk)
            tile = bias_ref[which, :, half * hk:(half + 1) * hk] - offset
            s = _dot_t(qq, k_ref[pl.ds(k0, hk), :]) + jnp.concatenate([tile, tile], axis=0)
            v = v_ref[pl.ds(k0, hk), :]
            if fixed_max:
                v = jnp.concatenate([v, ones_col], axis=1)
            _online_update(s, v, m_ref, l_ref, acc_ref, fixed_max)

    def two_tiles(jj, _):
        key_tile(2 * jj)
        key_tile(2 * jj + 1)
        return 0

    n_tiles = i + 1
    lax.fori_loop(0, n_tiles >> 1, two_tiles, 0)

    @pl.when((n_tiles & 1) == 1)
    def _():
        key_tile(i)

    o_ref[...] = _finish_head(m_ref, l_ref, acc_ref, lam_ref[0], g_ref[...], out_scale, tq).astype(o_ref.dtype)


def _attn_prompt(q, kb, vb, slopes, lam, bound, g_subln, out_scale, batch, seq):
    n_heads = q.shape[1] // HEAD_W
    tq = ATT_TQ
    nq = seq // tq
    smem = pl.BlockSpec(memory_space=pltpu.SMEM)
    pos = jnp.arange(tq, dtype=jnp.int32)
    rel = (pos[:, None] - pos[None, :]).astype(F32)
    visible = (pos[None, :] // CHUNK) <= (pos[:, None] // CHUNK)
    sl = slopes[:, None, None]
    bias = jnp.stack([-sl * rel[None], jnp.where(visible[None], -sl * jnp.abs(rel)[None], NEG_INF)], axis=1)

    def call(fixed_max):
        return pl.pallas_call(
            functools.partial(_attn_prompt_kernel, out_scale=out_scale, fixed_max=fixed_max),
            grid=(batch, n_heads, nq),
            in_specs=[smem, smem, smem,
                      pl.BlockSpec((tq, HEAD_W), lambda b, h, i: (b * nq + i, h)),
                      pl.BlockSpec((seq, HEAD_W), lambda b, h, i: (b, h)),
                      pl.BlockSpec((seq, HEAD_W), lambda b, h, i: (b, h)),
                      pl.BlockSpec((None, 2, tq, tq), lambda b, h, i: (h, 0, 0, 0)),
                      pl.BlockSpec((1, HEAD_W), lambda b, h, i: (0, 0))],
            out_specs=pl.BlockSpec((tq, HEAD_W), lambda b, h, i: (b * nq + i, h)),
            out_shape=jax.ShapeDtypeStruct(q.shape, BF16),
            scratch_shapes=[pltpu.VMEM((2 * tq, 1), F32), pltpu.VMEM((2 * tq, 1), F32),
                            pltpu.VMEM((2 * tq, 2 * HEAD_W if fixed_max else HEAD_W), F32)],
            compiler_params=_cparams(("parallel", "parallel", "arbitrary")),
            name="attn_prompt_fixed" if fixed_max else "attn_prompt",
        )(slopes, lam, bound, q, kb, vb, bias, g_subln[None].astype(F32))

    return lax.cond(bound[0] <= FIXED_MAX_LIMIT, lambda: call(True), lambda: call(False))


def _attn_sample_kernel(slope_ref, lam_ref, q_ref, kn_ref, vn_ref, kc_ref, vc_ref, g_ref, shared_ref, o_ref,
                        m_ref, l_ref, acc_ref, *, out_scale, past_len):
    del shared_ref
    j = pl.program_id(1)
    nj = pl.num_programs(1)
    tq = q_ref.shape[0]
    n_heads = q_ref.shape[1] // HEAD_W
    tk = kc_ref.shape[0] // n_heads

    @pl.when(j == 0)
    def _():
        m_ref[...] = jnp.full(m_ref.shape, NEG_INF, F32)
        l_ref[...] = jnp.zeros(l_ref.shape, F32)
        acc_ref[...] = jnp.zeros(acc_ref.shape, F32)

    rq = lax.broadcasted_iota(jnp.int32, (2 * tq, 1), 0)
    rq = jnp.where(rq >= tq, rq - tq, rq)
    qpos = (past_len + rq).astype(F32)
    kpos = (j * tk + lax.broadcasted_iota(jnp.int32, (1, tk), 1)).astype(F32)
    for h in range(n_heads):
        cols = slice(h * HEAD_W, (h + 1) * HEAD_W)
        qq = _stack_q(q_ref[:, cols])
        k = kc_ref[pl.ds(h, tk, stride=n_heads), :].astype(BF16)
        v = vc_ref[pl.ds(h, tk, stride=n_heads), :].astype(BF16)
        slope = slope_ref[h]
        s = (_dot_t(qq, k) + slope * kpos) - slope * qpos
        _online_update(s, v, m_ref.at[h], l_ref.at[h], acc_ref.at[h])

    @pl.when(j == nj - 1)
    def _():
        r = lax.broadcasted_iota(jnp.int32, (2 * tq, tq), 0)
        r = jnp.where(r >= tq, r - tq, r)
        c = lax.broadcasted_iota(jnp.int32, (2 * tq, tq), 1)
        dist = jnp.abs(r - c).astype(F32)
        for h in range(n_heads):
            cols = slice(h * HEAD_W, (h + 1) * HEAD_W)
            qq = _stack_q(q_ref[:, cols])
            s = _dot_t(qq, kn_ref[:, cols]) - slope_ref[h] * dist
            _online_update(s, vn_ref[:, cols], m_ref.at[h], l_ref.at[h], acc_ref.at[h])
            o_ref[:, cols] = _finish_head(m_ref.at[h], l_ref.at[h], acc_ref.at[h], lam_ref[0],
                                          g_ref[...], out_scale, tq).astype(o_ref.dtype)


def _attn_sample(q, kb, vb, cache_k, cache_v, layer, slopes, lam, g_subln, out_scale, row0, o_shared):
    depth, n_streams, past_len, n_heads, _ = cache_k.shape
    aw = n_heads * HEAD_W
    tq = CHUNK
    tk = min(ATT_TKC, past_len)
    blk0 = row0 // tq
    smem = pl.BlockSpec(memory_space=pltpu.SMEM)
    new = pl.BlockSpec((tq, aw), lambda s, j: (blk0 + s, 0))
    past = pl.BlockSpec((None, None, tk * n_heads, HEAD_W), lambda s, j: (layer, s, j, 0))
    ck = cache_k.reshape(depth, n_streams, past_len * n_heads, HEAD_W)
    cv = cache_v.reshape(depth, n_streams, past_len * n_heads, HEAD_W)

    return pl.pallas_call(
        functools.partial(_attn_sample_kernel, out_scale=out_scale, past_len=past_len),
        grid=(n_streams, past_len // tk),
        in_specs=[smem, smem, new, new, new, past, past,
                  pl.BlockSpec((1, HEAD_W), lambda s, j: (0, 0)), pl.BlockSpec(memory_space=pl.ANY)],
        out_specs=pl.BlockSpec((tq, aw), lambda s, j: (blk0 + s, 0)),
        out_shape=jax.ShapeDtypeStruct(o_shared.shape, BF16),
        input_output_aliases={8: 0},
        scratch_shapes=[pltpu.VMEM((n_heads, 2 * tq, 1), F32), pltpu.VMEM((n_heads, 2 * tq, 1), F32),
                        pltpu.VMEM((n_heads, 2 * tq, HEAD_W), F32)],
        compiler_params=_cparams(("parallel", "arbitrary")),
        name="attn_sample",
    )(slopes, lam, q, kb, vb, ck, cv, g_subln[None].astype(F32), o_shared)


def _outproj_kernel(ys_ref, oa_ref, *refs, n_x, n_prompt_tiles):
    x_refs = refs[:n_x]
    wglu_ref, gs_ref, wtop_ref, wbot_ref, gf_ref, wrh_ref, wrl_ref, br_ref, x1_ref, h2_ref, rt_ref = refs[n_x:]
    y = ys_ref[...]
    y = 0.5 * y * (1.0 + jnp.tanh(math.sqrt(2.0 / math.pi) * (y + 0.044715 * (y * y * y))))
    z = _dot(y.astype(BF16), wglu_ref[...])
    y = y * (1.0 / (1.0 + jnp.exp(-z)))
    ms = jnp.mean(y * y, axis=-1, keepdims=True)
    y = y * lax.rsqrt(ms + EPS) * gs_ref[...]
    x = _token_tile(x_refs, pl.program_id(0), n_prompt_tiles)
    x1 = x + _dot(y.astype(BF16), wtop_ref[...]) + _dot(oa_ref[...], wbot_ref[...])
    x1_ref[...] = x1
    ms = jnp.mean(x1 * x1, axis=-1, keepdims=True)
    h2 = x1 * lax.rsqrt(ms + EPS) * gf_ref[...]
    tm, d = h2.shape
    nblk = d // 128
    for j in range(nblk):
        h2_ref[pl.ds(j, tm, stride=nblk), :] = h2[:, j * 128:(j + 1) * 128]
    hi, lo = _split_bf16(h2)
    wrh = wrh_ref[...]
    lg = _dot(hi, wrh) + _dot(lo, wrh) + _dot(hi, wrl_ref[...]) + br_ref[...]
    lane = lax.broadcasted_iota(jnp.int32, lg.shape, 1).astype(F32)
    vals, idxs = [], []
    for _ in range(TOP_K):
        m = jnp.max(lg, axis=-1, keepdims=True)
        idx = jnp.min(jnp.where(lg == m, lane, float(lg.shape[1])), axis=-1, keepdims=True)
        vals.append(m)
        idxs.append(idx)
        lg = jnp.where(lane == idx, -3.0e38, lg)
    ex = [jnp.exp(v - vals[0]) for v in vals]
    inv = 1.0 / functools.reduce(lambda a, b: a + b, ex)
    out = jnp.zeros(lg.shape, F32)
    for k in range(TOP_K):
        out = jnp.where(lane == float(k), ex[k] * inv, out)
        out = jnp.where(lane == float(TOP_K + k), idxs[k], out)
    rt_ref[...] = out


def _outproj(ys, oa, x_parts, n_prompt_tiles, w_glu, g_ssm, w_out, g_ffn, w_router, b_router):
    t, sw = ys.shape
    d = x_parts[0].shape[1]
    n_exp = w_router.shape[1]
    lw = max(128, n_exp)
    wr = jnp.zeros((d, lw), F32).at[:, :n_exp].set(w_router)
    wrh, wrl = _split_bf16(wr)
    br = jnp.full((1, lw), NEG_INF, F32).at[0, :n_exp].set(b_router)
    w_out_bf = w_out.astype(BF16)
    tm = TOK_TILE
    row = lambda i: (i, 0)
    fixed = lambda i: (0, 0)
    return pl.pallas_call(
        functools.partial(_outproj_kernel, n_x=len(x_parts), n_prompt_tiles=n_prompt_tiles),
        grid=(t // tm,),
        in_specs=[pl.BlockSpec((tm, sw), row), pl.BlockSpec((tm, d - sw), row)]
                 + _token_specs(x_parts, tm, n_prompt_tiles)
                 + [pl.BlockSpec((sw, sw), fixed), pl.BlockSpec((1, sw), fixed),
                    pl.BlockSpec((sw, d), fixed), pl.BlockSpec((d - sw, d), fixed), pl.BlockSpec((1, d), fixed),
                    pl.BlockSpec((d, lw), fixed), pl.BlockSpec((d, lw), fixed), pl.BlockSpec((1, lw), fixed)],
        out_specs=[pl.BlockSpec((tm, d), row), pl.BlockSpec((tm * (d // 128), 128), row),
                   pl.BlockSpec((tm, lw), row)],
        out_shape=[jax.ShapeDtypeStruct((t, d), F32), jax.ShapeDtypeStruct((t * (d // 128), 128), F32),
                   jax.ShapeDtypeStruct((t, lw), F32)],
        compiler_params=_cparams(("parallel",)),
        name="outproj",
    )(ys, oa, *x_parts, w_glu.astype(BF16), g_ssm[None].astype(F32), w_out_bf[:sw], w_out_bf[sw:],
      g_ffn[None].astype(F32), wrh, wrl, br)


def _moe_kernel(te_ref, nt_ref, src_ref, nxt_ref, dprev_ref, h2_hbm, wgu_ref, bgu_ref, wd_ref, bd_ref,
                out_hbm, xbuf, obuf, wgu_bf, wd_bf, gsem, ssem):
    i = pl.program_id(0)
    nt = nt_ref[0]
    slot = i % 2
    other = 1 - slot
    tm = src_ref.shape[2]
    nblk = xbuf.shape[1] // tm

    def token_rows(ref, first_row):
        return ref.at[pl.ds(pl.multiple_of(first_row, nblk), nblk), :]

    def start_gather(idx_ref, s):
        def two(rr, _):
            for p in range(2):
                r = 2 * rr + p
                pltpu.make_async_copy(token_rows(h2_hbm, idx_ref[0, 0, r]), token_rows(xbuf.at[s], r * nblk),
                                      gsem.at[s]).start(priority=p)
            return 0
        lax.fori_loop(0, tm // 2, two, 0, unroll=4)

    def wait_gather(s):
        pltpu.make_async_copy(xbuf.at[s], xbuf.at[s], gsem.at[s]).wait()

    def start_scatter(idx_ref, s, inline=False):
        def one(r, p):
            pltpu.make_async_copy(token_rows(obuf.at[s], r * nblk), token_rows(out_hbm, idx_ref[0, 0, r]),
                                  ssem.at[0]).start(priority=p)

        if inline:
            for r in range(tm):
                one(r, r % 2)
            return

        def two(rr, _):
            for p in range(2):
                one(2 * rr + p, p)
            return 0
        lax.fori_loop(0, tm // 2, two, 0, unroll=4)

    def wait_scatter(s):
        pltpu.make_async_copy(obuf.at[s], obuf.at[s], ssem.at[0]).wait()

    @pl.when(i == 0)
    def _():
        obuf[...] = jnp.zeros(obuf.shape, obuf.dtype)
        start_gather(src_ref, 0)

    @pl.when(i < nt)
    def _():
        wait_gather(slot)

    prev = te_ref[jnp.maximum(i - 1, 0)]
    new_expert = jnp.logical_or(i == 0, te_ref[i] != prev)

    @pl.when(jnp.logical_and(i < nt, new_expert))
    def _():
        wgu_bf[...] = wgu_ref[0].astype(BF16)
        wd_bf[...] = wd_ref[0].astype(BF16)

    @pl.when(i < nt)
    def _():
        start_gather(nxt_ref, other)
        start_scatter(dprev_ref, other, inline=True)
        dff = wd_bf.shape[0]
        xin = xbuf.at[slot]
        x = jnp.concatenate([xin[pl.ds(j, tm, stride=nblk), :].astype(BF16) for j in range(nblk)], axis=1)
        gu = _dot(x, wgu_bf[...]) + bgu_ref[0]
        x_glu = jnp.minimum(gu[:, :dff], SWIGLU_LIMIT)
        x_lin = jnp.clip(gu[:, dff:], -SWIGLU_LIMIT, SWIGLU_LIMIT)
        hdn = x_glu * (1.0 / (1.0 + jnp.exp(-SWIGLU_ALPHA * x_glu))) * (x_lin + 1.0)
        out = _dot(hdn.astype(BF16), wd_bf[...]) + bd_ref[0]
        res = obuf.at[slot]
        for j in range(nblk):
            res[pl.ds(j, tm, stride=nblk), :] = out[:, j * 128:(j + 1) * 128]
        wait_scatter(other)

    @pl.when(i == nt)
    def _():
        wait_gather(slot)
        start_scatter(dprev_ref, other)
        wait_scatter(other)


def _moe_rows(h2, route, layer, w_gate_up, b_gate_up, w_down, b_down):
    tile_expert, n_used, src, dst = route
    depth, n_exp, d, dgu = w_gate_up.shape
    nblk = d // 128
    t = h2.shape[0] // nblk
    dff = w_down.shape[2]
    tm = MOE_TILE
    n_tiles = tile_expert.shape[0]
    by_e = lambda i, te, nt: (layer * n_exp + te[i], 0, 0)
    smem = lambda f: pl.BlockSpec((1, 1, tm), f, memory_space=pltpu.SMEM)
    any_ = pl.BlockSpec(memory_space=pl.ANY)
    return pl.pallas_call(
        _moe_kernel,
        grid_spec=pltpu.PrefetchScalarGridSpec(
            num_scalar_prefetch=2,
            grid=(n_tiles,),
            in_specs=[smem(lambda i, te, nt: (i, 0, 0)), smem(lambda i, te, nt: (i + 1, 0, 0)),
                      smem(lambda i, te, nt: (i, 0, 0)), any_,
                      pl.BlockSpec((1, d, dgu), by_e), pl.BlockSpec((1, 1, dgu), by_e),
                      pl.BlockSpec((1, dff, d), by_e), pl.BlockSpec((1, 1, d), by_e)],
            out_specs=any_,
            scratch_shapes=[pltpu.VMEM((2, tm * nblk, 128), F32), pltpu.VMEM((2, tm * nblk, 128), F32),
                            pltpu.VMEM((d, dgu), BF16), pltpu.VMEM((dff, d), BF16),
                            pltpu.SemaphoreType.DMA((2,)), pltpu.SemaphoreType.DMA((1,))]),
        out_shape=jax.ShapeDtypeStruct(((TOP_K * t + tm) * nblk, 128), F32),
        compiler_params=_cparams(("arbitrary",)),
        name="moe",
    )(tile_expert, n_used, src, src, dst, h2, w_gate_up.reshape(depth * n_exp, d, dgu),
      b_gate_up.reshape(depth * n_exp, 1, dgu), w_down.reshape(depth * n_exp, dff, d),
      b_down.reshape(depth * n_exp, 1, d))


def _moe_route(top_idx, n_exp, nblk):
    t = top_idx.shape[0]
    tm = MOE_TILE
    n_assign = t * TOP_K
    e_flat = top_idx.T.reshape(-1).astype(jnp.int32)
    order = jnp.argsort(e_flat).astype(jnp.int32)
    experts = jnp.arange(n_exp, dtype=jnp.int32)
    counts = jnp.sum((e_flat[:, None] == experts[None, :]).astype(jnp.int32), axis=0)
    starts = jnp.cumsum(counts) - counts
    tiles_e = (counts + tm - 1) // tm
    tile_end = jnp.cumsum(tiles_e)
    tile_beg = tile_end - tiles_e
    n_used = tile_end[-1]
    n_tiles = n_assign // tm + n_exp + 1
    tile = jnp.arange(n_tiles, dtype=jnp.int32)
    tile_c = jnp.minimum(tile, n_used - 1)
    te = jnp.sum((tile_end[None, :] <= tile_c[:, None]).astype(jnp.int32), axis=1)
    te = jnp.minimum(te, n_exp - 1)
    first = starts[te] + (tile - tile_beg[te]) * tm
    n_valid = jnp.where(tile < n_used, jnp.clip(counts[te] - (tile - tile_beg[te]) * tm, 0, tm), 0)
    r = jnp.arange(tm, dtype=jnp.int32)
    valid = r[None, :] < n_valid[:, None]
    a = order[jnp.clip(first[:, None] + r[None, :], 0, n_assign - 1)]
    src = jnp.where(valid, a % t, 0)
    dst = jnp.where(valid, a, n_assign + r[None, :])
    spare = jnp.broadcast_to(n_assign + r[None, :], (1, tm))
    src = jnp.concatenate([src, jnp.zeros((1, tm), jnp.int32)], axis=0)[:, None, :] * nblk
    dst = jnp.concatenate([spare, dst], axis=0)[:, None, :] * nblk
    return te.astype(jnp.int32), n_used.astype(jnp.int32)[None], src, dst


def _combine_kernel(x1_ref, g_ref, o0_ref, o1_ref, o2_ref, o3_ref, *out_refs, n_prompt_tiles):
    g = g_ref[...]
    tm, d = x1_ref.shape
    nblk = d // 128

    def emit(x2_ref):
        for j in range(nblk):
            cols = slice(j * 128, (j + 1) * 128)
            acc = x1_ref[:, cols]
            for k, o_ref in enumerate((o0_ref, o1_ref, o2_ref, o3_ref)):
                acc = acc + g[:, k:k + 1] * o_ref[pl.ds(j, tm, stride=nblk), :]
            x2_ref[:, cols] = acc

    if len(out_refs) == 1:
        emit(out_refs[0])
        return
    i = pl.program_id(0)

    @pl.when(i < n_prompt_tiles)
    def _():
        emit(out_refs[0])

    @pl.when(i >= n_prompt_tiles)
    def _():
        emit(out_refs[1])


def _combine(x1, gates, out_rows, n_prompt_tiles, split):
    t, d = x1.shape
    tm = TOK_TILE
    nt = t // tm
    rows = lambda k: pl.BlockSpec((tm * (d // 128), 128), lambda i: (k * nt + i, 0))
    if split:
        out_specs = [pl.BlockSpec((tm, d), lambda i: (jnp.minimum(i, n_prompt_tiles - 1), 0)),
                     pl.BlockSpec((tm, d), lambda i: (jnp.maximum(i - n_prompt_tiles, 0), 0))]
        out_shape = [jax.ShapeDtypeStruct((n_prompt_tiles * tm, d), F32),
                     jax.ShapeDtypeStruct(((nt - n_prompt_tiles) * tm, d), F32)]
    else:
        out_specs = pl.BlockSpec((tm, d), lambda i: (i, 0))
        out_shape = jax.ShapeDtypeStruct((t, d), F32)
    return pl.pallas_call(
        functools.partial(_combine_kernel, n_prompt_tiles=n_prompt_tiles),
        grid=(nt,),
        in_specs=[pl.BlockSpec((tm, d), lambda i: (i, 0)), pl.BlockSpec((tm, TOP_K), lambda i: (i, 0)),
                  rows(0), rows(1), rows(2), rows(3)],
        out_specs=out_specs,
        out_shape=out_shape,
        compiler_params=_cparams(("arbitrary",) if split else ("parallel",)),
        name="combine",
    )(x1, gates, out_rows, out_rows, out_rows, out_rows)


def _lambda_init(layer):
    return 0.8 - 0.6 * math.exp(-0.3 * layer)


def kernel(x_prompt, x_sample, cache_k, cache_v, state_ssm_re, state_ssm_im, g_mix, w_in, ssm_a_re, ssm_a_im, ssm_log_dt, ssm_b_re, ssm_b_im, ssm_c_re, ssm_c_im, ssm_d, w_glu, g_ssm_out, g_q, g_k, lambda_q1, lambda_k1, lambda_q2, lambda_k2, g_subln, w_out, g_ffn, w_router, b_router, w_gate_up, b_gate_up, w_down, b_down):
    batch, seq, d = x_prompt.shape
    n_streams, dec_seq, _ = x_sample.shape
    depth = w_in.shape[0]
    past_len = cache_k.shape[2]
    n_heads = cache_k.shape[3]
    aw = n_heads * HEAD_W
    n_groups, n_state = ssm_a_re.shape[1:]
    nq = n_groups // SSM_QG
    assert dec_seq == CHUNK and seq % SSM_ROWS == 0 and n_streams % 8 == 0 and n_state == SSM_STATE
    tp = batch * seq
    ts = n_streams * dec_seq
    n_prompt_sc = tp // SSM_ROWS
    sc_per_seq = seq // SSM_ROWS
    n_sample_sc = ts // SSM_ROWS
    slopes = jnp.asarray([2.0 ** (-8.0 * (h + 1) / n_heads) for h in range(n_heads)], F32)

    x_parts = (x_prompt.reshape(tp, d), x_sample.reshape(ts, d))
    outs = {name: [] for name in ("srp", "sip", "srs", "sis")}
    kv_out = None
    n_exp = w_router.shape[2]
    n_prompt_tiles = tp // TOK_TILE
    for l in range(depth):
        u, q, kb, vb, *kv_out = _inproj(x_parts, g_mix[l], w_in[l].astype(BF16), g_q[l], g_k[l], l, depth,
                                        n_prompt_tiles, kv_out)

        tables = _ssm_tables(ssm_a_re[l], ssm_a_im[l], ssm_log_dt[l], ssm_b_re[l], ssm_b_im[l],
                             ssm_c_re[l], ssm_c_im[l], ssm_d[l])

        def state_lanes(z):
            return z.reshape(n_sample_sc, 8, nq, SSM_QW).transpose(0, 2, 1, 3)

        h0_s = jnp.concatenate([state_lanes(state_ssm_re[l]), state_lanes(state_ssm_im[l])], axis=-1)
        h0_all = jnp.concatenate([jnp.zeros((n_prompt_sc,) + h0_s.shape[1:], F32), h0_s], axis=0)
        ys, fin = _ssm(u, tables, h0_all, n_prompt_sc, sc_per_seq)

        lam_init = _lambda_init(l)
        lam = (jnp.exp(jnp.sum(lambda_q1[l].astype(F32) * lambda_k1[l].astype(F32)))
               - jnp.exp(jnp.sum(lambda_q2[l].astype(F32) * lambda_k2[l].astype(F32))) + lam_init)[None]
        out_scale = 1.0 - lam_init
        bound = (SCORE_BOUND_SCALE * jnp.max(jnp.abs(g_q[l].astype(F32)))
                 * jnp.max(jnp.abs(g_k[l].astype(F32))))[None]
        oa = _attn_prompt(q, kb, vb, slopes, lam, bound, g_subln[l], out_scale, batch, seq)
        oa = _attn_sample(q, kb, vb, cache_k, cache_v, l, slopes, lam, g_subln[l], out_scale, tp, oa)

        x1, h2, routing = _outproj(ys, oa, x_parts, n_prompt_tiles, w_glu[l], g_ssm_out[l], w_out[l], g_ffn[l],
                                   w_router[l], b_router[l])
        gates = routing[:, :TOP_K]
        route = _moe_route(routing[:, TOP_K:2 * TOP_K].astype(jnp.int32), n_exp, d // 128)
        x_new = _combine(x1, gates, _moe_rows(h2, route, l, w_gate_up, b_gate_up, w_down, b_down),
                         n_prompt_tiles, split=(l == depth - 1))
        x_parts = tuple(x_new) if l == depth - 1 else (x_new,)

        fin_p = fin[:n_prompt_sc].reshape(batch, sc_per_seq, nq, 8, 2, SSM_QW)[:, -1, :, -1]
        outs["srp"].append(fin_p[:, :, 0].reshape(batch, n_groups, n_state))
        outs["sip"].append(fin_p[:, :, 1].reshape(batch, n_groups, n_state))
        fin_s = fin[n_prompt_sc:].reshape(n_sample_sc, nq, 8, 2, SSM_QW).transpose(0, 2, 3, 1, 4)
        outs["srs"].append(fin_s[:, :, 0].reshape(n_streams, n_groups, n_state))
        outs["sis"].append(fin_s[:, :, 1].reshape(n_streams, n_groups, n_state))

    st = {name: jnp.stack(vals) for name, vals in outs.items()}
    kp, vp, ks, vs = kv_out
    p_shape = (depth, batch, seq, n_heads, HEAD_W)
    s_shape = (depth, n_streams, dec_seq, n_heads, HEAD_W)
    y_prompt, y_sample = x_parts
    return (y_prompt.reshape(batch, seq, d), y_sample.reshape(n_streams, dec_seq, d),
            kp.reshape(p_shape), vp.reshape(p_shape), st["srp"], st["sip"],
            ks.reshape(s_shape), vs.reshape(s_shape), st["srs"], st["sis"])
```

```python
import functools
import math

import jax
import jax.numpy as jnp
from jax import lax
from jax.experimental import pallas as pl
from jax.experimental.pallas import tpu as pltpu

F32 = jnp.float32
BF16 = jnp.bfloat16

CHUNK = 64
HEAD_DIM = 64
HEAD_W = 2 * HEAD_DIM
SSM_GROUP = 16
SSM_STATE = 64
SSM_QG = 8
SSM_QW = SSM_QG * SSM_STATE
TOP_K = 4
SWIGLU_ALPHA = 1.702
SWIGLU_LIMIT = 7.0
EPS = 1e-6
NEG_INF = -1e30

TOK_TILE = 512
SSM_ROWS = 8 * CHUNK
ATT_TQ = 512
ATT_TKC = 2048
MOE_TILE = 256
VMEM_LIMIT = 56 * 1024 * 1024


def _cparams(sem):
    return pltpu.CompilerParams(dimension_semantics=sem, vmem_limit_bytes=VMEM_LIMIT)


def _dot(a, b):
    return jnp.dot(a, b, preferred_element_type=F32)


def _dot_t(a, b):
    return lax.dot_general(a, b, (((1,), (1,)), ((), ())), preferred_element_type=F32)


def _split_bf16(x):
    hi = x.astype(BF16)
    lo = (x - hi.astype(F32)).astype(BF16)
    return hi, lo


def _token_tile(x_refs, i, n_prompt_tiles):
    if len(x_refs) == 1:
        return x_refs[0][...]
    is_prompt = (jnp.zeros(x_refs[0].shape, jnp.int32) + (i < n_prompt_tiles).astype(jnp.int32)) > 0
    return jnp.where(is_prompt, x_refs[0][...], x_refs[1][...])


def _token_specs(x_parts, tm, n_prompt_tiles):
    d = x_parts[0].shape[1]
    if len(x_parts) == 1:
        return [pl.BlockSpec((tm, d), lambda i: (i, 0))]
    return [pl.BlockSpec((tm, d), lambda i: (jnp.minimum(i, n_prompt_tiles - 1), 0)),
            pl.BlockSpec((tm, d), lambda i: (jnp.maximum(i - n_prompt_tiles, 0), 0))]


def _inproj_kernel(*refs, n_x, n_prompt_tiles, n_prev):
    x_refs = refs[:n_x]
    g_ref, w_ref, gq_ref, gk_ref, seg_ref = refs[n_x:n_x + 5]
    u_ref, q_ref, kb_ref, vb_ref, kp_ref, vp_ref, ks_ref, vs_ref = refs[n_x + 5 + n_prev:]
    i = pl.program_id(0)
    x = _token_tile(x_refs, i, n_prompt_tiles)
    ms = jnp.mean(x * x, axis=-1, keepdims=True)
    h = (x * lax.rsqrt(ms + EPS) * g_ref[...]).astype(BF16)
    proj = _dot(h, w_ref[...])
    w = u_ref.shape[-1]
    seg = seg_ref[...]

    def head_norm(z, g):
        hi, lo = _split_bf16(z * z)
        ms_ = _dot(hi, seg) + _dot(lo, seg)
        return z * lax.rsqrt(ms_ + EPS) * g

    u_ref[...] = proj[:, :w]
    qn = head_norm(proj[:, w:2 * w], gq_ref[...])
    q_ref[...] = (qn * (HEAD_DIM ** -0.5)).astype(BF16)
    kn = head_norm(proj[:, 2 * w:3 * w], gk_ref[...])
    kb_ref[...] = kn.astype(BF16)
    vv = proj[:, 3 * w:]
    vb_ref[...] = vv.astype(BF16)

    def emit(k_out, v_out):
        tm = kn.shape[0]
        nh = k_out.shape[0] // tm
        for hd in range(nh):
            k_out[pl.ds(hd, tm, stride=nh), :] = kn[:, hd * HEAD_W:(hd + 1) * HEAD_W]
            v_out[pl.ds(hd, tm, stride=nh), :] = vv[:, hd * HEAD_W:(hd + 1) * HEAD_W]

    @pl.when(i < n_prompt_tiles)
    def _():
        emit(kp_ref, vp_ref)

    @pl.when(i >= n_prompt_tiles)
    def _():
        emit(ks_ref, vs_ref)


def _inproj(x_parts, g_mix, w_in_bf, g_q, g_k, layer, depth, n_prompt_tiles, prev):
    t = sum(p.shape[0] for p in x_parts)
    d = x_parts[0].shape[1]
    aw = w_in_bf.shape[1] // 4
    nh = aw // HEAD_W
    nrep = aw // HEAD_DIM
    gq = jnp.tile(g_q.astype(F32), nrep)[None]
    gk = jnp.tile(g_k.astype(F32), nrep)[None]
    ids = jnp.arange(aw) // HEAD_DIM
    seg = jnp.where(ids[:, None] == ids[None, :], 1.0 / HEAD_DIM, 0.0).astype(BF16)
    tm = TOK_TILE
    n_tiles = t // tm
    n_sample_tiles = n_tiles - n_prompt_tiles
    row = lambda i: (i, 0)
    fixed = lambda i: (0, 0)
    p_blk = pl.BlockSpec((None, tm * nh, HEAD_W),
                         lambda i: (layer * n_prompt_tiles + jnp.minimum(i, n_prompt_tiles - 1), 0, 0))
    s_blk = pl.BlockSpec((None, tm * nh, HEAD_W),
                         lambda i: (layer * n_sample_tiles + jnp.maximum(i - n_prompt_tiles, 0), 0, 0))
    p_shape = jax.ShapeDtypeStruct((depth * n_prompt_tiles, tm * nh, HEAD_W), F32)
    s_shape = jax.ShapeDtypeStruct((depth * n_sample_tiles, tm * nh, HEAD_W), F32)
    outs = [jax.ShapeDtypeStruct((t, aw), dt) for dt in (F32, BF16, BF16, BF16)] + [p_shape, p_shape, s_shape, s_shape]
    prev = () if prev is None else tuple(prev)
    n_x = len(x_parts)
    n_in = n_x + 5
    kern = functools.partial(_inproj_kernel, n_x=n_x, n_prompt_tiles=n_prompt_tiles, n_prev=len(prev))
    return pl.pallas_call(
        kern,
        grid=(n_tiles,),
        in_specs=_token_specs(x_parts, tm, n_prompt_tiles)
                 + [pl.BlockSpec((1, d), fixed),
                    pl.BlockSpec(w_in_bf.shape, fixed), pl.BlockSpec((1, aw), fixed),
                    pl.BlockSpec((1, aw), fixed), pl.BlockSpec((aw, aw), fixed)]
                 + [pl.BlockSpec(memory_space=pl.ANY)] * len(prev),
        out_specs=[pl.BlockSpec((tm, aw), row)] * 4 + [p_blk, p_blk, s_blk, s_blk],
        out_shape=outs,
        input_output_aliases={n_in + j: 4 + j for j in range(len(prev))},
        compiler_params=_cparams(("arbitrary",)),
        name="inproj",
    )(*x_parts, g_mix[None].astype(F32), w_in_bf, gq, gk, seg, *prev)


def _ssm_kernel(u_ref, wb_ref, wc_ref, ab_ref, a64_ref, pw_ref, d_ref, h0_ref,
                y_ref, fin_ref, uperm_ref, st_ref, carry_ref, *, n_prompt_sc, sc_per_seq):
    sc = pl.program_id(0)
    qb = pl.program_id(1)
    nsteps = CHUNK
    w = SSM_QW

    for t in range(nsteps):
        uperm_ref[t * 8:(t + 1) * 8, :] = u_ref[pl.ds(t, 8, stride=nsteps), :]
    up = uperm_ref[...]
    st_ref[...] = _dot(up.astype(BF16), wb_ref[0])

    ab = ab_ref[0]
    ar = jnp.broadcast_to(ab[:, :w], (8, w))
    ai = jnp.broadcast_to(ab[:, w:], (8, w))

    def scan_step(t, carry):
        hr, hi = carry
        r0 = pl.multiple_of(t * 8, 8)
        br = st_ref[pl.ds(r0, 8), :w]
        bi = st_ref[pl.ds(r0, 8), w:]
        nr = ar * hr - ai * hi + br
        ni = ar * hi + ai * hr + bi
        st_ref[pl.ds(r0, 8), :w] = nr
        st_ref[pl.ds(r0, 8), w:] = ni
        return nr, ni

    zero = jnp.zeros((8, w), F32)
    er, ei = lax.fori_loop(0, nsteps, scan_step, (zero, zero), unroll=4)

    a64 = a64_ref[0]
    a64r = a64[:, :w]
    a64i = a64[:, w:]
    is_sample = sc >= n_prompt_sc

    @pl.when(jnp.logical_or(is_sample, sc % sc_per_seq == 0))
    def _():
        carry_ref[qb] = jnp.zeros(carry_ref.shape[1:], F32)

    cin = carry_ref[qb]
    cr = cin[:, :w]
    ci = cin[:, w:]
    rows = lax.broadcasted_iota(jnp.int32, (8, w), 0)
    sr = jnp.zeros((8, w), F32)
    si = jnp.zeros((8, w), F32)
    for j in range(8):
        sr = jnp.where(rows == j, cr, sr)
        si = jnp.where(rows == j, ci, si)
        ejr = er[j:j + 1]
        eji = ei[j:j + 1]
        cr, ci = a64r * cr - a64i * ci + ejr, a64r * ci + a64i * cr + eji
    carry_ref[qb] = jnp.concatenate([cr, ci], axis=1)
    h0 = h0_ref[0, 0]
    given = (jnp.zeros((8, w), jnp.int32) + is_sample.astype(jnp.int32)) > 0
    sr = jnp.where(given, h0[:, :w], sr)
    si = jnp.where(given, h0[:, w:], si)
    fr = a64r * sr - a64i * si + er
    fi = a64r * si + a64i * sr + ei
    fin_ref[0, 0] = jnp.concatenate([fr, fi], axis=1)

    def fix_step(t, _):
        r0 = pl.multiple_of(t * 8, 8)
        p = pw_ref[0, pl.ds(t, 1), :]
        pr = p[:, :w]
        pi = p[:, w:]
        st_ref[pl.ds(r0, 8), :w] = st_ref[pl.ds(r0, 8), :w] + (pr * sr - pi * si)
        st_ref[pl.ds(r0, 8), w:] = st_ref[pl.ds(r0, 8), w:] + (pr * si + pi * sr)
        return 0

    lax.fori_loop(0, nsteps, fix_step, 0, unroll=4)

    y = _dot(st_ref[...].astype(BF16), wc_ref[0]) + up * d_ref[0]
    for t in range(nsteps):
        y_ref[pl.ds(t, 8, stride=nsteps), :] = y[t * 8:(t + 1) * 8, :]


def _ssm_tables(a_re, a_im, log_dt, b_re, b_im, c_re, c_im, d_skip):
    g, n = a_re.shape
    c = b_re.shape[-1]
    nq = g // SSM_QG
    dt = jnp.exp(log_dt)[:, None]
    za_re, za_im = dt * a_re, dt * a_im
    mag = jnp.exp(za_re)
    ab_re, ab_im = mag * jnp.cos(za_im), mag * jnp.sin(za_im)
    den = a_re * a_re + a_im * a_im
    n_re, n_im = ab_re - 1.0, ab_im
    f_re = (n_re * a_re + n_im * a_im) / den
    f_im = (n_im * a_re - n_re * a_im) / den
    bb_re = f_re[..., None] * b_re - f_im[..., None] * b_im
    bb_im = f_re[..., None] * b_im + f_im[..., None] * b_re
    eye = jnp.eye(SSM_QG, dtype=F32)

    def in_w(bb):
        bq = bb.reshape(nq, SSM_QG, n, c)
        return jnp.einsum('qgnc,gh->qgchn', bq, eye).reshape(nq, SSM_QG * c, SSM_QG * n)

    def out_w(cc):
        cq = cc.reshape(nq, SSM_QG, c, n)
        return jnp.einsum('qgcn,gh->qgnhc', cq, eye).reshape(nq, SSM_QG * n, SSM_QG * c)

    wb = jnp.concatenate([in_w(bb_re), in_w(bb_im)], axis=2).astype(BF16)
    wc = jnp.concatenate([out_w(c_re), out_w(-c_im)], axis=1).astype(BF16)

    def lanes(z):
        return jnp.moveaxis(z.reshape(z.shape[:-2] + (nq, SSM_QG * n)), -2, 0)

    def power(k):
        m = jnp.exp(k * za_re)
        return m * jnp.cos(k * za_im), m * jnp.sin(k * za_im)

    ab = jnp.concatenate([lanes(ab_re), lanes(ab_im)], axis=-1)[:, None]
    p64 = power(float(CHUNK))
    a64 = jnp.concatenate([lanes(p64[0]), lanes(p64[1])], axis=-1)[:, None]
    ks = jnp.arange(1, CHUNK + 1, dtype=F32)[:, None, None]
    pk = power(ks)
    pw = jnp.concatenate([lanes(pk[0]), lanes(pk[1])], axis=-1)
    dq = d_skip.reshape(nq, 1, SSM_QG * c).astype(F32)
    return wb, wc, ab, a64, pw, dq


def _ssm(u, tables, h0_all, n_prompt_sc, sc_per_seq):
    wb, wc, ab, a64, pw, dq = tables
    t, cw = u.shape
    nq = wb.shape[0]
    n_sc = t // SSM_ROWS
    sw = 2 * SSM_QW
    kern = functools.partial(_ssm_kernel, n_prompt_sc=n_prompt_sc, sc_per_seq=sc_per_seq)
    per_q = lambda s, q: (q, 0, 0)
    return pl.pallas_call(
        kern,
        grid=(n_sc, nq),
        in_specs=[pl.BlockSpec((SSM_ROWS, 128), lambda s, q: (s, q)),
                  pl.BlockSpec((1,) + wb.shape[1:], per_q), pl.BlockSpec((1,) + wc.shape[1:], per_q),
                  pl.BlockSpec((1, 1, sw), per_q), pl.BlockSpec((1, 1, sw), per_q),
                  pl.BlockSpec((1, CHUNK, sw), per_q), pl.BlockSpec((1, 1, 128), per_q),
                  pl.BlockSpec((1, 1, 8, sw), lambda s, q: (s, q, 0, 0))],
        out_specs=[pl.BlockSpec((SSM_ROWS, 128), lambda s, q: (s, q)),
                   pl.BlockSpec((1, 1, 8, sw), lambda s, q: (s, q, 0, 0))],
        out_shape=[jax.ShapeDtypeStruct((t, cw), F32),
                   jax.ShapeDtypeStruct((n_sc, nq, 8, sw), F32)],
        scratch_shapes=[pltpu.VMEM((SSM_ROWS, 128), F32), pltpu.VMEM((SSM_ROWS, sw), F32),
                        pltpu.VMEM((nq, 1, sw), F32)],
        compiler_params=_cparams(("arbitrary", "arbitrary")),
        name="ssm",
    )(u, wb, wc, ab, a64, pw, dq, h0_all)


def _stack_q(q):
    lane = lax.broadcasted_iota(jnp.int32, q.shape, 1)
    zero = jnp.zeros_like(q)
    return jnp.concatenate([jnp.where(lane < HEAD_DIM, q, zero), jnp.where(lane >= HEAD_DIM, q, zero)], axis=0)


def _online_update(s, v, m_ref, l_ref, acc_ref, fixed_max=False):
    if fixed_max:
        acc_ref[...] = acc_ref[...] + _dot(jnp.exp(s).astype(BF16), v)
        return
    m_old = m_ref[...]
    m_new = jnp.maximum(m_old, jnp.max(s, axis=-1, keepdims=True))
    alpha = jnp.exp(m_old - m_new)
    p = jnp.exp(s - m_new)
    l_ref[...] = alpha * l_ref[...] + jnp.sum(p, axis=-1, keepdims=True)
    acc_ref[...] = alpha * acc_ref[...] + _dot(p.astype(BF16), v)
    m_ref[...] = m_new


SCORE_BOUND_SCALE = 1.02 * HEAD_DIM ** 0.5
FIXED_MAX_LIMIT = 30.0


def _finish_head(m_ref, l_ref, acc_ref, lam, g, out_scale, tq):
    acc = acc_ref[...]
    if acc.shape[1] > HEAD_W:
        l = acc[:, HEAD_W:HEAD_W + 1]
        acc = acc[:, :HEAD_W]
    else:
        l = l_ref[...]
    o = acc[:tq] / l[:tq] - lam * (acc[tq:] / l[tq:])
    ms = jnp.mean(o * o, axis=-1, keepdims=True)
    return o * lax.rsqrt(ms + EPS) * g * out_scale


def _attn_prompt_kernel(slope_ref, lam_ref, bound_ref, q_ref, k_ref, v_ref, bias_ref, g_ref, o_ref,
                        m_ref, l_ref, acc_ref, *, out_scale, fixed_max):
    h = pl.program_id(1)
    i = pl.program_id(2)
    tq = q_ref.shape[0]
    tk = tq
    hk = tk // 2
    slope = slope_ref[h]
    shift = bound_ref[0] if fixed_max else 0.0
    qq = _stack_q(q_ref[...])
    m_ref[...] = jnp.full(m_ref.shape, NEG_INF, F32)
    l_ref[...] = jnp.zeros(l_ref.shape, F32)
    acc_ref[...] = jnp.zeros(acc_ref.shape, F32)
    lane = lax.broadcasted_iota(jnp.int32, (hk, HEAD_W), 1)
    ones_col = jnp.where(lane == 0, 1.0, 0.0).astype(BF16)

    def key_tile(j):
        which = jnp.where(j == i, 1, 0)
        offset = slope * ((i - j) * tq).astype(F32) + shift
        for half in range(2):
            k0 = pl.multiple_of(j * tk + half * hk, hk)
            tile = bias_ref[which, :, half * hk:(half + 1) * hk] - offset
            s = _dot_t(qq, k_ref[pl.ds(k0, hk), :]) + jnp.concatenate([tile, tile], axis=0)
            v = v_ref[pl.ds(k0, hk), :]
            if fixed_max:
                v = jnp.concatenate([v, ones_col], axis=1)
            _online_update(s, v, m_ref, l_ref, acc_ref, fixed_max)

    def two_tiles(jj, _):
        key_tile(2 * jj)
        key_tile(2 * jj + 1)
        return 0

    n_tiles = i + 1
    lax.fori_loop(0, n_tiles >> 1, two_tiles, 0)

    @pl.when((n_tiles & 1) == 1)
    def _():
        key_tile(i)

    o_ref[...] = _finish_head(m_ref, l_ref, acc_ref, lam_ref[0], g_ref[...], out_scale, tq).astype(o_ref.dtype)


def _attn_prompt(q, kb, vb, slopes, lam, bound, g_subln, out_scale, batch, seq):
    n_heads = q.shape[1] // HEAD_W
    tq = ATT_TQ
    nq = seq // tq
    smem = pl.BlockSpec(memory_space=pltpu.SMEM)
    pos = jnp.arange(tq, dtype=jnp.int32)
    rel = (pos[:, None] - pos[None, :]).astype(F32)
    visible = (pos[None, :] // CHUNK) <= (pos[:, None] // CHUNK)
    sl = slopes[:, None, None]
    bias = jnp.stack([-sl * rel[None], jnp.where(visible[None], -sl * jnp.abs(rel)[None], NEG_INF)], axis=1)

    def call(fixed_max):
        return pl.pallas_call(
            functools.partial(_attn_prompt_kernel, out_scale=out_scale, fixed_max=fixed_max),
            grid=(batch, n_heads, nq),
            in_specs=[smem, smem, smem,
                      pl.BlockSpec((tq, HEAD_W), lambda b, h, i: (b * nq + i, h)),
                      pl.BlockSpec((seq, HEAD_W), lambda b, h, i: (b, h)),
                      pl.BlockSpec((seq, HEAD_W), lambda b, h, i: (b, h)),
                      pl.BlockSpec((None, 2, tq, tq), lambda b, h, i: (h, 0, 0, 0)),
                      pl.BlockSpec((1, HEAD_W), lambda b, h, i: (0, 0))],
            out_specs=pl.BlockSpec((tq, HEAD_W), lambda b, h, i: (b * nq + i, h)),
            out_shape=jax.ShapeDtypeStruct(q.shape, BF16),
            scratch_shapes=[pltpu.VMEM((2 * tq, 1), F32), pltpu.VMEM((2 * tq, 1), F32),
                            pltpu.VMEM((2 * tq, 2 * HEAD_W if fixed_max else HEAD_W), F32)],
            compiler_params=_cparams(("parallel", "parallel", "arbitrary")),
            name="attn_prompt_fixed" if fixed_max else "attn_prompt",
        )(slopes, lam, bound, q, kb, vb, bias, g_subln[None].astype(F32))

    return lax.cond(bound[0] <= FIXED_MAX_LIMIT, lambda: call(True), lambda: call(False))


def _attn_sample_kernel(slope_ref, lam_ref, q_ref, kn_ref, vn_ref, kc_ref, vc_ref, g_ref, shared_ref, o_ref,
                        m_ref, l_ref, acc_ref, *, out_scale, past_len):
    del shared_ref
    j = pl.program_id(1)
    nj = pl.num_programs(1)
    tq = q_ref.shape[0]
    n_heads = q_ref.shape[1] // HEAD_W
    tk = kc_ref.shape[0] // n_heads

    @pl.when(j == 0)
    def _():
        m_ref[...] = jnp.full(m_ref.shape, NEG_INF, F32)
        l_ref[...] = jnp.zeros(l_ref.shape, F32)
        acc_ref[...] = jnp.zeros(acc_ref.shape, F32)

    rq = lax.broadcasted_iota(jnp.int32, (2 * tq, 1), 0)
    rq = jnp.where(rq >= tq, rq - tq, rq)
    qpos = (past_len + rq).astype(F32)
    kpos = (j * tk + lax.broadcasted_iota(jnp.int32, (1, tk), 1)).astype(F32)
    for h in range(n_heads):
        cols = slice(h * HEAD_W, (h + 1) * HEAD_W)
        qq = _stack_q(q_ref[:, cols])
        k = kc_ref[pl.ds(h, tk, stride=n_heads), :].astype(BF16)
        v = vc_ref[pl.ds(h, tk, stride=n_heads), :].astype(BF16)
        slope = slope_ref[h]
        s = (_dot_t(qq, k) + slope * kpos) - slope * qpos
        _online_update(s, v, m_ref.at[h], l_ref.at[h], acc_ref.at[h])

    @pl.when(j == nj - 1)
    def _():
        r = lax.broadcasted_iota(jnp.int32, (2 * tq, tq), 0)
        r = jnp.where(r >= tq, r - tq, r)
        c = lax.broadcasted_iota(jnp.int32, (2 * tq, tq), 1)
        dist = jnp.abs(r - c).astype(F32)
        for h in range(n_heads):
            cols = slice(h * HEAD_W, (h + 1) * HEAD_W)
            qq = _stack_q(q_ref[:, cols])
            s = _dot_t(qq, kn_ref[:, cols]) - slope_ref[h] * dist
            _online_update(s, vn_ref[:, cols], m_ref.at[h], l_ref.at[h], acc_ref.at[h])
            o_ref[:, cols] = _finish_head(m_ref.at[h], l_ref.at[h], acc_ref.at[h], lam_ref[0],
                                          g_ref[...], out_scale, tq).astype(o_ref.dtype)


def _attn_sample(q, kb, vb, cache_k, cache_v, layer, slopes, lam, g_subln, out_scale, row0, o_shared):
    depth, n_streams, past_len, n_heads, _ = cache_k.shape
    aw = n_heads * HEAD_W
    tq = CHUNK
    tk = min(ATT_TKC, past_len)
    blk0 = row0 // tq
    smem = pl.BlockSpec(memory_space=pltpu.SMEM)
    new = pl.BlockSpec((tq, aw), lambda s, j: (blk0 + s, 0))
    past = pl.BlockSpec((None, None, tk * n_heads, HEAD_W), lambda s, j: (layer, s, j, 0))
    ck = cache_k.reshape(depth, n_streams, past_len * n_heads, HEAD_W)
    cv = cache_v.reshape(depth, n_streams, past_len * n_heads, HEAD_W)

    return pl.pallas_call(
        functools.partial(_attn_sample_kernel, out_scale=out_scale, past_len=past_len),
        grid=(n_streams, past_len // tk),
        in_specs=[smem, smem, new, new, new, past, past,
                  pl.BlockSpec((1, HEAD_W), lambda s, j: (0, 0)), pl.BlockSpec(memory_space=pl.ANY)],
        out_specs=pl.BlockSpec((tq, aw), lambda s, j: (blk0 + s, 0)),
        out_shape=jax.ShapeDtypeStruct(o_shared.shape, BF16),
        input_output_aliases={8: 0},
        scratch_shapes=[pltpu.VMEM((n_heads, 2 * tq, 1), F32), pltpu.VMEM((n_heads, 2 * tq, 1), F32),
                        pltpu.VMEM((n_heads, 2 * tq, HEAD_W), F32)],
        compiler_params=_cparams(("parallel", "arbitrary")),
        name="attn_sample",
    )(slopes, lam, q, kb, vb, ck, cv, g_subln[None].astype(F32), o_shared)


def _outproj_kernel(ys_ref, oa_ref, *refs, n_x, n_prompt_tiles):
    x_refs = refs[:n_x]
    wglu_ref, gs_ref, wtop_ref, wbot_ref, gf_ref, wrh_ref, wrl_ref, br_ref, x1_ref, h2_ref, rt_ref = refs[n_x:]
    y = ys_ref[...]
    y = 0.5 * y * (1.0 + jnp.tanh(math.sqrt(2.0 / math.pi) * (y + 0.044715 * (y * y * y))))
    z = _dot(y.astype(BF16), wglu_ref[...])
    y = y * (1.0 / (1.0 + jnp.exp(-z)))
    ms = jnp.mean(y * y, axis=-1, keepdims=True)
    y = y * lax.rsqrt(ms + EPS) * gs_ref[...]
    x = _token_tile(x_refs, pl.program_id(0), n_prompt_tiles)
    x1 = x + _dot(y.astype(BF16), wtop_ref[...]) + _dot(oa_ref[...], wbot_ref[...])
    x1_ref[...] = x1
    ms = jnp.mean(x1 * x1, axis=-1, keepdims=True)
    h2 = x1 * lax.rsqrt(ms + EPS) * gf_ref[...]
    tm, d = h2.shape
    nblk = d // 128
    for j in range(nblk):
        h2_ref[pl.ds(j, tm, stride=nblk), :] = h2[:, j * 128:(j + 1) * 128]
    hi, lo = _split_bf16(h2)
    wrh = wrh_ref[...]
    lg = _dot(hi, wrh) + _dot(lo, wrh) + _dot(hi, wrl_ref[...]) + br_ref[...]
    lane = lax.broadcasted_iota(jnp.int32, lg.shape, 1).astype(F32)
    vals, idxs = [], []
    for _ in range(TOP_K):
        m = jnp.max(lg, axis=-1, keepdims=True)
        idx = jnp.min(jnp.where(lg == m, lane, float(lg.shape[1])), axis=-1, keepdims=True)
        vals.append(m)
        idxs.append(idx)
        lg = jnp.where(lane == idx, -3.0e38, lg)
    ex = [jnp.exp(v - vals[0]) for v in vals]
    inv = 1.0 / functools.reduce(lambda a, b: a + b, ex)
    out = jnp.zeros(lg.shape, F32)
    for k in range(TOP_K):
        out = jnp.where(lane == float(k), ex[k] * inv, out)
        out = jnp.where(lane == float(TOP_K + k), idxs[k], out)
    rt_ref[...] = out


def _outproj(ys, oa, x_parts, n_prompt_tiles, w_glu, g_ssm, w_out, g_ffn, w_router, b_router):
    t, sw = ys.shape
    d = x_parts[0].shape[1]
    n_exp = w_router.shape[1]
    lw = max(128, n_exp)
    wr = jnp.zeros((d, lw), F32).at[:, :n_exp].set(w_router)
    wrh, wrl = _split_bf16(wr)
    br = jnp.full((1, lw), NEG_INF, F32).at[0, :n_exp].set(b_router)
    w_out_bf = w_out.astype(BF16)
    tm = TOK_TILE
    row = lambda i: (i, 0)
    fixed = lambda i: (0, 0)
    return pl.pallas_call(
        functools.partial(_outproj_kernel, n_x=len(x_parts), n_prompt_tiles=n_prompt_tiles),
        grid=(t // tm,),
        in_specs=[pl.BlockSpec((tm, sw), row), pl.BlockSpec((tm, d - sw), row)]
                 + _token_specs(x_parts, tm, n_prompt_tiles)
                 + [pl.BlockSpec((sw, sw), fixed), pl.BlockSpec((1, sw), fixed),
                    pl.BlockSpec((sw, d), fixed), pl.BlockSpec((d - sw, d), fixed), pl.BlockSpec((1, d), fixed),
                    pl.BlockSpec((d, lw), fixed), pl.BlockSpec((d, lw), fixed), pl.BlockSpec((1, lw), fixed)],
        out_specs=[pl.BlockSpec((tm, d), row), pl.BlockSpec((tm * (d // 128), 128), row),
                   pl.BlockSpec((tm, lw), row)],
        out_shape=[jax.ShapeDtypeStruct((t, d), F32), jax.ShapeDtypeStruct((t * (d // 128), 128), F32),
                   jax.ShapeDtypeStruct((t, lw), F32)],
        compiler_params=_cparams(("parallel",)),
        name="outproj",
    )(ys, oa, *x_parts, w_glu.astype(BF16), g_ssm[None].astype(F32), w_out_bf[:sw], w_out_bf[sw:],
      g_ffn[None].astype(F32), wrh, wrl, br)


def _moe_kernel(te_ref, nt_ref, src_ref, nxt_ref, dprev_ref, h2_hbm, wgu_ref, bgu_ref, wd_ref, bd_ref,
                out_hbm, xbuf0, xbuf1, obuf0, obuf1, wgu_bf, wd_bf, gsem, ssem):
    i = pl.program_id(0)
    nt = nt_ref[0]
    parity = i % 2
    xbufs = (xbuf0, xbuf1)
    obufs = (obuf0, obuf1)
    tm = src_ref.shape[2]
    nblk = xbuf0.shape[0] // tm

    def token_rows(ref, first_row):
        return ref.at[pl.ds(pl.multiple_of(first_row, nblk), nblk), :]

    def row_copies(one, inline):
        if inline:
            for r in range(tm):
                one(r, r % 2)
            return

        def two(rr, _):
            for p in range(2):
                one(2 * rr + p, p)
            return 0
        lax.fori_loop(0, tm // 2, two, 0, unroll=4)

    def start_gather(idx_ref, s, inline=False):
        def one(r, p):
            pltpu.make_async_copy(token_rows(h2_hbm, idx_ref[0, 0, r]), token_rows(xbufs[s], r * nblk),
                                  gsem.at[s]).start(priority=p)
        row_copies(one, inline)

    def wait_gather(s):
        pltpu.make_async_copy(xbufs[s], xbufs[s], gsem.at[s]).wait()

    def start_scatter(idx_ref, s, inline=False):
        def one(r, p):
            pltpu.make_async_copy(token_rows(obufs[s], r * nblk), token_rows(out_hbm, idx_ref[0, 0, r]),
                                  ssem.at[0]).start(priority=p)
        row_copies(one, inline)

    def wait_scatter(s):
        pltpu.make_async_copy(obufs[s], obufs[s], ssem.at[0]).wait()

    @pl.when(i == 0)
    def _():
        obuf1[...] = jnp.zeros(obuf1.shape, obuf1.dtype)
        start_gather(src_ref, 0)

    prev = te_ref[jnp.maximum(i - 1, 0)]
    new_expert = jnp.logical_or(i == 0, te_ref[i] != prev)

    @pl.when(jnp.logical_and(i < nt, new_expert))
    def _():
        wgu_bf[...] = wgu_ref[0].astype(BF16)
        wd_bf[...] = wd_ref[0].astype(BF16)

    def tile_step(s):
        o = 1 - s
        wait_gather(s)
        start_gather(nxt_ref, o, inline=True)
        start_scatter(dprev_ref, o, inline=True)
        dff = wd_bf.shape[0]
        xin = xbufs[s]
        x = jnp.concatenate([xin[pl.ds(j, tm, stride=nblk), :].astype(BF16) for j in range(nblk)], axis=1)
        gu = _dot(x, wgu_bf[...]) + bgu_ref[0]
        x_glu = jnp.minimum(gu[:, :dff], SWIGLU_LIMIT)
        x_lin = jnp.clip(gu[:, dff:], -SWIGLU_LIMIT, SWIGLU_LIMIT)
        hdn = x_glu * (1.0 / (1.0 + jnp.exp(-SWIGLU_ALPHA * x_glu))) * (x_lin + 1.0)
        out = _dot(hdn.astype(BF16), wd_bf[...]) + bd_ref[0]
        for j in range(nblk):
            obufs[s][pl.ds(j, tm, stride=nblk), :] = out[:, j * 128:(j + 1) * 128]
        wait_scatter(o)

    def drain(s):
        wait_gather(s)
        start_scatter(dprev_ref, 1 - s)
        wait_scatter(1 - s)

    for s in range(2):
        pl.when(jnp.logical_and(i < nt, parity == s))(functools.partial(tile_step, s))
        pl.when(jnp.logical_and(i == nt, parity == s))(functools.partial(drain, s))


def _moe_rows(h2, route, layer, w_gate_up, b_gate_up, w_down, b_down):
    tile_expert, n_used, src, dst = route
    depth, n_exp, d, dgu = w_gate_up.shape
    nblk = d // 128
    t = h2.shape[0] // nblk
    dff = w_down.shape[2]
    tm = MOE_TILE
    n_tiles = tile_expert.shape[0]
    by_e = lambda i, te, nt: (layer * n_exp + te[i], 0, 0)
    smem = lambda f: pl.BlockSpec((1, 1, tm), f, memory_space=pltpu.SMEM)
    any_ = pl.BlockSpec(memory_space=pl.ANY)
    return pl.pallas_call(
        _moe_kernel,
        grid_spec=pltpu.PrefetchScalarGridSpec(
            num_scalar_prefetch=2,
            grid=(n_tiles,),
            in_specs=[smem(lambda i, te, nt: (i, 0, 0)), smem(lambda i, te, nt: (i + 1, 0, 0)),
                      smem(lambda i, te, nt: (i, 0, 0)), any_,
                      pl.BlockSpec((1, d, dgu), by_e), pl.BlockSpec((1, 1, dgu), by_e),
                      pl.BlockSpec((1, dff, d), by_e), pl.BlockSpec((1, 1, d), by_e)],
            out_specs=any_,
            scratch_shapes=[pltpu.VMEM((tm * nblk, 128), F32)] * 4
                           + [pltpu.VMEM((d, dgu), BF16), pltpu.VMEM((dff, d), BF16),
                            pltpu.SemaphoreType.DMA((2,)), pltpu.SemaphoreType.DMA((1,))]),
        out_shape=jax.ShapeDtypeStruct(((TOP_K * t + tm) * nblk, 128), F32),
        compiler_params=_cparams(("arbitrary",)),
        name="moe",
    )(tile_expert, n_used, src, src, dst, h2, w_gate_up.reshape(depth * n_exp, d, dgu),
      b_gate_up.reshape(depth * n_exp, 1, dgu), w_down.reshape(depth * n_exp, dff, d),
      b_down.reshape(depth * n_exp, 1, d))


def _moe_route(top_idx, n_exp, nblk):
    t = top_idx.shape[0]
    tm = MOE_TILE
    n_assign = t * TOP_K
    e_flat = top_idx.T.reshape(-1).astype(jnp.int32)
    order = jnp.argsort(e_flat).astype(jnp.int32)
    experts = jnp.arange(n_exp, dtype=jnp.int32)
    counts = jnp.sum((e_flat[:, None] == experts[None, :]).astype(jnp.int32), axis=0)
    starts = jnp.cumsum(counts) - counts
    tiles_e = (counts + tm - 1) // tm
    tile_end = jnp.cumsum(tiles_e)
    tile_beg = tile_end - tiles_e
    n_used = tile_end[-1]
    n_tiles = n_assign // tm + n_exp + 1
    tile = jnp.arange(n_tiles, dtype=jnp.int32)
    tile_c = jnp.minimum(tile, n_used - 1)
    te = jnp.sum((tile_end[None, :] <= tile_c[:, None]).astype(jnp.int32), axis=1)
    te = jnp.minimum(te, n_exp - 1)
    first = starts[te] + (tile - tile_beg[te]) * tm
    n_valid = jnp.where(tile < n_used, jnp.clip(counts[te] - (tile - tile_beg[te]) * tm, 0, tm), 0)
    r = jnp.arange(tm, dtype=jnp.int32)
    valid = r[None, :] < n_valid[:, None]
    a = order[jnp.clip(first[:, None] + r[None, :], 0, n_assign - 1)]
    src = jnp.where(valid, a % t, 0)
    dst = jnp.where(valid, a, n_assign + r[None, :])
    spare = jnp.broadcast_to(n_assign + r[None, :], (1, tm))
    src = jnp.concatenate([src, jnp.zeros((1, tm), jnp.int32)], axis=0)[:, None, :] * nblk
    dst = jnp.concatenate([spare, dst], axis=0)[:, None, :] * nblk
    return te.astype(jnp.int32), n_used.astype(jnp.int32)[None], src, dst


def _combine_kernel(x1_ref, g_ref, o0_ref, o1_ref, o2_ref, o3_ref, *out_refs, n_prompt_tiles):
    g = g_ref[...]
    tm, d = x1_ref.shape
    nblk = d // 128

    def emit(x2_ref):
        for j in range(nblk):
            cols = slice(j * 128, (j + 1) * 128)
            acc = x1_ref[:, cols]
            for k, o_ref in enumerate((o0_ref, o1_ref, o2_ref, o3_ref)):
                acc = acc + g[:, k:k + 1] * o_ref[pl.ds(j, tm, stride=nblk), :]
            x2_ref[:, cols] = acc

    if len(out_refs) == 1:
        emit(out_refs[0])
        return
    i = pl.program_id(0)

    @pl.when(i < n_prompt_tiles)
    def _():
        emit(out_refs[0])

    @pl.when(i >= n_prompt_tiles)
    def _():
        emit(out_refs[1])


def _combine(x1, gates, out_rows, n_prompt_tiles, split):
    t, d = x1.shape
    tm = TOK_TILE
    nt = t // tm
    rows = lambda k: pl.BlockSpec((tm * (d // 128), 128), lambda i: (k * nt + i, 0))
    if split:
        out_specs = [pl.BlockSpec((tm, d), lambda i: (jnp.minimum(i, n_prompt_tiles - 1), 0)),
                     pl.BlockSpec((tm, d), lambda i: (jnp.maximum(i - n_prompt_tiles, 0), 0))]
        out_shape = [jax.ShapeDtypeStruct((n_prompt_tiles * tm, d), F32),
                     jax.ShapeDtypeStruct(((nt - n_prompt_tiles) * tm, d), F32)]
    else:
        out_specs = pl.BlockSpec((tm, d), lambda i: (i, 0))
        out_shape = jax.ShapeDtypeStruct((t, d), F32)
    return pl.pallas_call(
        functools.partial(_combine_kernel, n_prompt_tiles=n_prompt_tiles),
        grid=(nt,),
        in_specs=[pl.BlockSpec((tm, d), lambda i: (i, 0)), pl.BlockSpec((tm, TOP_K), lambda i: (i, 0)),
                  rows(0), rows(1), rows(2), rows(3)],
        out_specs=out_specs,
        out_shape=out_shape,
        compiler_params=_cparams(("arbitrary",) if split else ("parallel",)),
        name="combine",
    )(x1, gates, out_rows, out_rows, out_rows, out_rows)


def _lambda_init(layer):
    return 0.8 - 0.6 * math.exp(-0.3 * layer)


def kernel(x_prompt, x_sample, cache_k, cache_v, state_ssm_re, state_ssm_im, g_mix, w_in, ssm_a_re, ssm_a_im, ssm_log_dt, ssm_b_re, ssm_b_im, ssm_c_re, ssm_c_im, ssm_d, w_glu, g_ssm_out, g_q, g_k, lambda_q1, lambda_k1, lambda_q2, lambda_k2, g_subln, w_out, g_ffn, w_router, b_router, w_gate_up, b_gate_up, w_down, b_down):
    batch, seq, d = x_prompt.shape
    n_streams, dec_seq, _ = x_sample.shape
    depth = w_in.shape[0]
    past_len = cache_k.shape[2]
    n_heads = cache_k.shape[3]
    aw = n_heads * HEAD_W
    n_groups, n_state = ssm_a_re.shape[1:]
    nq = n_groups // SSM_QG
    assert dec_seq == CHUNK and seq % SSM_ROWS == 0 and n_streams % 8 == 0 and n_state == SSM_STATE
    tp = batch * seq
    ts = n_streams * dec_seq
    n_prompt_sc = tp // SSM_ROWS
    sc_per_seq = seq // SSM_ROWS
    n_sample_sc = ts // SSM_ROWS
    slopes = jnp.asarray([2.0 ** (-8.0 * (h + 1) / n_heads) for h in range(n_heads)], F32)

    x_parts = (x_prompt.reshape(tp, d), x_sample.reshape(ts, d))
    outs = {name: [] for name in ("srp", "sip", "srs", "sis")}
    kv_out = None
    n_exp = w_router.shape[2]
    n_prompt_tiles = tp // TOK_TILE
    for l in range(depth):
        u, q, kb, vb, *kv_out = _inproj(x_parts, g_mix[l], w_in[l].astype(BF16), g_q[l], g_k[l], l, depth,
                                        n_prompt_tiles, kv_out)

        tables = _ssm_tables(ssm_a_re[l], ssm_a_im[l], ssm_log_dt[l], ssm_b_re[l], ssm_b_im[l],
                             ssm_c_re[l], ssm_c_im[l], ssm_d[l])

        def state_lanes(z):
            return z.reshape(n_sample_sc, 8, nq, SSM_QW).transpose(0, 2, 1, 3)

        h0_s = jnp.concatenate([state_lanes(state_ssm_re[l]), state_lanes(state_ssm_im[l])], axis=-1)
        h0_all = jnp.concatenate([jnp.zeros((n_prompt_sc,) + h0_s.shape[1:], F32), h0_s], axis=0)
        ys, fin = _ssm(u, tables, h0_all, n_prompt_sc, sc_per_seq)

        lam_init = _lambda_init(l)
        lam = (jnp.exp(jnp.sum(lambda_q1[l].astype(F32) * lambda_k1[l].astype(F32)))
               - jnp.exp(jnp.sum(lambda_q2[l].astype(F32) * lambda_k2[l].astype(F32))) + lam_init)[None]
        out_scale = 1.0 - lam_init
        bound = (SCORE_BOUND_SCALE * jnp.max(jnp.abs(g_q[l].astype(F32)))
                 * jnp.max(jnp.abs(g_k[l].astype(F32))))[None]
        oa = _attn_prompt(q, kb, vb, slopes, lam, bound, g_subln[l], out_scale, batch, seq)
        oa = _attn_sample(q, kb, vb, cache_k, cache_v, l, slopes, lam, g_subln[l], out_scale, tp, oa)

        x1, h2, routing = _outproj(ys, oa, x_parts, n_prompt_tiles, w_glu[l], g_ssm_out[l], w_out[l], g_ffn[l],
                                   w_router[l], b_router[l])
        gates = routing[:, :TOP_K]
        route = _moe_route(routing[:, TOP_K:2 * TOP_K].astype(jnp.int32), n_exp, d // 128)
        x_new = _combine(x1, gates, _moe_rows(h2, route, l, w_gate_up, b_gate_up, w_down, b_down),
                         n_prompt_tiles, split=(l == depth - 1))
        x_parts = tuple(x_new) if l == depth - 1 else (x_new,)

        fin_p = fin[:n_prompt_sc].reshape(batch, sc_per_seq, nq, 8, 2, SSM_QW)[:, -1, :, -1]
        outs["srp"].append(fin_p[:, :, 0].reshape(batch, n_groups, n_state))
        outs["sip"].append(fin_p[:, :, 1].reshape(batch, n_groups, n_state))
        fin_s = fin[n_prompt_sc:].reshape(n_sample_sc, nq, 8, 2, SSM_QW).transpose(0, 2, 3, 1, 4)
        outs["srs"].append(fin_s[:, :, 0].reshape(n_streams, n_groups, n_state))
        outs["sis"].append(fin_s[:, :, 1].reshape(n_streams, n_groups, n_state))

    st = {name: jnp.stack(vals) for name, vals in outs.items()}
    kp, vp, ks, vs = kv_out
    p_shape = (depth, batch, seq, n_heads, HEAD_W)
    s_shape = (depth, n_streams, dec_seq, n_heads, HEAD_W)
    y_prompt, y_sample = x_parts
    return (y_prompt.reshape(batch, seq, d), y_sample.reshape(n_streams, dec_seq, d),
            kp.reshape(p_shape), vp.reshape(p_shape), st["srp"], st["sip"],
            ks.reshape(s_shape), vs.reshape(s_shape), st["srs"], st["sis"])
```

```python
import functools
import math

import jax
import jax.numpy as jnp
from jax import lax
from jax.experimental import pallas as pl
from jax.experimental.pallas import tpu as pltpu

F32 = jnp.float32
BF16 = jnp.bfloat16

CHUNK = 64
HEAD_DIM = 64
HEAD_W = 2 * HEAD_DIM
SSM_GROUP = 16
SSM_STATE = 64
SSM_QG = 8
SSM_QW = SSM_QG * SSM_STATE
TOP_K = 4
SWIGLU_ALPHA = 1.702
SWIGLU_LIMIT = 7.0
EPS = 1e-6
NEG_INF = -1e30

TOK_TILE = 512
SSM_ROWS = 8 * CHUNK
ATT_TQ = 512
ATT_TKC = 2048
MOE_TILE = 256
VMEM_LIMIT = 56 * 1024 * 1024


def _cparams(sem):
    return pltpu.CompilerParams(dimension_semantics=sem, vmem_limit_bytes=VMEM_LIMIT)


def _dot(a, b):
    return jnp.dot(a, b, preferred_element_type=F32)


def _dot_t(a, b):
    return lax.dot_general(a, b, (((1,), (1,)), ((), ())), preferred_element_type=F32)


def _split_bf16(x):
    hi = x.astype(BF16)
    lo = (x - hi.astype(F32)).astype(BF16)
    return hi, lo


def _token_tile(x_refs, i, n_prompt_tiles):
    if len(x_refs) == 1:
        return x_refs[0][...]
    is_prompt = (jnp.zeros(x_refs[0].shape, jnp.int32) + (i < n_prompt_tiles).astype(jnp.int32)) > 0
    return jnp.where(is_prompt, x_refs[0][...], x_refs[1][...])


def _token_specs(x_parts, tm, n_prompt_tiles):
    d = x_parts[0].shape[1]
    if len(x_parts) == 1:
        return [pl.BlockSpec((tm, d), lambda i: (i, 0))]
    return [pl.BlockSpec((tm, d), lambda i: (jnp.minimum(i, n_prompt_tiles - 1), 0)),
            pl.BlockSpec((tm, d), lambda i: (jnp.maximum(i - n_prompt_tiles, 0), 0))]


def _inproj_kernel(*refs, n_x, n_prompt_tiles, n_prev):
    x_refs = refs[:n_x]
    g_ref, w_ref, gq_ref, gk_ref, seg_ref = refs[n_x:n_x + 5]
    u_ref, q_ref, kb_ref, vb_ref, kp_ref, vp_ref, ks_ref, vs_ref = refs[n_x + 5 + n_prev:]
    i = pl.program_id(0)
    x = _token_tile(x_refs, i, n_prompt_tiles)
    ms = jnp.mean(x * x, axis=-1, keepdims=True)
    h = (x * lax.rsqrt(ms + EPS) * g_ref[...]).astype(BF16)
    proj = _dot(h, w_ref[...])
    w = u_ref.shape[-1]
    seg = seg_ref[...]

    def head_norm(z, g):
        hi, lo = _split_bf16(z * z)
        ms_ = _dot(hi, seg) + _dot(lo, seg)
        return z * lax.rsqrt(ms_ + EPS) * g

    u_ref[...] = proj[:, :w]
    qn = head_norm(proj[:, w:2 * w], gq_ref[...])
    q_ref[...] = (qn * (HEAD_DIM ** -0.5)).astype(BF16)
    kn = head_norm(proj[:, 2 * w:3 * w], gk_ref[...])
    kb_ref[...] = kn.astype(BF16)
    vv = proj[:, 3 * w:]
    vb_ref[...] = vv.astype(BF16)

    def emit(k_out, v_out):
        tm = kn.shape[0]
        nh = k_out.shape[0] // tm
        for hd in range(nh):
            k_out[pl.ds(hd, tm, stride=nh), :] = kn[:, hd * HEAD_W:(hd + 1) * HEAD_W]
            v_out[pl.ds(hd, tm, stride=nh), :] = vv[:, hd * HEAD_W:(hd + 1) * HEAD_W]

    @pl.when(i < n_prompt_tiles)
    def _():
        emit(kp_ref, vp_ref)

    @pl.when(i >= n_prompt_tiles)
    def _():
        emit(ks_ref, vs_ref)


def _inproj(x_parts, g_mix, w_in_bf, g_q, g_k, layer, depth, n_prompt_tiles, prev):
    t = sum(p.shape[0] for p in x_parts)
    d = x_parts[0].shape[1]
    aw = w_in_bf.shape[1] // 4
    nh = aw // HEAD_W
    nrep = aw // HEAD_DIM
    gq = jnp.tile(g_q.astype(F32), nrep)[None]
    gk = jnp.tile(g_k.astype(F32), nrep)[None]
    ids = jnp.arange(aw) // HEAD_DIM
    seg = jnp.where(ids[:, None] == ids[None, :], 1.0 / HEAD_DIM, 0.0).astype(BF16)
    tm = TOK_TILE
    n_tiles = t // tm
    n_sample_tiles = n_tiles - n_prompt_tiles
    row = lambda i: (i, 0)
    fixed = lambda i: (0, 0)
    p_blk = pl.BlockSpec((None, tm * nh, HEAD_W),
                         lambda i: (layer * n_prompt_tiles + jnp.minimum(i, n_prompt_tiles - 1), 0, 0))
    s_blk = pl.BlockSpec((None, tm * nh, HEAD_W),
                         lambda i: (layer * n_sample_tiles + jnp.maximum(i - n_prompt_tiles, 0), 0, 0))
    p_shape = jax.ShapeDtypeStruct((depth * n_prompt_tiles, tm * nh, HEAD_W), F32)
    s_shape = jax.ShapeDtypeStruct((depth * n_sample_tiles, tm * nh, HEAD_W), F32)
    outs = [jax.ShapeDtypeStruct((t, aw), dt) for dt in (F32, BF16, BF16, BF16)] + [p_shape, p_shape, s_shape, s_shape]
    prev = () if prev is None else tuple(prev)
    n_x = len(x_parts)
    n_in = n_x + 5
    kern = functools.partial(_inproj_kernel, n_x=n_x, n_prompt_tiles=n_prompt_tiles, n_prev=len(prev))
    return pl.pallas_call(
        kern,
        grid=(n_tiles,),
        in_specs=_token_specs(x_parts, tm, n_prompt_tiles)
                 + [pl.BlockSpec((1, d), fixed),
                    pl.BlockSpec(w_in_bf.shape, fixed), pl.BlockSpec((1, aw), fixed),
                    pl.BlockSpec((1, aw), fixed), pl.BlockSpec((aw, aw), fixed)]
                 + [pl.BlockSpec(memory_space=pl.ANY)] * len(prev),
        out_specs=[pl.BlockSpec((tm, aw), row)] * 4 + [p_blk, p_blk, s_blk, s_blk],
        out_shape=outs,
        input_output_aliases={n_in + j: 4 + j for j in range(len(prev))},
        compiler_params=_cparams(("arbitrary",)),
        name="inproj",
    )(*x_parts, g_mix[None].astype(F32), w_in_bf, gq, gk, seg, *prev)


def _ssm_kernel(u_ref, wb_ref, wc_ref, ab_ref, a64_ref, pw_ref, d_ref, h0_ref,
                y_ref, fin_ref, uperm_ref, st_ref, carry_ref, *, n_prompt_sc, sc_per_seq):
    sc = pl.program_id(0)
    qb = pl.program_id(1)
    nsteps = CHUNK
    w = SSM_QW

    for t in range(nsteps):
        uperm_ref[t * 8:(t + 1) * 8, :] = u_ref[pl.ds(t, 8, stride=nsteps), :]
    up = uperm_ref[...]
    st_ref[...] = _dot(up.astype(BF16), wb_ref[0])

    ab = ab_ref[0]
    ar = jnp.broadcast_to(ab[:, :w], (8, w))
    ai = jnp.broadcast_to(ab[:, w:], (8, w))

    def scan_step(t, carry):
        hr, hi = carry
        r0 = pl.multiple_of(t * 8, 8)
        br = st_ref[pl.ds(r0, 8), :w]
        bi = st_ref[pl.ds(r0, 8), w:]
        nr = ar * hr - ai * hi + br
        ni = ar * hi + ai * hr + bi
        st_ref[pl.ds(r0, 8), :w] = nr
        st_ref[pl.ds(r0, 8), w:] = ni
        return nr, ni

    zero = jnp.zeros((8, w), F32)
    er, ei = lax.fori_loop(0, nsteps, scan_step, (zero, zero), unroll=4)

    a64 = a64_ref[0]
    a64r = a64[:, :w]
    a64i = a64[:, w:]
    is_sample = sc >= n_prompt_sc

    @pl.when(jnp.logical_or(is_sample, sc % sc_per_seq == 0))
    def _():
        carry_ref[qb] = jnp.zeros(carry_ref.shape[1:], F32)

    cin = carry_ref[qb]
    cr = cin[:, :w]
    ci = cin[:, w:]
    rows = lax.broadcasted_iota(jnp.int32, (8, w), 0)
    sr = jnp.zeros((8, w), F32)
    si = jnp.zeros((8, w), F32)
    for j in range(8):
        sr = jnp.where(rows == j, cr, sr)
        si = jnp.where(rows == j, ci, si)
        ejr = er[j:j + 1]
        eji = ei[j:j + 1]
        cr, ci = a64r * cr - a64i * ci + ejr, a64r * ci + a64i * cr + eji
    carry_ref[qb] = jnp.concatenate([cr, ci], axis=1)
    h0 = h0_ref[0, 0]
    given = (jnp.zeros((8, w), jnp.int32) + is_sample.astype(jnp.int32)) > 0
    sr = jnp.where(given, h0[:, :w], sr)
    si = jnp.where(given, h0[:, w:], si)
    fr = a64r * sr - a64i * si + er
    fi = a64r * si + a64i * sr + ei
    fin_ref[0, 0] = jnp.concatenate([fr, fi], axis=1)

    def fix_step(t, _):
        r0 = pl.multiple_of(t * 8, 8)
        p = pw_ref[0, pl.ds(t, 1), :]
        pr = p[:, :w]
        pi = p[:, w:]
        st_ref[pl.ds(r0, 8), :w] = st_ref[pl.ds(r0, 8), :w] + (pr * sr - pi * si)
        st_ref[pl.ds(r0, 8), w:] = st_ref[pl.ds(r0, 8), w:] + (pr * si + pi * sr)
        return 0

    lax.fori_loop(0, nsteps, fix_step, 0, unroll=4)

    y = _dot(st_ref[...].astype(BF16), wc_ref[0]) + up * d_ref[0]
    for t in range(nsteps):
        y_ref[pl.ds(t, 8, stride=nsteps), :] = y[t * 8:(t + 1) * 8, :]


def _ssm_tables(a_re, a_im, log_dt, b_re, b_im, c_re, c_im, d_skip):
    g, n = a_re.shape
    c = b_re.shape[-1]
    nq = g // SSM_QG
    dt = jnp.exp(log_dt)[:, None]
    za_re, za_im = dt * a_re, dt * a_im
    mag = jnp.exp(za_re)
    ab_re, ab_im = mag * jnp.cos(za_im), mag * jnp.sin(za_im)
    den = a_re * a_re + a_im * a_im
    n_re, n_im = ab_re - 1.0, ab_im
    f_re = (n_re * a_re + n_im * a_im) / den
    f_im = (n_im * a_re - n_re * a_im) / den
    bb_re = f_re[..., None] * b_re - f_im[..., None] * b_im
    bb_im = f_re[..., None] * b_im + f_im[..., None] * b_re
    eye = jnp.eye(SSM_QG, dtype=F32)

    def in_w(bb):
        bq = bb.reshape(nq, SSM_QG, n, c)
        return jnp.einsum('qgnc,gh->qgchn', bq, eye).reshape(nq, SSM_QG * c, SSM_QG * n)

    def out_w(cc):
        cq = cc.reshape(nq, SSM_QG, c, n)
        return jnp.einsum('qgcn,gh->qgnhc', cq, eye).reshape(nq, SSM_QG * n, SSM_QG * c)

    wb = jnp.concatenate([in_w(bb_re), in_w(bb_im)], axis=2).astype(BF16)
    wc = jnp.concatenate([out_w(c_re), out_w(-c_im)], axis=1).astype(BF16)

    def lanes(z):
        return jnp.moveaxis(z.reshape(z.shape[:-2] + (nq, SSM_QG * n)), -2, 0)

    def power(k):
        m = jnp.exp(k * za_re)
        return m * jnp.cos(k * za_im), m * jnp.sin(k * za_im)

    ab = jnp.concatenate([lanes(ab_re), lanes(ab_im)], axis=-1)[:, None]
    p64 = power(float(CHUNK))
    a64 = jnp.concatenate([lanes(p64[0]), lanes(p64[1])], axis=-1)[:, None]
    ks = jnp.arange(1, CHUNK + 1, dtype=F32)[:, None, None]
    pk = power(ks)
    pw = jnp.concatenate([lanes(pk[0]), lanes(pk[1])], axis=-1)
    dq = d_skip.reshape(nq, 1, SSM_QG * c).astype(F32)
    return wb, wc, ab, a64, pw, dq


def _ssm(u, tables, h0_all, n_prompt_sc, sc_per_seq):
    wb, wc, ab, a64, pw, dq = tables
    t, cw = u.shape
    nq = wb.shape[0]
    n_sc = t // SSM_ROWS
    sw = 2 * SSM_QW
    kern = functools.partial(_ssm_kernel, n_prompt_sc=n_prompt_sc, sc_per_seq=sc_per_seq)
    per_q = lambda s, q: (q, 0, 0)
    return pl.pallas_call(
        kern,
        grid=(n_sc, nq),
        in_specs=[pl.BlockSpec((SSM_ROWS, 128), lambda s, q: (s, q)),
                  pl.BlockSpec((1,) + wb.shape[1:], per_q), pl.BlockSpec((1,) + wc.shape[1:], per_q),
                  pl.BlockSpec((1, 1, sw), per_q), pl.BlockSpec((1, 1, sw), per_q),
                  pl.BlockSpec((1, CHUNK, sw), per_q), pl.BlockSpec((1, 1, 128), per_q),
                  pl.BlockSpec((1, 1, 8, sw), lambda s, q: (s, q, 0, 0))],
        out_specs=[pl.BlockSpec((SSM_ROWS, 128), lambda s, q: (s, q)),
                   pl.BlockSpec((1, 1, 8, sw), lambda s, q: (s, q, 0, 0))],
        out_shape=[jax.ShapeDtypeStruct((t, cw), F32),
                   jax.ShapeDtypeStruct((n_sc, nq, 8, sw), F32)],
        scratch_shapes=[pltpu.VMEM((SSM_ROWS, 128), F32), pltpu.VMEM((SSM_ROWS, sw), F32),
                        pltpu.VMEM((nq, 1, sw), F32)],
        compiler_params=_cparams(("arbitrary", "arbitrary")),
        name="ssm",
    )(u, wb, wc, ab, a64, pw, dq, h0_all)


def _stack_q(q):
    lane = lax.broadcasted_iota(jnp.int32, q.shape, 1)
    zero = jnp.zeros_like(q)
    return jnp.concatenate([jnp.where(lane < HEAD_DIM, q, zero), jnp.where(lane >= HEAD_DIM, q, zero)], axis=0)


def _online_update(s, v, m_ref, l_ref, acc_ref, fixed_max=False):
    if fixed_max:
        acc_ref[...] = acc_ref[...] + _dot(jnp.exp(s).astype(BF16), v)
        return
    m_old = m_ref[...]
    m_new = jnp.maximum(m_old, jnp.max(s, axis=-1, keepdims=True))
    alpha = jnp.exp(m_old - m_new)
    p = jnp.exp(s - m_new)
    if acc_ref.shape[1] == HEAD_W:
        l_ref[...] = alpha * l_ref[...] + jnp.sum(p, axis=-1, keepdims=True)
    acc_ref[...] = alpha * acc_ref[...] + _dot(p.astype(BF16), v)
    m_ref[...] = m_new


SCORE_BOUND_SCALE = 1.02 * HEAD_DIM ** 0.5
FIXED_MAX_LIMIT = 30.0


def _finish_head(m_ref, l_ref, acc_ref, lam, g, out_scale, tq):
    acc = acc_ref[...]
    if acc.shape[1] > HEAD_W:
        l = acc[:, HEAD_W:HEAD_W + 1]
        acc = acc[:, :HEAD_W]
    else:
        l = l_ref[...]
    o = acc[:tq] / l[:tq] - lam * (acc[tq:] / l[tq:])
    ms = jnp.mean(o * o, axis=-1, keepdims=True)
    return o * lax.rsqrt(ms + EPS) * g * out_scale


def _attn_prompt_kernel(slope_ref, lam_ref, bound_ref, q_ref, k_ref, v_ref, bias_ref, g_ref, o_ref,
                        m_ref, l_ref, acc_ref, *, out_scale, fixed_max):
    h = pl.program_id(1)
    i = pl.program_id(2)
    tq = q_ref.shape[0]
    tk = tq
    hk = tk // 2
    slope = slope_ref[h]
    shift = bound_ref[0] if fixed_max else 0.0
    qq = _stack_q(q_ref[...])
    m_ref[...] = jnp.full(m_ref.shape, NEG_INF, F32)
    l_ref[...] = jnp.zeros(l_ref.shape, F32)
    acc_ref[...] = jnp.zeros(acc_ref.shape, F32)
    lane = lax.broadcasted_iota(jnp.int32, (hk, HEAD_W), 1)
    ones_col = jnp.where(lane == 0, 1.0, 0.0).astype(BF16)

    def key_tile(j):
        which = jnp.where(j == i, 1, 0)
        offset = slope * ((i - j) * tq).astype(F32) + shift
        for half in range(2):
            k0 = pl.multiple_of(j * tk + half * hk, hk)
            tile = bias_ref[which, :, half * hk:(half + 1) * hk] - offset
            s = _dot_t(qq, k_ref[pl.ds(k0, hk), :]) + jnp.concatenate([tile, tile], axis=0)
            v = v_ref[pl.ds(k0, hk), :]
            if fixed_max:
                v = jnp.concatenate([v, ones_col], axis=1)
            _online_update(s, v, m_ref, l_ref, acc_ref, fixed_max)

    def two_tiles(jj, _):
        key_tile(2 * jj)
        key_tile(2 * jj + 1)
        return 0

    n_tiles = i + 1
    lax.fori_loop(0, n_tiles >> 1, two_tiles, 0)

    @pl.when((n_tiles & 1) == 1)
    def _():
        key_tile(i)

    o_ref[...] = _finish_head(m_ref, l_ref, acc_ref, lam_ref[0], g_ref[...], out_scale, tq).astype(o_ref.dtype)


def _attn_prompt(q, kb, vb, slopes, lam, bound, g_subln, out_scale, batch, seq):
    n_heads = q.shape[1] // HEAD_W
    tq = ATT_TQ
    nq = seq // tq
    smem = pl.BlockSpec(memory_space=pltpu.SMEM)
    pos = jnp.arange(tq, dtype=jnp.int32)
    rel = (pos[:, None] - pos[None, :]).astype(F32)
    visible = (pos[None, :] // CHUNK) <= (pos[:, None] // CHUNK)
    sl = slopes[:, None, None]
    bias = jnp.stack([-sl * rel[None], jnp.where(visible[None], -sl * jnp.abs(rel)[None], NEG_INF)], axis=1)

    def call(fixed_max):
        return pl.pallas_call(
            functools.partial(_attn_prompt_kernel, out_scale=out_scale, fixed_max=fixed_max),
            grid=(batch, n_heads, nq),
            in_specs=[smem, smem, smem,
                      pl.BlockSpec((tq, HEAD_W), lambda b, h, i: (b * nq + i, h)),
                      pl.BlockSpec((seq, HEAD_W), lambda b, h, i: (b, h)),
                      pl.BlockSpec((seq, HEAD_W), lambda b, h, i: (b, h)),
                      pl.BlockSpec((None, 2, tq, tq), lambda b, h, i: (h, 0, 0, 0)),
                      pl.BlockSpec((1, HEAD_W), lambda b, h, i: (0, 0))],
            out_specs=pl.BlockSpec((tq, HEAD_W), lambda b, h, i: (b * nq + i, h)),
            out_shape=jax.ShapeDtypeStruct(q.shape, BF16),
            scratch_shapes=[pltpu.VMEM((2 * tq, 1), F32), pltpu.VMEM((2 * tq, 1), F32),
                            pltpu.VMEM((2 * tq, 2 * HEAD_W if fixed_max else HEAD_W), F32)],
            compiler_params=_cparams(("parallel", "parallel", "arbitrary")),
            name="attn_prompt_fixed" if fixed_max else "attn_prompt",
        )(slopes, lam, bound, q, kb, vb, bias, g_subln[None].astype(F32))

    return lax.cond(bound[0] <= FIXED_MAX_LIMIT, lambda: call(True), lambda: call(False))


def _attn_sample_kernel(slope_ref, lam_ref, q_ref, kn_ref, vn_ref, kc_ref, vc_ref, g_ref, shared_ref, o_ref,
                        m_ref, l_ref, acc_ref, *, out_scale, past_len):
    del shared_ref
    j = pl.program_id(1)
    nj = pl.num_programs(1)
    tq = q_ref.shape[0]
    n_heads = q_ref.shape[1] // HEAD_W
    tk = kc_ref.shape[0] // n_heads

    @pl.when(j == 0)
    def _():
        m_ref[...] = jnp.full(m_ref.shape, NEG_INF, F32)
        l_ref[...] = jnp.zeros(l_ref.shape, F32)
        acc_ref[...] = jnp.zeros(acc_ref.shape, F32)

    def with_ones(v):
        lane = lax.broadcasted_iota(jnp.int32, v.shape, 1)
        return jnp.concatenate([v, jnp.where(lane == 0, 1.0, 0.0).astype(BF16)], axis=1)

    rq = lax.broadcasted_iota(jnp.int32, (2 * tq, 1), 0)
    rq = jnp.where(rq >= tq, rq - tq, rq)
    qpos = (past_len + rq).astype(F32)
    kpos = (j * tk + lax.broadcasted_iota(jnp.int32, (1, tk), 1)).astype(F32)
    for h in range(n_heads):
        cols = slice(h * HEAD_W, (h + 1) * HEAD_W)
        qq = _stack_q(q_ref[:, cols])
        k = kc_ref[pl.ds(h, tk, stride=n_heads), :].astype(BF16)
        v = with_ones(vc_ref[pl.ds(h, tk, stride=n_heads), :].astype(BF16))
        slope = slope_ref[h]
        s = (_dot_t(qq, k) + slope * kpos) - slope * qpos
        _online_update(s, v, m_ref.at[h], l_ref.at[h], acc_ref.at[h])

    @pl.when(j == nj - 1)
    def _():
        r = lax.broadcasted_iota(jnp.int32, (2 * tq, tq), 0)
        r = jnp.where(r >= tq, r - tq, r)
        c = lax.broadcasted_iota(jnp.int32, (2 * tq, tq), 1)
        dist = jnp.abs(r - c).astype(F32)
        for h in range(n_heads):
            cols = slice(h * HEAD_W, (h + 1) * HEAD_W)
            qq = _stack_q(q_ref[:, cols])
            s = _dot_t(qq, kn_ref[:, cols]) - slope_ref[h] * dist
            _online_update(s, with_ones(vn_ref[:, cols]), m_ref.at[h], l_ref.at[h], acc_ref.at[h])
            o_ref[:, cols] = _finish_head(m_ref.at[h], l_ref.at[h], acc_ref.at[h], lam_ref[0],
                                          g_ref[...], out_scale, tq).astype(o_ref.dtype)


def _attn_sample(q, kb, vb, cache_k, cache_v, layer, slopes, lam, g_subln, out_scale, row0, o_shared):
    depth, n_streams, past_len, n_heads, _ = cache_k.shape
    aw = n_heads * HEAD_W
    tq = CHUNK
    tk = min(ATT_TKC, past_len)
    blk0 = row0 // tq
    smem = pl.BlockSpec(memory_space=pltpu.SMEM)
    new = pl.BlockSpec((tq, aw), lambda s, j: (blk0 + s, 0))
    past = pl.BlockSpec((None, None, tk * n_heads, HEAD_W), lambda s, j: (layer, s, j, 0))
    ck = cache_k.reshape(depth, n_streams, past_len * n_heads, HEAD_W)
    cv = cache_v.reshape(depth, n_streams, past_len * n_heads, HEAD_W)

    return pl.pallas_call(
        functools.partial(_attn_sample_kernel, out_scale=out_scale, past_len=past_len),
        grid=(n_streams, past_len // tk),
        in_specs=[smem, smem, new, new, new, past, past,
                  pl.BlockSpec((1, HEAD_W), lambda s, j: (0, 0)), pl.BlockSpec(memory_space=pl.ANY)],
        out_specs=pl.BlockSpec((tq, aw), lambda s, j: (blk0 + s, 0)),
        out_shape=jax.ShapeDtypeStruct(o_shared.shape, BF16),
        input_output_aliases={8: 0},
        scratch_shapes=[pltpu.VMEM((n_heads, 2 * tq, 1), F32), pltpu.VMEM((n_heads, 2 * tq, 1), F32),
                        pltpu.VMEM((n_heads, 2 * tq, 2 * HEAD_W), F32)],
        compiler_params=_cparams(("parallel", "arbitrary")),
        name="attn_sample",
    )(slopes, lam, q, kb, vb, ck, cv, g_subln[None].astype(F32), o_shared)


def _outproj_kernel(ys_ref, oa_ref, *refs, n_x, n_prompt_tiles):
    x_refs = refs[:n_x]
    wglu_ref, gs_ref, wtop_ref, wbot_ref, gf_ref, wrh_ref, wrl_ref, br_ref, x1_ref, h2_ref, rt_ref = refs[n_x:]
    y = ys_ref[...]
    y = 0.5 * y * (1.0 + jnp.tanh(math.sqrt(2.0 / math.pi) * (y + 0.044715 * (y * y * y))))
    z = _dot(y.astype(BF16), wglu_ref[...])
    y = y * (1.0 / (1.0 + jnp.exp(-z)))
    ms = jnp.mean(y * y, axis=-1, keepdims=True)
    y = y * lax.rsqrt(ms + EPS) * gs_ref[...]
    x = _token_tile(x_refs, pl.program_id(0), n_prompt_tiles)
    x1 = x + _dot(y.astype(BF16), wtop_ref[...]) + _dot(oa_ref[...], wbot_ref[...])
    x1_ref[...] = x1
    ms = jnp.mean(x1 * x1, axis=-1, keepdims=True)
    h2 = x1 * lax.rsqrt(ms + EPS) * gf_ref[...]
    tm, d = h2.shape
    nblk = d // 128
    for j in range(nblk):
        h2_ref[pl.ds(j, tm, stride=nblk), :] = h2[:, j * 128:(j + 1) * 128]
    hi, lo = _split_bf16(h2)
    wrh = wrh_ref[...]
    lg = _dot(hi, wrh) + _dot(lo, wrh) + _dot(hi, wrl_ref[...]) + br_ref[...]
    lane = lax.broadcasted_iota(jnp.int32, lg.shape, 1).astype(F32)
    vals, idxs = [], []
    for _ in range(TOP_K):
        m = jnp.max(lg, axis=-1, keepdims=True)
        idx = jnp.min(jnp.where(lg == m, lane, float(lg.shape[1])), axis=-1, keepdims=True)
        vals.append(m)
        idxs.append(idx)
        lg = jnp.where(lane == idx, -3.0e38, lg)
    ex = [jnp.exp(v - vals[0]) for v in vals]
    inv = 1.0 / functools.reduce(lambda a, b: a + b, ex)
    out = jnp.zeros(lg.shape, F32)
    for k in range(TOP_K):
        out = jnp.where(lane == float(k), ex[k] * inv, out)
        out = jnp.where(lane == float(TOP_K + k), idxs[k], out)
    rt_ref[...] = out


def _outproj(ys, oa, x_parts, n_prompt_tiles, w_glu, g_ssm, w_out, g_ffn, w_router, b_router):
    t, sw = ys.shape
    d = x_parts[0].shape[1]
    n_exp = w_router.shape[1]
    lw = max(128, n_exp)
    wr = jnp.zeros((d, lw), F32).at[:, :n_exp].set(w_router)
    wrh, wrl = _split_bf16(wr)
    br = jnp.full((1, lw), NEG_INF, F32).at[0, :n_exp].set(b_router)
    w_out_bf = w_out.astype(BF16)
    tm = TOK_TILE
    row = lambda i: (i, 0)
    fixed = lambda i: (0, 0)
    return pl.pallas_call(
        functools.partial(_outproj_kernel, n_x=len(x_parts), n_prompt_tiles=n_prompt_tiles),
        grid=(t // tm,),
        in_specs=[pl.BlockSpec((tm, sw), row), pl.BlockSpec((tm, d - sw), row)]
                 + _token_specs(x_parts, tm, n_prompt_tiles)
                 + [pl.BlockSpec((sw, sw), fixed), pl.BlockSpec((1, sw), fixed),
                    pl.BlockSpec((sw, d), fixed), pl.BlockSpec((d - sw, d), fixed), pl.BlockSpec((1, d), fixed),
                    pl.BlockSpec((d, lw), fixed), pl.BlockSpec((d, lw), fixed), pl.BlockSpec((1, lw), fixed)],
        out_specs=[pl.BlockSpec((tm, d), row), pl.BlockSpec((tm * (d // 128), 128), row),
                   pl.BlockSpec((tm, lw), row)],
        out_shape=[jax.ShapeDtypeStruct((t, d), F32), jax.ShapeDtypeStruct((t * (d // 128), 128), F32),
                   jax.ShapeDtypeStruct((t, lw), F32)],
        compiler_params=_cparams(("parallel",)),
        name="outproj",
    )(ys, oa, *x_parts, w_glu.astype(BF16), g_ssm[None].astype(F32), w_out_bf[:sw], w_out_bf[sw:],
      g_ffn[None].astype(F32), wrh, wrl, br)


def _moe_kernel(te_ref, nt_ref, src_ref, nxt_ref, dprev_ref, h2_hbm, wgu_ref, bgu_ref, wd_ref, bd_ref,
                out_hbm, xbuf, obuf, wgu_bf, wd_bf, gsem, ssem):
    i = pl.program_id(0)
    nt = nt_ref[0]
    slot = i % 2
    other = 1 - slot
    tm = src_ref.shape[2]
    nblk = xbuf.shape[1] // tm

    def token_rows(ref, first_row):
        return ref.at[pl.ds(pl.multiple_of(first_row, nblk), nblk), :]

    def start_gather(idx_ref, s):
        def two(rr, _):
            for p in range(2):
                r = 2 * rr + p
                pltpu.make_async_copy(token_rows(h2_hbm, idx_ref[0, 0, r]), token_rows(xbuf.at[s], r * nblk),
                                      gsem.at[s]).start(priority=p)
            return 0
        lax.fori_loop(0, tm // 2, two, 0, unroll=4)

    def wait_gather(s):
        pltpu.make_async_copy(xbuf.at[s], xbuf.at[s], gsem.at[s]).wait()

    def start_scatter(idx_ref, s, inline=False):
        def one(r, p):
            pltpu.make_async_copy(token_rows(obuf.at[s], r * nblk), token_rows(out_hbm, idx_ref[0, 0, r]),
                                  ssem.at[0]).start(priority=p)

        if inline:
            for r in range(tm):
                one(r, r % 2)
            return

        def two(rr, _):
            for p in range(2):
                one(2 * rr + p, p)
            return 0
        lax.fori_loop(0, tm // 2, two, 0, unroll=4)

    def wait_scatter(s):
        pltpu.make_async_copy(obuf.at[s], obuf.at[s], ssem.at[0]).wait()

    @pl.when(i == 0)
    def _():
        obuf[...] = jnp.zeros(obuf.shape, obuf.dtype)
        start_gather(src_ref, 0)

    @pl.when(i < nt)
    def _():
        wait_gather(slot)

    prev = te_ref[jnp.maximum(i - 1, 0)]
    new_expert = jnp.logical_or(i == 0, te_ref[i] != prev)

    @pl.when(jnp.logical_and(i < nt, new_expert))
    def _():
        wgu_bf[...] = wgu_ref[0].astype(BF16)
        wd_bf[...] = wd_ref[0].astype(BF16)

    @pl.when(i < nt)
    def _():
        start_gather(nxt_ref, other)
        start_scatter(dprev_ref, other, inline=True)
        dff = wd_bf.shape[0]
        xin = xbuf.at[slot]
        x = jnp.concatenate([xin[pl.ds(j, tm, stride=nblk), :].astype(BF16) for j in range(nblk)], axis=1)
        gu = _dot(x, wgu_bf[...]) + bgu_ref[0]
        x_glu = jnp.minimum(gu[:, :dff], SWIGLU_LIMIT)
        x_lin = jnp.clip(gu[:, dff:], -SWIGLU_LIMIT, SWIGLU_LIMIT)
        hdn = x_glu * (1.0 / (1.0 + jnp.exp(-SWIGLU_ALPHA * x_glu))) * (x_lin + 1.0)
        out = _dot(hdn.astype(BF16), wd_bf[...]) + bd_ref[0]
        res = obuf.at[slot]
        for j in range(nblk):
            res[pl.ds(j, tm, stride=nblk), :] = out[:, j * 128:(j + 1) * 128]
        wait_scatter(other)

    @pl.when(i == nt)
    def _():
        wait_gather(slot)
        start_scatter(dprev_ref, other)
        wait_scatter(other)


def _moe_rows(h2, route, layer, w_gate_up, b_gate_up, w_down, b_down):
    tile_expert, n_used, src, dst = route
    depth, n_exp, d, dgu = w_gate_up.shape
    nblk = d // 128
    t = h2.shape[0] // nblk
    dff = w_down.shape[2]
    tm = MOE_TILE
    n_tiles = tile_expert.shape[0]
    by_e = lambda i, te, nt: (layer * n_exp + te[i], 0, 0)
    smem = lambda f: pl.BlockSpec((1, 1, tm), f, memory_space=pltpu.SMEM)
    any_ = pl.BlockSpec(memory_space=pl.ANY)
    return pl.pallas_call(
        _moe_kernel,
        grid_spec=pltpu.PrefetchScalarGridSpec(
            num_scalar_prefetch=2,
            grid=(n_tiles,),
            in_specs=[smem(lambda i, te, nt: (i, 0, 0)), smem(lambda i, te, nt: (i + 1, 0, 0)),
                      smem(lambda i, te, nt: (i, 0, 0)), any_,
                      pl.BlockSpec((1, d, dgu), by_e), pl.BlockSpec((1, 1, dgu), by_e),
                      pl.BlockSpec((1, dff, d), by_e), pl.BlockSpec((1, 1, d), by_e)],
            out_specs=any_,
            scratch_shapes=[pltpu.VMEM((2, tm * nblk, 128), F32), pltpu.VMEM((2, tm * nblk, 128), F32),
                            pltpu.VMEM((d, dgu), BF16), pltpu.VMEM((dff, d), BF16),
                            pltpu.SemaphoreType.DMA((2,)), pltpu.SemaphoreType.DMA((1,))]),
        out_shape=jax.ShapeDtypeStruct(((TOP_K * t + tm) * nblk, 128), F32),
        compiler_params=_cparams(("arbitrary",)),
        name="moe",
    )(tile_expert, n_used, src, src, dst, h2, w_gate_up.reshape(depth * n_exp, d, dgu),
      b_gate_up.reshape(depth * n_exp, 1, dgu), w_down.reshape(depth * n_exp, dff, d),
      b_down.reshape(depth * n_exp, 1, d))


def _moe_route(top_idx, n_exp, nblk):
    t = top_idx.shape[0]
    tm = MOE_TILE
    n_assign = t * TOP_K
    e_flat = top_idx.T.reshape(-1).astype(jnp.int32)
    order = jnp.argsort(e_flat).astype(jnp.int32)
    experts = jnp.arange(n_exp, dtype=jnp.int32)
    counts = jnp.sum((e_flat[:, None] == experts[None, :]).astype(jnp.int32), axis=0)
    starts = jnp.cumsum(counts) - counts
    tiles_e = (counts + tm - 1) // tm
    tile_end = jnp.cumsum(tiles_e)
    tile_beg = tile_end - tiles_e
    n_used = tile_end[-1]
    n_tiles = n_assign // tm + n_exp + 1
    tile = jnp.arange(n_tiles, dtype=jnp.int32)
    tile_c = jnp.minimum(tile, n_used - 1)
    te = jnp.sum((tile_end[None, :] <= tile_c[:, None]).astype(jnp.int32), axis=1)
    te = jnp.minimum(te, n_exp - 1)
    first = starts[te] + (tile - tile_beg[te]) * tm
    n_valid = jnp.where(tile < n_used, jnp.clip(counts[te] - (tile - tile_beg[te]) * tm, 0, tm), 0)
    r = jnp.arange(tm, dtype=jnp.int32)
    valid = r[None, :] < n_valid[:, None]
    a = order[jnp.clip(first[:, None] + r[None, :], 0, n_assign - 1)]
    src = jnp.where(valid, a % t, 0)
    dst = jnp.where(valid, a, n_assign + r[None, :])
    spare = jnp.broadcast_to(n_assign + r[None, :], (1, tm))
    src = jnp.concatenate([src, jnp.zeros((1, tm), jnp.int32)], axis=0)[:, None, :] * nblk
    dst = jnp.concatenate([spare, dst], axis=0)[:, None, :] * nblk
    return te.astype(jnp.int32), n_used.astype(jnp.int32)[None], src, dst


def _combine_kernel(x1_ref, g_ref, o0_ref, o1_ref, o2_ref, o3_ref, *out_refs, n_prompt_tiles):
    g = g_ref[...]
    tm, d = x1_ref.shape
    nblk = d // 128

    def emit(x2_ref):
        for j in range(nblk):
            cols = slice(j * 128, (j + 1) * 128)
            acc = x1_ref[:, cols]
            for k, o_ref in enumerate((o0_ref, o1_ref, o2_ref, o3_ref)):
                acc = acc + g[:, k:k + 1] * o_ref[pl.ds(j, tm, stride=nblk), :]
            x2_ref[:, cols] = acc

    if len(out_refs) == 1:
        emit(out_refs[0])
        return
    i = pl.program_id(0)

    @pl.when(i < n_prompt_tiles)
    def _():
        emit(out_refs[0])

    @pl.when(i >= n_prompt_tiles)
    def _():
        emit(out_refs[1])


def _combine(x1, gates, out_rows, n_prompt_tiles, split):
    t, d = x1.shape
    tm = TOK_TILE
    nt = t // tm
    rows = lambda k: pl.BlockSpec((tm * (d // 128), 128), lambda i: (k * nt + i, 0))
    if split:
        out_specs = [pl.BlockSpec((tm, d), lambda i: (jnp.minimum(i, n_prompt_tiles - 1), 0)),
                     pl.BlockSpec((tm, d), lambda i: (jnp.maximum(i - n_prompt_tiles, 0), 0))]
        out_shape = [jax.ShapeDtypeStruct((n_prompt_tiles * tm, d), F32),
                     jax.ShapeDtypeStruct(((nt - n_prompt_tiles) * tm, d), F32)]
    else:
        out_specs = pl.BlockSpec((tm, d), lambda i: (i, 0))
        out_shape = jax.ShapeDtypeStruct((t, d), F32)
    return pl.pallas_call(
        functools.partial(_combine_kernel, n_prompt_tiles=n_prompt_tiles),
        grid=(nt,),
        in_specs=[pl.BlockSpec((tm, d), lambda i: (i, 0)), pl.BlockSpec((tm, TOP_K), lambda i: (i, 0)),
                  rows(0), rows(1), rows(2), rows(3)],
        out_specs=out_specs,
        out_shape=out_shape,
        compiler_params=_cparams(("arbitrary",) if split else ("parallel",)),
        name="combine",
    )(x1, gates, out_rows, out_rows, out_rows, out_rows)


def _lambda_init(layer):
    return 0.8 - 0.6 * math.exp(-0.3 * layer)


def kernel(x_prompt, x_sample, cache_k, cache_v, state_ssm_re, state_ssm_im, g_mix, w_in, ssm_a_re, ssm_a_im, ssm_log_dt, ssm_b_re, ssm_b_im, ssm_c_re, ssm_c_im, ssm_d, w_glu, g_ssm_out, g_q, g_k, lambda_q1, lambda_k1, lambda_q2, lambda_k2, g_subln, w_out, g_ffn, w_router, b_router, w_gate_up, b_gate_up, w_down, b_down):
    batch, seq, d = x_prompt.shape
    n_streams, dec_seq, _ = x_sample.shape
    depth = w_in.shape[0]
    past_len = cache_k.shape[2]
    n_heads = cache_k.shape[3]
    aw = n_heads * HEAD_W
    n_groups, n_state = ssm_a_re.shape[1:]
    nq = n_groups // SSM_QG
    assert dec_seq == CHUNK and seq % SSM_ROWS == 0 and n_streams % 8 == 0 and n_state == SSM_STATE
    tp = batch * seq
    ts = n_streams * dec_seq
    n_prompt_sc = tp // SSM_ROWS
    sc_per_seq = seq // SSM_ROWS
    n_sample_sc = ts // SSM_ROWS
    slopes = jnp.asarray([2.0 ** (-8.0 * (h + 1) / n_heads) for h in range(n_heads)], F32)

    x_parts = (x_prompt.reshape(tp, d), x_sample.reshape(ts, d))
    outs = {name: [] for name in ("srp", "sip", "srs", "sis")}
    kv_out = None
    n_exp = w_router.shape[2]
    n_prompt_tiles = tp // TOK_TILE
    for l in range(depth):
        u, q, kb, vb, *kv_out = _inproj(x_parts, g_mix[l], w_in[l].astype(BF16), g_q[l], g_k[l], l, depth,
                                        n_prompt_tiles, kv_out)

        tables = _ssm_tables(ssm_a_re[l], ssm_a_im[l], ssm_log_dt[l], ssm_b_re[l], ssm_b_im[l],
                             ssm_c_re[l], ssm_c_im[l], ssm_d[l])

        def state_lanes(z):
            return z.reshape(n_sample_sc, 8, nq, SSM_QW).transpose(0, 2, 1, 3)

        h0_s = jnp.concatenate([state_lanes(state_ssm_re[l]), state_lanes(state_ssm_im[l])], axis=-1)
        h0_all = jnp.concatenate([jnp.zeros((n_prompt_sc,) + h0_s.shape[1:], F32), h0_s], axis=0)
        ys, fin = _ssm(u, tables, h0_all, n_prompt_sc, sc_per_seq)

        lam_init = _lambda_init(l)
        lam = (jnp.exp(jnp.sum(lambda_q1[l].astype(F32) * lambda_k1[l].astype(F32)))
               - jnp.exp(jnp.sum(lambda_q2[l].astype(F32) * lambda_k2[l].astype(F32))) + lam_init)[None]
        out_scale = 1.0 - lam_init
        bound = (SCORE_BOUND_SCALE * jnp.max(jnp.abs(g_q[l].astype(F32)))
                 * jnp.max(jnp.abs(g_k[l].astype(F32))))[None]
        oa = _attn_prompt(q, kb, vb, slopes, lam, bound, g_subln[l], out_scale, batch, seq)
        oa = _attn_sample(q, kb, vb, cache_k, cache_v, l, slopes, lam, g_subln[l], out_scale, tp, oa)

        x1, h2, routing = _outproj(ys, oa, x_parts, n_prompt_tiles, w_glu[l], g_ssm_out[l], w_out[l], g_ffn[l],
                                   w_router[l], b_router[l])
        gates = routing[:, :TOP_K]
        route = _moe_route(routing[:, TOP_K:2 * TOP_K].astype(jnp.int32), n_exp, d // 128)
        x_new = _combine(x1, gates, _moe_rows(h2, route, l, w_gate_up, b_gate_up, w_down, b_down),
                         n_prompt_tiles, split=(l == depth - 1))
        x_parts = tuple(x_new) if l == depth - 1 else (x_new,)

        fin_p = fin[:n_prompt_sc].reshape(batch, sc_per_seq, nq, 8, 2, SSM_QW)[:, -1, :, -1]
        outs["srp"].append(fin_p[:, :, 0].reshape(batch, n_groups, n_state))
        outs["sip"].append(fin_p[:, :, 1].reshape(batch, n_groups, n_state))
        fin_s = fin[n_prompt_sc:].reshape(n_sample_sc, nq, 8, 2, SSM_QW).transpose(0, 2, 3, 1, 4)
        outs["srs"].append(fin_s[:, :, 0].reshape(n_streams, n_groups, n_state))
        outs["sis"].append(fin_s[:, :, 1].reshape(n_streams, n_groups, n_state))

    st = {name: jnp.stack(vals) for name, vals in outs.items()}
    kp, vp, ks, vs = kv_out
    p_shape = (depth, batch, seq, n_heads, HEAD_W)
    s_shape = (depth, n_streams, dec_seq, n_heads, HEAD_W)
    y_prompt, y_sample = x_parts
    return (y_prompt.reshape(batch, seq, d), y_sample.reshape(n_streams, dec_seq, d),
            kp.reshape(p_shape), vp.reshape(p_shape), st["srp"], st["sip"],
            ks.reshape(s_shape), vs.reshape(s_shape), st["srs"], st["sis"])
```

```python
import functools
import math

import jax
import jax.numpy as jnp
from jax import lax
from jax.experimental import pallas as pl
from jax.experimental.pallas import tpu as pltpu

F32 = jnp.float32
BF16 = jnp.bfloat16

CHUNK = 64
HEAD_DIM = 64
HEAD_W = 2 * HEAD_DIM
SSM_GROUP = 16
SSM_STATE = 64
SSM_QG = 8
SSM_QW = SSM_QG * SSM_STATE
TOP_K = 4
SWIGLU_ALPHA = 1.702
SWIGLU_LIMIT = 7.0
EPS = 1e-6
NEG_INF = -1e30

TOK_TILE = 512
SSM_ROWS = 8 * CHUNK
ATT_TQ = 512
ATT_TKC = 2048
MOE_TILE = 256
VMEM_LIMIT = 56 * 1024 * 1024


def _cparams(sem):
    return pltpu.CompilerParams(dimension_semantics=sem, vmem_limit_bytes=VMEM_LIMIT)


def _dot(a, b):
    return jnp.dot(a, b, preferred_element_type=F32)


def _dot_t(a, b):
    return lax.dot_general(a, b, (((1,), (1,)), ((), ())), preferred_element_type=F32)


def _split_bf16(x):
    hi = x.astype(BF16)
    lo = (x - hi.astype(F32)).astype(BF16)
    return hi, lo


def _token_tile(x_refs, i, n_prompt_tiles):
    if len(x_refs) == 1:
        return x_refs[0][...]
    is_prompt = (jnp.zeros(x_refs[0].shape, jnp.int32) + (i < n_prompt_tiles).astype(jnp.int32)) > 0
    return jnp.where(is_prompt, x_refs[0][...], x_refs[1][...])


def _token_specs(x_parts, tm, n_prompt_tiles):
    d = x_parts[0].shape[1]
    if len(x_parts) == 1:
        return [pl.BlockSpec((tm, d), lambda i: (i, 0))]
    return [pl.BlockSpec((tm, d), lambda i: (jnp.minimum(i, n_prompt_tiles - 1), 0)),
            pl.BlockSpec((tm, d), lambda i: (jnp.maximum(i - n_prompt_tiles, 0), 0))]


def _inproj_kernel(*refs, n_x, n_prompt_tiles, n_prev):
    x_refs = refs[:n_x]
    g_ref, w_ref, gq_ref, gk_ref, seg_ref = refs[n_x:n_x + 5]
    u_ref, q_ref, kb_ref, vb_ref, kp_ref, vp_ref, ks_ref, vs_ref = refs[n_x + 5 + n_prev:]
    i = pl.program_id(0)
    x = _token_tile(x_refs, i, n_prompt_tiles)
    ms = jnp.mean(x * x, axis=-1, keepdims=True)
    h = (x * lax.rsqrt(ms + EPS) * g_ref[...]).astype(BF16)
    proj = _dot(h, w_ref[...])
    w = u_ref.shape[-1]
    seg = seg_ref[...]

    def head_norm(z, g):
        hi, lo = _split_bf16(z * z)
        ms_ = _dot(hi, seg) + _dot(lo, seg)
        return z * lax.rsqrt(ms_ + EPS) * g

    u_ref[...] = proj[:, :w]
    qn = head_norm(proj[:, w:2 * w], gq_ref[...])
    q_ref[...] = (qn * (HEAD_DIM ** -0.5)).astype(BF16)
    kn = head_norm(proj[:, 2 * w:3 * w], gk_ref[...])
    kb_ref[...] = kn.astype(BF16)
    vv = proj[:, 3 * w:]
    vb_ref[...] = vv.astype(BF16)

    def emit(k_out, v_out):
        tm = kn.shape[0]
        nh = k_out.shape[0] // tm
        for hd in range(nh):
            k_out[pl.ds(hd, tm, stride=nh), :] = kn[:, hd * HEAD_W:(hd + 1) * HEAD_W]
            v_out[pl.ds(hd, tm, stride=nh), :] = vv[:, hd * HEAD_W:(hd + 1) * HEAD_W]

    @pl.when(i < n_prompt_tiles)
    def _():
        emit(kp_ref, vp_ref)

    @pl.when(i >= n_prompt_tiles)
    def _():
        emit(ks_ref, vs_ref)


def _inproj(x_parts, g_mix, w_in_bf, g_q, g_k, layer, depth, n_prompt_tiles, prev):
    t = sum(p.shape[0] for p in x_parts)
    d = x_parts[0].shape[1]
    aw = w_in_bf.shape[1] // 4
    nh = aw // HEAD_W
    nrep = aw // HEAD_DIM
    gq = jnp.tile(g_q.astype(F32), nrep)[None]
    gk = jnp.tile(g_k.astype(F32), nrep)[None]
    ids = jnp.arange(aw) // HEAD_DIM
    seg = jnp.where(ids[:, None] == ids[None, :], 1.0 / HEAD_DIM, 0.0).astype(BF16)
    tm = TOK_TILE
    n_tiles = t // tm
    n_sample_tiles = n_tiles - n_prompt_tiles
    row = lambda i: (i, 0)
    fixed = lambda i: (0, 0)
    p_blk = pl.BlockSpec((None, tm * nh, HEAD_W),
                         lambda i: (layer * n_prompt_tiles + jnp.minimum(i, n_prompt_tiles - 1), 0, 0))
    s_blk = pl.BlockSpec((None, tm * nh, HEAD_W),
                         lambda i: (layer * n_sample_tiles + jnp.maximum(i - n_prompt_tiles, 0), 0, 0))
    p_shape = jax.ShapeDtypeStruct((depth * n_prompt_tiles, tm * nh, HEAD_W), F32)
    s_shape = jax.ShapeDtypeStruct((depth * n_sample_tiles, tm * nh, HEAD_W), F32)
    outs = [jax.ShapeDtypeStruct((t, aw), dt) for dt in (F32, BF16, BF16, BF16)] + [p_shape, p_shape, s_shape, s_shape]
    prev = () if prev is None else tuple(prev)
    n_x = len(x_parts)
    n_in = n_x + 5
    kern = functools.partial(_inproj_kernel, n_x=n_x, n_prompt_tiles=n_prompt_tiles, n_prev=len(prev))
    return pl.pallas_call(
        kern,
        grid=(n_tiles,),
        in_specs=_token_specs(x_parts, tm, n_prompt_tiles)
                 + [pl.BlockSpec((1, d), fixed),
                    pl.BlockSpec(w_in_bf.shape, fixed), pl.BlockSpec((1, aw), fixed),
                    pl.BlockSpec((1, aw), fixed), pl.BlockSpec((aw, aw), fixed)]
                 + [pl.BlockSpec(memory_space=pl.ANY)] * len(prev),
        out_specs=[pl.BlockSpec((tm, aw), row)] * 4 + [p_blk, p_blk, s_blk, s_blk],
        out_shape=outs,
        input_output_aliases={n_in + j: 4 + j for j in range(len(prev))},
        compiler_params=_cparams(("arbitrary",)),
        name="inproj",
    )(*x_parts, g_mix[None].astype(F32), w_in_bf, gq, gk, seg, *prev)


def _ssm_kernel(u_ref, wb_ref, wc_ref, ab_ref, a64_ref, pw_ref, d_ref, h0_ref,
                y_ref, fin_ref, uperm_ref, st_ref, carry_ref, *, n_prompt_sc, sc_per_seq):
    sc = pl.program_id(0)
    qb = pl.program_id(1)
    nsteps = CHUNK
    w = SSM_QW

    for t in range(nsteps):
        uperm_ref[t * 8:(t + 1) * 8, :] = u_ref[pl.ds(t, 8, stride=nsteps), :]
    up = uperm_ref[...]
    st_ref[...] = _dot(up.astype(BF16), wb_ref[0])

    ab = ab_ref[0]
    ar = jnp.broadcast_to(ab[:, :w], (8, w))
    ai = jnp.broadcast_to(ab[:, w:], (8, w))

    def scan_step(t, carry):
        hr, hi = carry
        r0 = pl.multiple_of(t * 8, 8)
        br = st_ref[pl.ds(r0, 8), :w]
        bi = st_ref[pl.ds(r0, 8), w:]
        nr = ar * hr - ai * hi + br
        ni = ar * hi + ai * hr + bi
        st_ref[pl.ds(r0, 8), :w] = nr
        st_ref[pl.ds(r0, 8), w:] = ni
        return nr, ni

    zero = jnp.zeros((8, w), F32)
    er, ei = lax.fori_loop(0, nsteps, scan_step, (zero, zero), unroll=4)

    a64 = a64_ref[0]
    a64r = a64[:, :w]
    a64i = a64[:, w:]
    is_sample = sc >= n_prompt_sc

    @pl.when(jnp.logical_or(is_sample, sc % sc_per_seq == 0))
    def _():
        carry_ref[qb] = jnp.zeros(carry_ref.shape[1:], F32)

    cin = carry_ref[qb]
    cr = cin[:, :w]
    ci = cin[:, w:]
    rows = lax.broadcasted_iota(jnp.int32, (8, w), 0)
    sr = jnp.zeros((8, w), F32)
    si = jnp.zeros((8, w), F32)
    for j in range(8):
        sr = jnp.where(rows == j, cr, sr)
        si = jnp.where(rows == j, ci, si)
        ejr = er[j:j + 1]
        eji = ei[j:j + 1]
        cr, ci = a64r * cr - a64i * ci + ejr, a64r * ci + a64i * cr + eji
    carry_ref[qb] = jnp.concatenate([cr, ci], axis=1)
    h0 = h0_ref[0, 0]
    given = (jnp.zeros((8, w), jnp.int32) + is_sample.astype(jnp.int32)) > 0
    sr = jnp.where(given, h0[:, :w], sr)
    si = jnp.where(given, h0[:, w:], si)
    fr = a64r * sr - a64i * si + er
    fi = a64r * si + a64i * sr + ei
    fin_ref[0, 0] = jnp.concatenate([fr, fi], axis=1)

    def fix_step(t, _):
        r0 = pl.multiple_of(t * 8, 8)
        p = pw_ref[0, pl.ds(t, 1), :]
        pr = p[:, :w]
        pi = p[:, w:]
        st_ref[pl.ds(r0, 8), :w] = st_ref[pl.ds(r0, 8), :w] + (pr * sr - pi * si)
        st_ref[pl.ds(r0, 8), w:] = st_ref[pl.ds(r0, 8), w:] + (pr * si + pi * sr)
        return 0

    lax.fori_loop(0, nsteps, fix_step, 0, unroll=4)

    y = _dot(st_ref[...].astype(BF16), wc_ref[0]) + up * d_ref[0]
    for t in range(nsteps):
        y_ref[pl.ds(t, 8, stride=nsteps), :] = y[t * 8:(t + 1) * 8, :]


def _ssm_tables(a_re, a_im, log_dt, b_re, b_im, c_re, c_im, d_skip):
    g, n = a_re.shape
    c = b_re.shape[-1]
    nq = g // SSM_QG
    dt = jnp.exp(log_dt)[:, None]
    za_re, za_im = dt * a_re, dt * a_im
    mag = jnp.exp(za_re)
    ab_re, ab_im = mag * jnp.cos(za_im), mag * jnp.sin(za_im)
    den = a_re * a_re + a_im * a_im
    n_re, n_im = ab_re - 1.0, ab_im
    f_re = (n_re * a_re + n_im * a_im) / den
    f_im = (n_im * a_re - n_re * a_im) / den
    bb_re = f_re[..., None] * b_re - f_im[..., None] * b_im
    bb_im = f_re[..., None] * b_im + f_im[..., None] * b_re
    eye = jnp.eye(SSM_QG, dtype=F32)

    def in_w(bb):
        bq = bb.reshape(nq, SSM_QG, n, c)
        return jnp.einsum('qgnc,gh->qgchn', bq, eye).reshape(nq, SSM_QG * c, SSM_QG * n)

    def out_w(cc):
        cq = cc.reshape(nq, SSM_QG, c, n)
        return jnp.einsum('qgcn,gh->qgnhc', cq, eye).reshape(nq, SSM_QG * n, SSM_QG * c)

    wb = jnp.concatenate([in_w(bb_re), in_w(bb_im)], axis=2).astype(BF16)
    wc = jnp.concatenate([out_w(c_re), out_w(-c_im)], axis=1).astype(BF16)

    def lanes(z):
        return jnp.moveaxis(z.reshape(z.shape[:-2] + (nq, SSM_QG * n)), -2, 0)

    def power(k):
        m = jnp.exp(k * za_re)
        return m * jnp.cos(k * za_im), m * jnp.sin(k * za_im)

    ab = jnp.concatenate([lanes(ab_re), lanes(ab_im)], axis=-1)[:, None]
    p64 = power(float(CHUNK))
    a64 = jnp.concatenate([lanes(p64[0]), lanes(p64[1])], axis=-1)[:, None]
    ks = jnp.arange(1, CHUNK + 1, dtype=F32)[:, None, None]
    pk = power(ks)
    pw = jnp.concatenate([lanes(pk[0]), lanes(pk[1])], axis=-1)
    dq = d_skip.reshape(nq, 1, SSM_QG * c).astype(F32)
    return wb, wc, ab, a64, pw, dq


def _ssm(u, tables, h0_all, n_prompt_sc, sc_per_seq):
    wb, wc, ab, a64, pw, dq = tables
    t, cw = u.shape
    nq = wb.shape[0]
    n_sc = t // SSM_ROWS
    sw = 2 * SSM_QW
    kern = functools.partial(_ssm_kernel, n_prompt_sc=n_prompt_sc, sc_per_seq=sc_per_seq)
    per_q = lambda s, q: (q, 0, 0)
    return pl.pallas_call(
        kern,
        grid=(n_sc, nq),
        in_specs=[pl.BlockSpec((SSM_ROWS, 128), lambda s, q: (s, q)),
                  pl.BlockSpec((1,) + wb.shape[1:], per_q), pl.BlockSpec((1,) + wc.shape[1:], per_q),
                  pl.BlockSpec((1, 1, sw), per_q), pl.BlockSpec((1, 1, sw), per_q),
                  pl.BlockSpec((1, CHUNK, sw), per_q), pl.BlockSpec((1, 1, 128), per_q),
                  pl.BlockSpec((1, 1, 8, sw), lambda s, q: (s, q, 0, 0))],
        out_specs=[pl.BlockSpec((SSM_ROWS, 128), lambda s, q: (s, q)),
                   pl.BlockSpec((1, 1, 8, sw), lambda s, q: (s, q, 0, 0))],
        out_shape=[jax.ShapeDtypeStruct((t, cw), F32),
                   jax.ShapeDtypeStruct((n_sc, nq, 8, sw), F32)],
        scratch_shapes=[pltpu.VMEM((SSM_ROWS, 128), F32), pltpu.VMEM((SSM_ROWS, sw), F32),
                        pltpu.VMEM((nq, 1, sw), F32)],
        compiler_params=_cparams(("arbitrary", "arbitrary")),
        name="ssm",
    )(u, wb, wc, ab, a64, pw, dq, h0_all)


def _stack_q(q):
    lane = lax.broadcasted_iota(jnp.int32, q.shape, 1)
    zero = jnp.zeros_like(q)
    return jnp.concatenate([jnp.where(lane < HEAD_DIM, q, zero), jnp.where(lane >= HEAD_DIM, q, zero)], axis=0)


def _online_update(s, v, m_ref, l_ref, acc_ref, fixed_max=False):
    if fixed_max:
        acc_ref[...] = acc_ref[...] + _dot(jnp.exp(s).astype(BF16), v)
        return
    m_old = m_ref[...]
    m_new = jnp.maximum(m_old, jnp.max(s, axis=-1, keepdims=True))
    alpha = jnp.exp(m_old - m_new)
    p = jnp.exp(s - m_new)
    if acc_ref.shape[1] == HEAD_W:
        l_ref[...] = alpha * l_ref[...] + jnp.sum(p, axis=-1, keepdims=True)
    acc_ref[...] = alpha * acc_ref[...] + _dot(p.astype(BF16), v)
    m_ref[...] = m_new


SCORE_BOUND_SCALE = 1.02 * HEAD_DIM ** 0.5
FIXED_MAX_LIMIT = 30.0


def _finish_head(m_ref, l_ref, acc_ref, lam, g, out_scale, tq):
    acc = acc_ref[...]
    if acc.shape[1] > HEAD_W:
        l = acc[:, HEAD_W:HEAD_W + 1]
        acc = acc[:, :HEAD_W]
    else:
        l = l_ref[...]
    o = acc[:tq] / l[:tq] - lam * (acc[tq:] / l[tq:])
    ms = jnp.mean(o * o, axis=-1, keepdims=True)
    return o * lax.rsqrt(ms + EPS) * g * out_scale


def _attn_prompt_kernel(slope_ref, lam_ref, bound_ref, q_ref, k_ref, v_ref, bias_ref, g_ref, o_ref,
                        m_ref, l_ref, acc_ref, *, out_scale, fixed_max):
    h = pl.program_id(1)
    i = pl.program_id(2)
    tq = q_ref.shape[0]
    tk = tq
    hk = tk // 2
    slope = slope_ref[h]
    shift = bound_ref[0] if fixed_max else 0.0
    qq = _stack_q(q_ref[...])
    m_ref[...] = jnp.full(m_ref.shape, NEG_INF, F32)
    l_ref[...] = jnp.zeros(l_ref.shape, F32)
    acc_ref[...] = jnp.zeros(acc_ref.shape, F32)
    lane = lax.broadcasted_iota(jnp.int32, (hk, HEAD_W), 1)
    ones_col = jnp.where(lane == 0, 1.0, 0.0).astype(BF16)

    def key_tile(j):
        which = jnp.where(j == i, 1, 0)
        offset = slope * ((i - j) * tq).astype(F32) + shift
        for half in range(2):
            k0 = pl.multiple_of(j * tk + half * hk, hk)
            tile = bias_ref[which, :, half * hk:(half + 1) * hk] - offset
            s = _dot_t(qq, k_ref[pl.ds(k0, hk), :]) + jnp.concatenate([tile, tile], axis=0)
            v = v_ref[pl.ds(k0, hk), :]
            if fixed_max:
                v = jnp.concatenate([v, ones_col], axis=1)
            _online_update(s, v, m_ref, l_ref, acc_ref, fixed_max)

    def two_tiles(jj, _):
        key_tile(2 * jj)
        key_tile(2 * jj + 1)
        return 0

    n_tiles = i + 1
    lax.fori_loop(0, n_tiles >> 1, two_tiles, 0)

    @pl.when((n_tiles & 1) == 1)
    def _():
        key_tile(i)

    o_ref[...] = _finish_head(m_ref, l_ref, acc_ref, lam_ref[0], g_ref[...], out_scale, tq).astype(o_ref.dtype)


def _attn_prompt(q, kb, vb, slopes, lam, bound, g_subln, out_scale, batch, seq):
    n_heads = q.shape[1] // HEAD_W
    tq = ATT_TQ
    nq = seq // tq
    smem = pl.BlockSpec(memory_space=pltpu.SMEM)
    pos = jnp.arange(tq, dtype=jnp.int32)
    rel = (pos[:, None] - pos[None, :]).astype(F32)
    visible = (pos[None, :] // CHUNK) <= (pos[:, None] // CHUNK)
    sl = slopes[:, None, None]
    bias = jnp.stack([-sl * rel[None], jnp.where(visible[None], -sl * jnp.abs(rel)[None], NEG_INF)], axis=1)

    def call(fixed_max):
        return pl.pallas_call(
            functools.partial(_attn_prompt_kernel, out_scale=out_scale, fixed_max=fixed_max),
            grid=(batch, n_heads, nq),
            in_specs=[smem, smem, smem,
                      pl.BlockSpec((tq, HEAD_W), lambda b, h, i: (b * nq + i, h)),
                      pl.BlockSpec((seq, HEAD_W), lambda b, h, i: (b, h)),
                      pl.BlockSpec((seq, HEAD_W), lambda b, h, i: (b, h)),
                      pl.BlockSpec((None, 2, tq, tq), lambda b, h, i: (h, 0, 0, 0)),
                      pl.BlockSpec((1, HEAD_W), lambda b, h, i: (0, 0))],
            out_specs=pl.BlockSpec((tq, HEAD_W), lambda b, h, i: (b * nq + i, h)),
            out_shape=jax.ShapeDtypeStruct(q.shape, BF16),
            scratch_shapes=[pltpu.VMEM((2 * tq, 1), F32), pltpu.VMEM((2 * tq, 1), F32),
                            pltpu.VMEM((2 * tq, 2 * HEAD_W if fixed_max else HEAD_W), F32)],
            compiler_params=_cparams(("parallel", "parallel", "arbitrary")),
            name="attn_prompt_fixed" if fixed_max else "attn_prompt",
        )(slopes, lam, bound, q, kb, vb, bias, g_subln[None].astype(F32))

    return lax.cond(bound[0] <= FIXED_MAX_LIMIT, lambda: call(True), lambda: call(False))


def _attn_sample_kernel(slope_ref, lam_ref, q_ref, kn_ref, vn_ref, kc_ref, vc_ref, g_ref, shared_ref, o_ref,
                        m_ref, l_ref, acc_ref, *, out_scale, past_len):
    del shared_ref
    j = pl.program_id(1)
    nj = pl.num_programs(1)
    tq = q_ref.shape[0]
    n_heads = q_ref.shape[1] // HEAD_W
    tk = kc_ref.shape[0] // n_heads

    @pl.when(j == 0)
    def _():
        m_ref[...] = jnp.full(m_ref.shape, NEG_INF, F32)
        l_ref[...] = jnp.zeros(l_ref.shape, F32)
        acc_ref[...] = jnp.zeros(acc_ref.shape, F32)

    def with_ones(v):
        lane = lax.broadcasted_iota(jnp.int32, v.shape, 1)
        return jnp.concatenate([v, jnp.where(lane == 0, 1.0, 0.0).astype(BF16)], axis=1)

    rq = lax.broadcasted_iota(jnp.int32, (2 * tq, 1), 0)
    rq = jnp.where(rq >= tq, rq - tq, rq)
    qpos = (past_len + rq).astype(F32)
    kpos = (j * tk + lax.broadcasted_iota(jnp.int32, (1, tk), 1)).astype(F32)
    for h in range(n_heads):
        cols = slice(h * HEAD_W, (h + 1) * HEAD_W)
        qq = _stack_q(q_ref[:, cols])
        k = kc_ref[pl.ds(h, tk, stride=n_heads), :].astype(BF16)
        v = with_ones(vc_ref[pl.ds(h, tk, stride=n_heads), :].astype(BF16))
        slope = slope_ref[h]
        s = (_dot_t(qq, k) + slope * kpos) - slope * qpos
        _online_update(s, v, m_ref.at[h], l_ref.at[h], acc_ref.at[h])

    @pl.when(j == nj - 1)
    def _():
        r = lax.broadcasted_iota(jnp.int32, (2 * tq, tq), 0)
        r = jnp.where(r >= tq, r - tq, r)
        c = lax.broadcasted_iota(jnp.int32, (2 * tq, tq), 1)
        dist = jnp.abs(r - c).astype(F32)
        for h in range(n_heads):
            cols = slice(h * HEAD_W, (h + 1) * HEAD_W)
            qq = _stack_q(q_ref[:, cols])
            s = _dot_t(qq, kn_ref[:, cols]) - slope_ref[h] * dist
            _online_update(s, with_ones(vn_ref[:, cols]), m_ref.at[h], l_ref.at[h], acc_ref.at[h])
            o_ref[:, cols] = _finish_head(m_ref.at[h], l_ref.at[h], acc_ref.at[h], lam_ref[0],
                                          g_ref[...], out_scale, tq).astype(o_ref.dtype)


def _attn_sample(q, kb, vb, cache_k, cache_v, layer, slopes, lam, g_subln, out_scale, row0, o_shared):
    depth, n_streams, past_len, n_heads, _ = cache_k.shape
    aw = n_heads * HEAD_W
    tq = CHUNK
    tk = min(ATT_TKC, past_len)
    blk0 = row0 // tq
    smem = pl.BlockSpec(memory_space=pltpu.SMEM)
    new = pl.BlockSpec((tq, aw), lambda s, j: (blk0 + s, 0))
    past = pl.BlockSpec((None, None, tk * n_heads, HEAD_W), lambda s, j: (layer, s, j, 0))
    ck = cache_k.reshape(depth, n_streams, past_len * n_heads, HEAD_W)
    cv = cache_v.reshape(depth, n_streams, past_len * n_heads, HEAD_W)

    return pl.pallas_call(
        functools.partial(_attn_sample_kernel, out_scale=out_scale, past_len=past_len),
        grid=(n_streams, past_len // tk),
        in_specs=[smem, smem, new, new, new, past, past,
                  pl.BlockSpec((1, HEAD_W), lambda s, j: (0, 0)), pl.BlockSpec(memory_space=pl.ANY)],
        out_specs=pl.BlockSpec((tq, aw), lambda s, j: (blk0 + s, 0)),
        out_shape=jax.ShapeDtypeStruct(o_shared.shape, BF16),
        input_output_aliases={8: 0},
        scratch_shapes=[pltpu.VMEM((n_heads, 2 * tq, 1), F32), pltpu.VMEM((n_heads, 2 * tq, 1), F32),
                        pltpu.VMEM((n_heads, 2 * tq, 2 * HEAD_W), F32)],
        compiler_params=_cparams(("parallel", "arbitrary")),
        name="attn_sample",
    )(slopes, lam, q, kb, vb, ck, cv, g_subln[None].astype(F32), o_shared)


def _outproj_kernel(ys_ref, oa_ref, *refs, n_x, n_prompt_tiles):
    x_refs = refs[:n_x]
    wglu_ref, gs_ref, wtop_ref, wbot_ref, gf_ref, wrh_ref, wrl_ref, br_ref, x1_ref, h2_ref, rt_ref = refs[n_x:]
    y = ys_ref[...]
    y = 0.5 * y * (1.0 + jnp.tanh(math.sqrt(2.0 / math.pi) * (y + 0.044715 * (y * y * y))))
    z = _dot(y.astype(BF16), wglu_ref[...])
    y = y * (1.0 / (1.0 + jnp.exp(-z)))
    ms = jnp.mean(y * y, axis=-1, keepdims=True)
    y = y * lax.rsqrt(ms + EPS) * gs_ref[...]
    x = _token_tile(x_refs, pl.program_id(0), n_prompt_tiles)
    x1 = x + _dot(y.astype(BF16), wtop_ref[...]) + _dot(oa_ref[...], wbot_ref[...])
    x1_ref[...] = x1
    ms = jnp.mean(x1 * x1, axis=-1, keepdims=True)
    h2 = x1 * lax.rsqrt(ms + EPS) * gf_ref[...]
    tm, d = h2.shape
    nblk = d // 128
    for j in range(nblk):
        h2_ref[pl.ds(j, tm, stride=nblk), :] = h2[:, j * 128:(j + 1) * 128]
    hi, lo = _split_bf16(h2)
    wrh = wrh_ref[...]
    lg = _dot(hi, wrh) + _dot(lo, wrh) + _dot(hi, wrl_ref[...]) + br_ref[...]
    lane = lax.broadcasted_iota(jnp.int32, lg.shape, 1).astype(F32)
    vals, idxs = [], []
    for _ in range(TOP_K):
        m = jnp.max(lg, axis=-1, keepdims=True)
        idx = jnp.min(jnp.where(lg == m, lane, float(lg.shape[1])), axis=-1, keepdims=True)
        vals.append(m)
        idxs.append(idx)
        lg = jnp.where(lane == idx, -3.0e38, lg)
    ex = [jnp.exp(v - vals[0]) for v in vals]
    inv = 1.0 / functools.reduce(lambda a, b: a + b, ex)
    out = jnp.zeros(lg.shape, F32)
    for k in range(TOP_K):
        out = jnp.where(lane == float(k), ex[k] * inv, out)
        out = jnp.where(lane == float(TOP_K + k), idxs[k], out)
    rt_ref[...] = out


def _outproj(ys, oa, x_parts, n_prompt_tiles, w_glu, g_ssm, w_out, g_ffn, w_router, b_router):
    t, sw = ys.shape
    d = x_parts[0].shape[1]
    n_exp = w_router.shape[1]
    lw = max(128, n_exp)
    wr = jnp.zeros((d, lw), F32).at[:, :n_exp].set(w_router)
    wrh, wrl = _split_bf16(wr)
    br = jnp.full((1, lw), NEG_INF, F32).at[0, :n_exp].set(b_router)
    w_out_bf = w_out.astype(BF16)
    tm = TOK_TILE
    row = lambda i: (i, 0)
    fixed = lambda i: (0, 0)
    return pl.pallas_call(
        functools.partial(_outproj_kernel, n_x=len(x_parts), n_prompt_tiles=n_prompt_tiles),
        grid=(t // tm,),
        in_specs=[pl.BlockSpec((tm, sw), row), pl.BlockSpec((tm, d - sw), row)]
                 + _token_specs(x_parts, tm, n_prompt_tiles)
                 + [pl.BlockSpec((sw, sw), fixed), pl.BlockSpec((1, sw), fixed),
                    pl.BlockSpec((sw, d), fixed), pl.BlockSpec((d - sw, d), fixed), pl.BlockSpec((1, d), fixed),
                    pl.BlockSpec((d, lw), fixed), pl.BlockSpec((d, lw), fixed), pl.BlockSpec((1, lw), fixed)],
        out_specs=[pl.BlockSpec((tm, d), row), pl.BlockSpec((tm * (d // 128), 128), row),
                   pl.BlockSpec((tm, lw), row)],
        out_shape=[jax.ShapeDtypeStruct((t, d), F32), jax.ShapeDtypeStruct((t * (d // 128), 128), F32),
                   jax.ShapeDtypeStruct((t, lw), F32)],
        compiler_params=_cparams(("parallel",)),
        name="outproj",
    )(ys, oa, *x_parts, w_glu.astype(BF16), g_ssm[None].astype(F32), w_out_bf[:sw], w_out_bf[sw:],
      g_ffn[None].astype(F32), wrh, wrl, br)


def _moe_kernel(te_ref, nt_ref, src_ref, nxt_ref, dprev_ref, h2_hbm, wgu_ref, bgu_ref, wd_ref, bd_ref,
                out_hbm, xbuf, obuf, wgu_bf, wd_bf, gsem, ssem):
    i = pl.program_id(0)
    nt = nt_ref[0]
    slot = i % 2
    other = 1 - slot
    tm = src_ref.shape[2]
    nblk = xbuf.shape[1] // tm

    def token_rows(ref, first_row):
        return ref.at[pl.ds(pl.multiple_of(first_row, nblk), nblk), :]

    def start_gather(idx_ref, s):
        def two(rr, _):
            for p in range(2):
                r = 2 * rr + p
                pltpu.make_async_copy(token_rows(h2_hbm, idx_ref[0, 0, r]), token_rows(xbuf.at[s], r * nblk),
                                      gsem.at[s]).start(priority=p)
            return 0
        lax.fori_loop(0, tm // 2, two, 0, unroll=4)

    def wait_gather(s):
        pltpu.make_async_copy(xbuf.at[s], xbuf.at[s], gsem.at[s]).wait()

    def start_scatter(idx_ref, s, inline=False):
        def one(r, p):
            pltpu.make_async_copy(token_rows(obuf.at[s], r * nblk), token_rows(out_hbm, idx_ref[0, 0, r]),
                                  ssem.at[0]).start(priority=p)

        if inline:
            for r in range(tm):
                one(r, r % 2)
            return

        def two(rr, _):
            for p in range(2):
                one(2 * rr + p, p)
            return 0
        lax.fori_loop(0, tm // 2, two, 0, unroll=4)

    def wait_scatter(s):
        pltpu.make_async_copy(obuf.at[s], obuf.at[s], ssem.at[0]).wait()

    @pl.when(i == 0)
    def _():
        obuf[...] = jnp.zeros(obuf.shape, obuf.dtype)
        start_gather(src_ref, 0)

    @pl.when(i < nt)
    def _():
        wait_gather(slot)

    prev = te_ref[jnp.maximum(i - 1, 0)]
    new_expert = jnp.logical_or(i == 0, te_ref[i] != prev)

    @pl.when(jnp.logical_and(i < nt, new_expert))
    def _():
        wgu_bf[...] = wgu_ref[0].astype(BF16)
        wd_bf[...] = wd_ref[0].astype(BF16)

    @pl.when(i < nt)
    def _():
        start_gather(nxt_ref, other)
        start_scatter(dprev_ref, other, inline=True)
        dff = wd_bf.shape[0]
        xin = xbuf.at[slot]
        x = jnp.concatenate([xin[pl.ds(j, tm, stride=nblk), :].astype(BF16) for j in range(nblk)], axis=1)
        gu = _dot(x, wgu_bf[...]) + bgu_ref[0]
        x_glu = jnp.minimum(gu[:, :dff], SWIGLU_LIMIT)
        x_lin = jnp.clip(gu[:, dff:], -SWIGLU_LIMIT, SWIGLU_LIMIT)
        hdn = x_glu * (1.0 / (1.0 + jnp.exp(-SWIGLU_ALPHA * x_glu))) * (x_lin + 1.0)
        out = _dot(hdn.astype(BF16), wd_bf[...]) + bd_ref[0]
        res = obuf.at[slot]
        for j in range(nblk):
            res[pl.ds(j, tm, stride=nblk), :] = out[:, j * 128:(j + 1) * 128]
        wait_scatter(other)

    @pl.when(i == nt)
    def _():
        wait_gather(slot)
        start_scatter(dprev_ref, other)
        wait_scatter(other)


def _moe_rows(h2, route, layer, w_gate_up, b_gate_up, w_down, b_down):
    tile_expert, n_used, src, dst = route
    depth, n_exp, d, dgu = w_gate_up.shape
    nblk = d // 128
    t = h2.shape[0] // nblk
    dff = w_down.shape[2]
    tm = MOE_TILE
    n_tiles = tile_expert.shape[0]
    by_e = lambda i, te, nt: (layer * n_exp + te[i], 0, 0)
    smem = lambda f: pl.BlockSpec((1, 1, tm), f, memory_space=pltpu.SMEM)
    any_ = pl.BlockSpec(memory_space=pl.ANY)
    return pl.pallas_call(
        _moe_kernel,
        grid_spec=pltpu.PrefetchScalarGridSpec(
            num_scalar_prefetch=2,
            grid=(n_tiles,),
            in_specs=[smem(lambda i, te, nt: (i, 0, 0)), smem(lambda i, te, nt: (i + 1, 0, 0)),
                      smem(lambda i, te, nt: (i, 0, 0)), any_,
                      pl.BlockSpec((1, d, dgu), by_e), pl.BlockSpec((1, 1, dgu), by_e),
                      pl.BlockSpec((1, dff, d), by_e), pl.BlockSpec((1, 1, d), by_e)],
            out_specs=any_,
            scratch_shapes=[pltpu.VMEM((2, tm * nblk, 128), F32), pltpu.VMEM((2, tm * nblk, 128), F32),
                            pltpu.VMEM((d, dgu), BF16), pltpu.VMEM((dff, d), BF16),
                            pltpu.SemaphoreType.DMA((2,)), pltpu.SemaphoreType.DMA((1,))]),
        out_shape=jax.ShapeDtypeStruct(((TOP_K * t + tm) * nblk, 128), F32),
        compiler_params=_cparams(("arbitrary",)),
        name="moe",
    )(tile_expert, n_used, src, src, dst, h2, w_gate_up.reshape(depth * n_exp, d, dgu),
      b_gate_up.reshape(depth * n_exp, 1, dgu), w_down.reshape(depth * n_exp, dff, d),
      b_down.reshape(depth * n_exp, 1, d))


def _moe_route(top_idx, n_exp, nblk):
    t = top_idx.shape[0]
    tm = MOE_TILE
    n_assign = t * TOP_K
    e_flat = top_idx.T.reshape(-1).astype(jnp.int32)
    id_bits = max(1, (n_assign - 1).bit_length())
    assert (n_exp << id_bits) < 2 ** 31
    keys = jnp.left_shift(e_flat, id_bits) | jnp.arange(n_assign, dtype=jnp.int32)
    order = jnp.sort(keys) & ((1 << id_bits) - 1)
    experts = jnp.arange(n_exp, dtype=jnp.int32)
    counts = jnp.sum((e_flat[:, None] == experts[None, :]).astype(jnp.int32), axis=0)
    starts = jnp.cumsum(counts) - counts
    tiles_e = (counts + tm - 1) // tm
    tile_end = jnp.cumsum(tiles_e)
    tile_beg = tile_end - tiles_e
    n_used = tile_end[-1]
    n_tiles = n_assign // tm + n_exp + 1
    tile = jnp.arange(n_tiles, dtype=jnp.int32)
    tile_c = jnp.minimum(tile, n_used - 1)
    te = jnp.sum((tile_end[None, :] <= tile_c[:, None]).astype(jnp.int32), axis=1)
    te = jnp.minimum(te, n_exp - 1)
    first = starts[te] + (tile - tile_beg[te]) * tm
    n_valid = jnp.where(tile < n_used, jnp.clip(counts[te] - (tile - tile_beg[te]) * tm, 0, tm), 0)
    r = jnp.arange(tm, dtype=jnp.int32)
    valid = r[None, :] < n_valid[:, None]
    a = order[jnp.clip(first[:, None] + r[None, :], 0, n_assign - 1)]
    src = jnp.where(valid, a % t, 0)
    dst = jnp.where(valid, a, n_assign + r[None, :])
    spare = jnp.broadcast_to(n_assign + r[None, :], (1, tm))
    src = jnp.concatenate([src, jnp.zeros((1, tm), jnp.int32)], axis=0)[:, None, :] * nblk
    dst = jnp.concatenate([spare, dst], axis=0)[:, None, :] * nblk
    return te.astype(jnp.int32), n_used.astype(jnp.int32)[None], src, dst


def _combine_kernel(x1_ref, g_ref, o0_ref, o1_ref, o2_ref, o3_ref, *out_refs, n_prompt_tiles):
    g = g_ref[...]
    tm, d = x1_ref.shape
    nblk = d // 128

    def emit(x2_ref):
        for j in range(nblk):
            cols = slice(j * 128, (j + 1) * 128)
            acc = x1_ref[:, cols]
            for k, o_ref in enumerate((o0_ref, o1_ref, o2_ref, o3_ref)):
                acc = acc + g[:, k:k + 1] * o_ref[pl.ds(j, tm, stride=nblk), :]
            x2_ref[:, cols] = acc

    if len(out_refs) == 1:
        emit(out_refs[0])
        return
    i = pl.program_id(0)

    @pl.when(i < n_prompt_tiles)
    def _():
        emit(out_refs[0])

    @pl.when(i >= n_prompt_tiles)
    def _():
        emit(out_refs[1])


def _combine(x1, gates, out_rows, n_prompt_tiles, split):
    t, d = x1.shape
    tm = TOK_TILE
    nt = t // tm
    rows = lambda k: pl.BlockSpec((tm * (d // 128), 128), lambda i: (k * nt + i, 0))
    if split:
        out_specs = [pl.BlockSpec((tm, d), lambda i: (jnp.minimum(i, n_prompt_tiles - 1), 0)),
                     pl.BlockSpec((tm, d), lambda i: (jnp.maximum(i - n_prompt_tiles, 0), 0))]
        out_shape = [jax.ShapeDtypeStruct((n_prompt_tiles * tm, d), F32),
                     jax.ShapeDtypeStruct(((nt - n_prompt_tiles) * tm, d), F32)]
    else:
        out_specs = pl.BlockSpec((tm, d), lambda i: (i, 0))
        out_shape = jax.ShapeDtypeStruct((t, d), F32)
    return pl.pallas_call(
        functools.partial(_combine_kernel, n_prompt_tiles=n_prompt_tiles),
        grid=(nt,),
        in_specs=[pl.BlockSpec((tm, d), lambda i: (i, 0)), pl.BlockSpec((tm, TOP_K), lambda i: (i, 0)),
                  rows(0), rows(1), rows(2), rows(3)],
        out_specs=out_specs,
        out_shape=out_shape,
        compiler_params=_cparams(("arbitrary",) if split else ("parallel",)),
        name="combine",
    )(x1, gates, out_rows, out_rows, out_rows, out_rows)


def _lambda_init(layer):
    return 0.8 - 0.6 * math.exp(-0.3 * layer)


def kernel(x_prompt, x_sample, cache_k, cache_v, state_ssm_re, state_ssm_im, g_mix, w_in, ssm_a_re, ssm_a_im, ssm_log_dt, ssm_b_re, ssm_b_im, ssm_c_re, ssm_c_im, ssm_d, w_glu, g_ssm_out, g_q, g_k, lambda_q1, lambda_k1, lambda_q2, lambda_k2, g_subln, w_out, g_ffn, w_router, b_router, w_gate_up, b_gate_up, w_down, b_down):
    batch, seq, d = x_prompt.shape
    n_streams, dec_seq, _ = x_sample.shape
    depth = w_in.shape[0]
    past_len = cache_k.shape[2]
    n_heads = cache_k.shape[3]
    aw = n_heads * HEAD_W
    n_groups, n_state = ssm_a_re.shape[1:]
    nq = n_groups // SSM_QG
    assert dec_seq == CHUNK and seq % SSM_ROWS == 0 and n_streams % 8 == 0 and n_state == SSM_STATE
    tp = batch * seq
    ts = n_streams * dec_seq
    n_prompt_sc = tp // SSM_ROWS
    sc_per_seq = seq // SSM_ROWS
    n_sample_sc = ts // SSM_ROWS
    slopes = jnp.asarray([2.0 ** (-8.0 * (h + 1) / n_heads) for h in range(n_heads)], F32)

    x_parts = (x_prompt.reshape(tp, d), x_sample.reshape(ts, d))
    outs = {name: [] for name in ("srp", "sip", "srs", "sis")}
    kv_out = None
    n_exp = w_router.shape[2]
    n_prompt_tiles = tp // TOK_TILE
    for l in range(depth):
        u, q, kb, vb, *kv_out = _inproj(x_parts, g_mix[l], w_in[l].astype(BF16), g_q[l], g_k[l], l, depth,
                                        n_prompt_tiles, kv_out)

        tables = _ssm_tables(ssm_a_re[l], ssm_a_im[l], ssm_log_dt[l], ssm_b_re[l], ssm_b_im[l],
                             ssm_c_re[l], ssm_c_im[l], ssm_d[l])

        def state_lanes(z):
            return z.reshape(n_sample_sc, 8, nq, SSM_QW).transpose(0, 2, 1, 3)

        h0_s = jnp.concatenate([state_lanes(state_ssm_re[l]), state_lanes(state_ssm_im[l])], axis=-1)
        h0_all = jnp.concatenate([jnp.zeros((n_prompt_sc,) + h0_s.shape[1:], F32), h0_s], axis=0)
        ys, fin = _ssm(u, tables, h0_all, n_prompt_sc, sc_per_seq)

        lam_init = _lambda_init(l)
        lam = (jnp.exp(jnp.sum(lambda_q1[l].astype(F32) * lambda_k1[l].astype(F32)))
               - jnp.exp(jnp.sum(lambda_q2[l].astype(F32) * lambda_k2[l].astype(F32))) + lam_init)[None]
        out_scale = 1.0 - lam_init
        bound = (SCORE_BOUND_SCALE * jnp.max(jnp.abs(g_q[l].astype(F32)))
                 * jnp.max(jnp.abs(g_k[l].astype(F32))))[None]
        oa = _attn_prompt(q, kb, vb, slopes, lam, bound, g_subln[l], out_scale, batch, seq)
        oa = _attn_sample(q, kb, vb, cache_k, cache_v, l, slopes, lam, g_subln[l], out_scale, tp, oa)

        x1, h2, routing = _outproj(ys, oa, x_parts, n_prompt_tiles, w_glu[l], g_ssm_out[l], w_out[l], g_ffn[l],
                                   w_router[l], b_router[l])
        gates = routing[:, :TOP_K]
        route = _moe_route(routing[:, TOP_K:2 * TOP_K].astype(jnp.int32), n_exp, d // 128)
        x_new = _combine(x1, gates, _moe_rows(h2, route, l, w_gate_up, b_gate_up, w_down, b_down),
                         n_prompt_tiles, split=(l == depth - 1))
        x_parts = tuple(x_new) if l == depth - 1 else (x_new,)

        fin_p = fin[:n_prompt_sc].reshape(batch, sc_per_seq, nq, 8, 2, SSM_QW)[:, -1, :, -1]
        outs["srp"].append(fin_p[:, :, 0].reshape(batch, n_groups, n_state))
        outs["sip"].append(fin_p[:, :, 1].reshape(batch, n_groups, n_state))
        fin_s = fin[n_prompt_sc:].reshape(n_sample_sc, nq, 8, 2, SSM_QW).transpose(0, 2, 3, 1, 4)
        outs["srs"].append(fin_s[:, :, 0].reshape(n_streams, n_groups, n_state))
        outs["sis"].append(fin_s[:, :, 1].reshape(n_streams, n_groups, n_state))

    st = {name: jnp.stack(vals) for name, vals in outs.items()}
    kp, vp, ks, vs = kv_out
    p_shape = (depth, batch, seq, n_heads, HEAD_W)
    s_shape = (depth, n_streams, dec_seq, n_heads, HEAD_W)
    y_prompt, y_sample = x_parts
    return (y_prompt.reshape(batch, seq, d), y_sample.reshape(n_streams, dec_seq, d),
            kp.reshape(p_shape), vp.reshape(p_shape), st["srp"], st["sip"],
            ks.reshape(s_shape), vs.reshape(s_shape), st["srs"], st["sis"])
```

```python
import functools
import math

import jax
import jax.numpy as jnp
from jax import lax
from jax.experimental import pallas as pl
from jax.experimental.pallas import tpu as pltpu

F32 = jnp.float32
BF16 = jnp.bfloat16

CHUNK = 64
HEAD_DIM = 64
HEAD_W = 2 * HEAD_DIM
SSM_GROUP = 16
SSM_STATE = 64
SSM_QG = 8
SSM_QW = SSM_QG * SSM_STATE
TOP_K = 4
SWIGLU_ALPHA = 1.702
SWIGLU_LIMIT = 7.0
EPS = 1e-6
NEG_INF = -1e30

TOK_TILE = 512
SSM_ROWS = 8 * CHUNK
ATT_TQ = 512
ATT_TKC = 2048
MOE_TILE = 256
VMEM_LIMIT = 56 * 1024 * 1024


def _cparams(sem):
    return pltpu.CompilerParams(dimension_semantics=sem, vmem_limit_bytes=VMEM_LIMIT)


def _dot(a, b):
    return jnp.dot(a, b, preferred_element_type=F32)


def _dot_t(a, b):
    return lax.dot_general(a, b, (((1,), (1,)), ((), ())), preferred_element_type=F32)


def _split_bf16(x):
    hi = x.astype(BF16)
    lo = (x - hi.astype(F32)).astype(BF16)
    return hi, lo


def _token_tile(x_refs, i, n_prompt_tiles):
    if len(x_refs) == 1:
        return x_refs[0][...]
    is_prompt = (jnp.zeros(x_refs[0].shape, jnp.int32) + (i < n_prompt_tiles).astype(jnp.int32)) > 0
    return jnp.where(is_prompt, x_refs[0][...], x_refs[1][...])


def _token_specs(x_parts, tm, n_prompt_tiles):
    d = x_parts[0].shape[1]
    if len(x_parts) == 1:
        return [pl.BlockSpec((tm, d), lambda i: (i, 0))]
    return [pl.BlockSpec((tm, d), lambda i: (jnp.minimum(i, n_prompt_tiles - 1), 0)),
            pl.BlockSpec((tm, d), lambda i: (jnp.maximum(i - n_prompt_tiles, 0), 0))]


def _inproj_kernel(*refs, n_x, n_prompt_tiles, n_prev):
    x_refs = refs[:n_x]
    g_ref, w_ref, gq_ref, gk_ref, seg_ref = refs[n_x:n_x + 5]
    u_ref, q_ref, kb_ref, vb_ref, kp_ref, vp_ref, ks_ref, vs_ref = refs[n_x + 5 + n_prev:]
    i = pl.program_id(0)
    x = _token_tile(x_refs, i, n_prompt_tiles)
    ms = jnp.mean(x * x, axis=-1, keepdims=True)
    h = (x * lax.rsqrt(ms + EPS) * g_ref[...]).astype(BF16)
    proj = _dot(h, w_ref[...])
    w = u_ref.shape[-1]
    seg = seg_ref[...]

    def head_norm(z, g):
        ms_ = _dot((z * z).astype(BF16), seg)
        return z * lax.rsqrt(ms_ + EPS) * g

    u_ref[...] = proj[:, :w]
    qn = head_norm(proj[:, w:2 * w], gq_ref[...])
    q_ref[...] = (qn * (HEAD_DIM ** -0.5)).astype(BF16)
    kn = head_norm(proj[:, 2 * w:3 * w], gk_ref[...])
    kb_ref[...] = kn.astype(BF16)
    vv = proj[:, 3 * w:]
    vb_ref[...] = vv.astype(BF16)

    def emit(k_out, v_out):
        tm = kn.shape[0]
        nh = k_out.shape[0] // tm
        for hd in range(nh):
            k_out[pl.ds(hd, tm, stride=nh), :] = kn[:, hd * HEAD_W:(hd + 1) * HEAD_W]
            v_out[pl.ds(hd, tm, stride=nh), :] = vv[:, hd * HEAD_W:(hd + 1) * HEAD_W]

    @pl.when(i < n_prompt_tiles)
    def _():
        emit(kp_ref, vp_ref)

    @pl.when(i >= n_prompt_tiles)
    def _():
        emit(ks_ref, vs_ref)


def _inproj(x_parts, g_mix, w_in_bf, g_q, g_k, layer, depth, n_prompt_tiles, prev):
    t = sum(p.shape[0] for p in x_parts)
    d = x_parts[0].shape[1]
    aw = w_in_bf.shape[1] // 4
    nh = aw // HEAD_W
    nrep = aw // HEAD_DIM
    gq = jnp.tile(g_q.astype(F32), nrep)[None]
    gk = jnp.tile(g_k.astype(F32), nrep)[None]
    ids = jnp.arange(aw) // HEAD_DIM
    seg = jnp.where(ids[:, None] == ids[None, :], 1.0 / HEAD_DIM, 0.0).astype(BF16)
    tm = TOK_TILE
    n_tiles = t // tm
    n_sample_tiles = n_tiles - n_prompt_tiles
    row = lambda i: (i, 0)
    fixed = lambda i: (0, 0)
    p_blk = pl.BlockSpec((None, tm * nh, HEAD_W),
                         lambda i: (layer * n_prompt_tiles + jnp.minimum(i, n_prompt_tiles - 1), 0, 0))
    s_blk = pl.BlockSpec((None, tm * nh, HEAD_W),
                         lambda i: (layer * n_sample_tiles + jnp.maximum(i - n_prompt_tiles, 0), 0, 0))
    p_shape = jax.ShapeDtypeStruct((depth * n_prompt_tiles, tm * nh, HEAD_W), F32)
    s_shape = jax.ShapeDtypeStruct((depth * n_sample_tiles, tm * nh, HEAD_W), F32)
    outs = [jax.ShapeDtypeStruct((t, aw), dt) for dt in (F32, BF16, BF16, BF16)] + [p_shape, p_shape, s_shape, s_shape]
    prev = () if prev is None else tuple(prev)
    n_x = len(x_parts)
    n_in = n_x + 5
    kern = functools.partial(_inproj_kernel, n_x=n_x, n_prompt_tiles=n_prompt_tiles, n_prev=len(prev))
    return pl.pallas_call(
        kern,
        grid=(n_tiles,),
        in_specs=_token_specs(x_parts, tm, n_prompt_tiles)
                 + [pl.BlockSpec((1, d), fixed),
                    pl.BlockSpec(w_in_bf.shape, fixed), pl.BlockSpec((1, aw), fixed),
                    pl.BlockSpec((1, aw), fixed), pl.BlockSpec((aw, aw), fixed)]
                 + [pl.BlockSpec(memory_space=pl.ANY)] * len(prev),
        out_specs=[pl.BlockSpec((tm, aw), row)] * 4 + [p_blk, p_blk, s_blk, s_blk],
        out_shape=outs,
        input_output_aliases={n_in + j: 4 + j for j in range(len(prev))},
        compiler_params=_cparams(("arbitrary",)),
        name="inproj",
    )(*x_parts, g_mix[None].astype(F32), w_in_bf, gq, gk, seg, *prev)


def _ssm_kernel(u_ref, wb_ref, wc_ref, ab_ref, a64_ref, pw_ref, d_ref, h0_ref,
                y_ref, fin_ref, uperm_ref, st_ref, carry_ref, *, n_prompt_sc, sc_per_seq):
    sc = pl.program_id(0)
    qb = pl.program_id(1)
    nsteps = CHUNK
    w = SSM_QW

    for t in range(nsteps):
        uperm_ref[t * 8:(t + 1) * 8, :] = u_ref[pl.ds(t, 8, stride=nsteps), :]
    up = uperm_ref[...]
    st_ref[...] = _dot(up.astype(BF16), wb_ref[0])

    ab = ab_ref[0]
    ar = jnp.broadcast_to(ab[:, :w], (8, w))
    ai = jnp.broadcast_to(ab[:, w:], (8, w))

    def scan_step(t, carry):
        hr, hi = carry
        r0 = pl.multiple_of(t * 8, 8)
        br = st_ref[pl.ds(r0, 8), :w]
        bi = st_ref[pl.ds(r0, 8), w:]
        nr = ar * hr - ai * hi + br
        ni = ar * hi + ai * hr + bi
        st_ref[pl.ds(r0, 8), :w] = nr
        st_ref[pl.ds(r0, 8), w:] = ni
        return nr, ni

    zero = jnp.zeros((8, w), F32)
    er, ei = lax.fori_loop(0, nsteps, scan_step, (zero, zero), unroll=4)

    a64 = a64_ref[0]
    a64r = a64[:, :w]
    a64i = a64[:, w:]
    is_sample = sc >= n_prompt_sc

    @pl.when(jnp.logical_or(is_sample, sc % sc_per_seq == 0))
    def _():
        carry_ref[qb] = jnp.zeros(carry_ref.shape[1:], F32)

    cin = carry_ref[qb]
    cr = cin[:, :w]
    ci = cin[:, w:]
    rows = lax.broadcasted_iota(jnp.int32, (8, w), 0)
    sr = jnp.zeros((8, w), F32)
    si = jnp.zeros((8, w), F32)
    for j in range(8):
        sr = jnp.where(rows == j, cr, sr)
        si = jnp.where(rows == j, ci, si)
        ejr = er[j:j + 1]
        eji = ei[j:j + 1]
        cr, ci = a64r * cr - a64i * ci + ejr, a64r * ci + a64i * cr + eji
    carry_ref[qb] = jnp.concatenate([cr, ci], axis=1)
    h0 = h0_ref[0, 0]
    given = (jnp.zeros((8, w), jnp.int32) + is_sample.astype(jnp.int32)) > 0
    sr = jnp.where(given, h0[:, :w], sr)
    si = jnp.where(given, h0[:, w:], si)
    fr = a64r * sr - a64i * si + er
    fi = a64r * si + a64i * sr + ei
    fin_ref[0, 0] = jnp.concatenate([fr, fi], axis=1)

    def fix_step(t, _):
        r0 = pl.multiple_of(t * 8, 8)
        p = pw_ref[0, pl.ds(t, 1), :]
        pr = p[:, :w]
        pi = p[:, w:]
        st_ref[pl.ds(r0, 8), :w] = st_ref[pl.ds(r0, 8), :w] + (pr * sr - pi * si)
        st_ref[pl.ds(r0, 8), w:] = st_ref[pl.ds(r0, 8), w:] + (pr * si + pi * sr)
        return 0

    lax.fori_loop(0, nsteps, fix_step, 0, unroll=4)

    y = _dot(st_ref[...].astype(BF16), wc_ref[0]) + up * d_ref[0]
    for t in range(nsteps):
        y_ref[pl.ds(t, 8, stride=nsteps), :] = y[t * 8:(t + 1) * 8, :]


def _ssm_tables(a_re, a_im, log_dt, b_re, b_im, c_re, c_im, d_skip):
    g, n = a_re.shape
    c = b_re.shape[-1]
    nq = g // SSM_QG
    dt = jnp.exp(log_dt)[:, None]
    za_re, za_im = dt * a_re, dt * a_im
    mag = jnp.exp(za_re)
    ab_re, ab_im = mag * jnp.cos(za_im), mag * jnp.sin(za_im)
    den = a_re * a_re + a_im * a_im
    n_re, n_im = ab_re - 1.0, ab_im
    f_re = (n_re * a_re + n_im * a_im) / den
    f_im = (n_im * a_re - n_re * a_im) / den
    bb_re = f_re[..., None] * b_re - f_im[..., None] * b_im
    bb_im = f_re[..., None] * b_im + f_im[..., None] * b_re
    eye = jnp.eye(SSM_QG, dtype=F32)

    def in_w(bb):
        bq = bb.reshape(nq, SSM_QG, n, c)
        return jnp.einsum('qgnc,gh->qgchn', bq, eye).reshape(nq, SSM_QG * c, SSM_QG * n)

    def out_w(cc):
        cq = cc.reshape(nq, SSM_QG, c, n)
        return jnp.einsum('qgcn,gh->qgnhc', cq, eye).reshape(nq, SSM_QG * n, SSM_QG * c)

    wb = jnp.concatenate([in_w(bb_re), in_w(bb_im)], axis=2).astype(BF16)
    wc = jnp.concatenate([out_w(c_re), out_w(-c_im)], axis=1).astype(BF16)

    def lanes(z):
        return jnp.moveaxis(z.reshape(z.shape[:-2] + (nq, SSM_QG * n)), -2, 0)

    def power(k):
        m = jnp.exp(k * za_re)
        return m * jnp.cos(k * za_im), m * jnp.sin(k * za_im)

    ab = jnp.concatenate([lanes(ab_re), lanes(ab_im)], axis=-1)[:, None]
    p64 = power(float(CHUNK))
    a64 = jnp.concatenate([lanes(p64[0]), lanes(p64[1])], axis=-1)[:, None]
    ks = jnp.arange(1, CHUNK + 1, dtype=F32)[:, None, None]
    pk = power(ks)
    pw = jnp.concatenate([lanes(pk[0]), lanes(pk[1])], axis=-1)
    dq = d_skip.reshape(nq, 1, SSM_QG * c).astype(F32)
    return wb, wc, ab, a64, pw, dq


def _ssm(u, tables, h0_all, n_prompt_sc, sc_per_seq):
    wb, wc, ab, a64, pw, dq = tables
    t, cw = u.shape
    nq = wb.shape[0]
    n_sc = t // SSM_ROWS
    sw = 2 * SSM_QW
    kern = functools.partial(_ssm_kernel, n_prompt_sc=n_prompt_sc, sc_per_seq=sc_per_seq)
    per_q = lambda s, q: (q, 0, 0)
    return pl.pallas_call(
        kern,
        grid=(n_sc, nq),
        in_specs=[pl.BlockSpec((SSM_ROWS, 128), lambda s, q: (s, q)),
                  pl.BlockSpec((1,) + wb.shape[1:], per_q), pl.BlockSpec((1,) + wc.shape[1:], per_q),
                  pl.BlockSpec((1, 1, sw), per_q), pl.BlockSpec((1, 1, sw), per_q),
                  pl.BlockSpec((1, CHUNK, sw), per_q), pl.BlockSpec((1, 1, 128), per_q),
                  pl.BlockSpec((1, 1, 8, sw), lambda s, q: (s, q, 0, 0))],
        out_specs=[pl.BlockSpec((SSM_ROWS, 128), lambda s, q: (s, q)),
                   pl.BlockSpec((1, 1, 8, sw), lambda s, q: (s, q, 0, 0))],
        out_shape=[jax.ShapeDtypeStruct((t, cw), F32),
                   jax.ShapeDtypeStruct((n_sc, nq, 8, sw), F32)],
        scratch_shapes=[pltpu.VMEM((SSM_ROWS, 128), F32), pltpu.VMEM((SSM_ROWS, sw), F32),
                        pltpu.VMEM((nq, 1, sw), F32)],
        compiler_params=_cparams(("arbitrary", "arbitrary")),
        name="ssm",
    )(u, wb, wc, ab, a64, pw, dq, h0_all)


def _stack_q(q):
    lane = lax.broadcasted_iota(jnp.int32, q.shape, 1)
    zero = jnp.zeros_like(q)
    return jnp.concatenate([jnp.where(lane < HEAD_DIM, q, zero), jnp.where(lane >= HEAD_DIM, q, zero)], axis=0)


def _online_update(s, v, m_ref, l_ref, acc_ref, fixed_max=False):
    if fixed_max:
        acc_ref[...] = acc_ref[...] + _dot(jnp.exp(s).astype(BF16), v)
        return
    m_old = m_ref[...]
    m_new = jnp.maximum(m_old, jnp.max(s, axis=-1, keepdims=True))
    alpha = jnp.exp(m_old - m_new)
    p = jnp.exp(s - m_new)
    if acc_ref.shape[1] == HEAD_W:
        l_ref[...] = alpha * l_ref[...] + jnp.sum(p, axis=-1, keepdims=True)
    acc_ref[...] = alpha * acc_ref[...] + _dot(p.astype(BF16), v)
    m_ref[...] = m_new


SCORE_BOUND_SCALE = 1.02 * HEAD_DIM ** 0.5
FIXED_MAX_LIMIT = 30.0


def _finish_head(m_ref, l_ref, acc_ref, lam, g, out_scale, tq):
    acc = acc_ref[...]
    if acc.shape[1] > HEAD_W:
        l = acc[:, HEAD_W:HEAD_W + 1]
        acc = acc[:, :HEAD_W]
    else:
        l = l_ref[...]
    o = acc[:tq] / l[:tq] - lam * (acc[tq:] / l[tq:])
    ms = jnp.mean(o * o, axis=-1, keepdims=True)
    return o * lax.rsqrt(ms + EPS) * g * out_scale


def _attn_prompt_kernel(slope_ref, lam_ref, bound_ref, q_ref, k_ref, v_ref, bias_ref, g_ref, o_ref,
                        m_ref, l_ref, acc_ref, *, out_scale, fixed_max):
    h = pl.program_id(1)
    i = pl.program_id(2)
    tq = q_ref.shape[0]
    tk = tq
    hk = tk // 2
    slope = slope_ref[h]
    shift = bound_ref[0] if fixed_max else 0.0
    qq = _stack_q(q_ref[...])
    if not fixed_max:
        m_ref[...] = jnp.full(m_ref.shape, NEG_INF, F32)
        l_ref[...] = jnp.zeros(l_ref.shape, F32)
    acc_ref[...] = jnp.zeros(acc_ref.shape, F32)
    lane = lax.broadcasted_iota(jnp.int32, (hk, HEAD_W), 1)
    ones_col = jnp.where(lane == 0, 1.0, 0.0).astype(BF16)

    def key_tile(j):
        which = jnp.where(j == i, 1, 0)
        offset = slope * ((i - j) * tq).astype(F32) + shift
        for half in range(2):
            k0 = pl.multiple_of(j * tk + half * hk, hk)
            tile = bias_ref[which, :, half * hk:(half + 1) * hk] - offset
            s = _dot_t(qq, k_ref[pl.ds(k0, hk), :]) + jnp.concatenate([tile, tile], axis=0)
            v = v_ref[pl.ds(k0, hk), :]
            if fixed_max:
                v = jnp.concatenate([v, ones_col], axis=1)
            _online_update(s, v, m_ref, l_ref, acc_ref, fixed_max)

    def two_tiles(jj, _):
        key_tile(2 * jj)
        key_tile(2 * jj + 1)
        return 0

    n_tiles = i + 1
    lax.fori_loop(0, n_tiles >> 1, two_tiles, 0)

    @pl.when((n_tiles & 1) == 1)
    def _():
        key_tile(i)

    o_ref[...] = _finish_head(m_ref, l_ref, acc_ref, lam_ref[0], g_ref[...], out_scale, tq).astype(o_ref.dtype)


def _attn_prompt(q, kb, vb, slopes, lam, bound, g_subln, out_scale, batch, seq):
    n_heads = q.shape[1] // HEAD_W
    tq = ATT_TQ
    nq = seq // tq
    smem = pl.BlockSpec(memory_space=pltpu.SMEM)
    pos = jnp.arange(tq, dtype=jnp.int32)
    rel = (pos[:, None] - pos[None, :]).astype(F32)
    visible = (pos[None, :] // CHUNK) <= (pos[:, None] // CHUNK)
    sl = slopes[:, None, None]
    bias = jnp.stack([-sl * rel[None], jnp.where(visible[None], -sl * jnp.abs(rel)[None], NEG_INF)], axis=1)

    def call(fixed_max):
        return pl.pallas_call(
            functools.partial(_attn_prompt_kernel, out_scale=out_scale, fixed_max=fixed_max),
            grid=(batch, n_heads, nq),
            in_specs=[smem, smem, smem,
                      pl.BlockSpec((tq, HEAD_W), lambda b, h, i: (b * nq + i, h)),
                      pl.BlockSpec((seq, HEAD_W), lambda b, h, i: (b, h)),
                      pl.BlockSpec((seq, HEAD_W), lambda b, h, i: (b, h)),
                      pl.BlockSpec((None, 2, tq, tq), lambda b, h, i: (h, 0, 0, 0)),
                      pl.BlockSpec((1, HEAD_W), lambda b, h, i: (0, 0))],
            out_specs=pl.BlockSpec((tq, HEAD_W), lambda b, h, i: (b * nq + i, h)),
            out_shape=jax.ShapeDtypeStruct(q.shape, BF16),
            scratch_shapes=[pltpu.VMEM((2 * tq, 1), F32), pltpu.VMEM((2 * tq, 1), F32),
                            pltpu.VMEM((2 * tq, 2 * HEAD_W if fixed_max else HEAD_W), F32)],
            compiler_params=_cparams(("parallel", "parallel", "arbitrary")),
            name="attn_prompt_fixed" if fixed_max else "attn_prompt",
        )(slopes, lam, bound, q, kb, vb, bias, g_subln[None].astype(F32))

    return lax.cond(bound[0] <= FIXED_MAX_LIMIT, lambda: call(True), lambda: call(False))


def _attn_sample_kernel(slope_ref, lam_ref, q_ref, kn_ref, vn_ref, kc_ref, vc_ref, g_ref, shared_ref, o_ref,
                        m_ref, l_ref, acc_ref, *, out_scale, past_len):
    del shared_ref
    j = pl.program_id(1)
    nj = pl.num_programs(1)
    tq = q_ref.shape[0]
    n_heads = q_ref.shape[1] // HEAD_W
    tk = kc_ref.shape[0] // n_heads

    @pl.when(j == 0)
    def _():
        m_ref[...] = jnp.full(m_ref.shape, NEG_INF, F32)
        l_ref[...] = jnp.zeros(l_ref.shape, F32)
        acc_ref[...] = jnp.zeros(acc_ref.shape, F32)

    def with_ones(v):
        lane = lax.broadcasted_iota(jnp.int32, v.shape, 1)
        return jnp.concatenate([v, jnp.where(lane == 0, 1.0, 0.0).astype(BF16)], axis=1)

    rq = lax.broadcasted_iota(jnp.int32, (2 * tq, 1), 0)
    rq = jnp.where(rq >= tq, rq - tq, rq)
    qpos = (past_len + rq).astype(F32)
    kpos = (j * tk + lax.broadcasted_iota(jnp.int32, (1, tk), 1)).astype(F32)
    for h in range(n_heads):
        cols = slice(h * HEAD_W, (h + 1) * HEAD_W)
        qq = _stack_q(q_ref[:, cols])
        k = kc_ref[pl.ds(h, tk, stride=n_heads), :].astype(BF16)
        v = with_ones(vc_ref[pl.ds(h, tk, stride=n_heads), :].astype(BF16))
        slope = slope_ref[h]
        s = (_dot_t(qq, k) + slope * kpos) - slope * qpos
        _online_update(s, v, m_ref.at[h], l_ref.at[h], acc_ref.at[h])

    @pl.when(j == nj - 1)
    def _():
        r = lax.broadcasted_iota(jnp.int32, (2 * tq, tq), 0)
        r = jnp.where(r >= tq, r - tq, r)
        c = lax.broadcasted_iota(jnp.int32, (2 * tq, tq), 1)
        dist = jnp.abs(r - c).astype(F32)
        for h in range(n_heads):
            cols = slice(h * HEAD_W, (h + 1) * HEAD_W)
            qq = _stack_q(q_ref[:, cols])
            s = _dot_t(qq, kn_ref[:, cols]) - slope_ref[h] * dist
            _online_update(s, with_ones(vn_ref[:, cols]), m_ref.at[h], l_ref.at[h], acc_ref.at[h])
            o_ref[:, cols] = _finish_head(m_ref.at[h], l_ref.at[h], acc_ref.at[h], lam_ref[0],
                                          g_ref[...], out_scale, tq).astype(o_ref.dtype)


def _attn_sample(q, kb, vb, cache_k, cache_v, layer, slopes, lam, g_subln, out_scale, row0, o_shared):
    depth, n_streams, past_len, n_heads, _ = cache_k.shape
    aw = n_heads * HEAD_W
    tq = CHUNK
    tk = min(ATT_TKC, past_len)
    blk0 = row0 // tq
    smem = pl.BlockSpec(memory_space=pltpu.SMEM)
    new = pl.BlockSpec((tq, aw), lambda s, j: (blk0 + s, 0))
    past = pl.BlockSpec((None, None, tk * n_heads, HEAD_W), lambda s, j: (layer, s, j, 0))
    ck = cache_k.reshape(depth, n_streams, past_len * n_heads, HEAD_W)
    cv = cache_v.reshape(depth, n_streams, past_len * n_heads, HEAD_W)

    return pl.pallas_call(
        functools.partial(_attn_sample_kernel, out_scale=out_scale, past_len=past_len),
        grid=(n_streams, past_len // tk),
        in_specs=[smem, smem, new, new, new, past, past,
                  pl.BlockSpec((1, HEAD_W), lambda s, j: (0, 0)), pl.BlockSpec(memory_space=pl.ANY)],
        out_specs=pl.BlockSpec((tq, aw), lambda s, j: (blk0 + s, 0)),
        out_shape=jax.ShapeDtypeStruct(o_shared.shape, BF16),
        input_output_aliases={8: 0},
        scratch_shapes=[pltpu.VMEM((n_heads, 2 * tq, 1), F32), pltpu.VMEM((n_heads, 2 * tq, 1), F32),
                        pltpu.VMEM((n_heads, 2 * tq, 2 * HEAD_W), F32)],
        compiler_params=_cparams(("parallel", "arbitrary")),
        name="attn_sample",
    )(slopes, lam, q, kb, vb, ck, cv, g_subln[None].astype(F32), o_shared)


def _outproj_kernel(ys_ref, oa_ref, *refs, n_x, n_prompt_tiles):
    x_refs = refs[:n_x]
    wglu_ref, gs_ref, wtop_ref, wbot_ref, gf_ref, wrh_ref, wrl_ref, br_ref, x1_ref, h2_ref, rt_ref = refs[n_x:]
    y = ys_ref[...]
    y = 0.5 * y * (1.0 + jnp.tanh(math.sqrt(2.0 / math.pi) * (y + 0.044715 * (y * y * y))))
    z = _dot(y.astype(BF16), wglu_ref[...])
    y = y * (1.0 / (1.0 + jnp.exp(-z)))
    ms = jnp.mean(y * y, axis=-1, keepdims=True)
    y = y * lax.rsqrt(ms + EPS) * gs_ref[...]
    x = _token_tile(x_refs, pl.program_id(0), n_prompt_tiles)
    x1 = x + _dot(y.astype(BF16), wtop_ref[...]) + _dot(oa_ref[...], wbot_ref[...])
    x1_ref[...] = x1
    ms = jnp.mean(x1 * x1, axis=-1, keepdims=True)
    h2 = x1 * lax.rsqrt(ms + EPS) * gf_ref[...]
    tm, d = h2.shape
    nblk = d // 128
    for j in range(nblk):
        h2_ref[pl.ds(j, tm, stride=nblk), :] = h2[:, j * 128:(j + 1) * 128]
    hi, lo = _split_bf16(h2)
    wrh = wrh_ref[...]
    lg = _dot(hi, wrh) + _dot(lo, wrh) + _dot(hi, wrl_ref[...]) + br_ref[...]
    lane = lax.broadcasted_iota(jnp.int32, lg.shape, 1).astype(F32)
    vals, idxs = [], []
    for _ in range(TOP_K):
        m = jnp.max(lg, axis=-1, keepdims=True)
        idx = jnp.min(jnp.where(lg == m, lane, float(lg.shape[1])), axis=-1, keepdims=True)
        vals.append(m)
        idxs.append(idx)
        lg = jnp.where(lane == idx, -3.0e38, lg)
    ex = [jnp.exp(v - vals[0]) for v in vals]
    inv = 1.0 / functools.reduce(lambda a, b: a + b, ex)
    out = jnp.zeros(lg.shape, F32)
    for k in range(TOP_K):
        out = jnp.where(lane == float(k), ex[k] * inv, out)
        out = jnp.where(lane == float(TOP_K + k), idxs[k], out)
    rt_ref[...] = out


def _outproj(ys, oa, x_parts, n_prompt_tiles, w_glu, g_ssm, w_out, g_ffn, w_router, b_router):
    t, sw = ys.shape
    d = x_parts[0].shape[1]
    n_exp = w_router.shape[1]
    lw = max(128, n_exp)
    wr = jnp.zeros((d, lw), F32).at[:, :n_exp].set(w_router)
    wrh, wrl = _split_bf16(wr)
    br = jnp.full((1, lw), NEG_INF, F32).at[0, :n_exp].set(b_router)
    w_out_bf = w_out.astype(BF16)
    tm = TOK_TILE
    row = lambda i: (i, 0)
    fixed = lambda i: (0, 0)
    return pl.pallas_call(
        functools.partial(_outproj_kernel, n_x=len(x_parts), n_prompt_tiles=n_prompt_tiles),
        grid=(t // tm,),
        in_specs=[pl.BlockSpec((tm, sw), row), pl.BlockSpec((tm, d - sw), row)]
                 + _token_specs(x_parts, tm, n_prompt_tiles)
                 + [pl.BlockSpec((sw, sw), fixed), pl.BlockSpec((1, sw), fixed),
                    pl.BlockSpec((sw, d), fixed), pl.BlockSpec((d - sw, d), fixed), pl.BlockSpec((1, d), fixed),
                    pl.BlockSpec((d, lw), fixed), pl.BlockSpec((d, lw), fixed), pl.BlockSpec((1, lw), fixed)],
        out_specs=[pl.BlockSpec((tm, d), row), pl.BlockSpec((tm * (d // 128), 128), row),
                   pl.BlockSpec((tm, lw), row)],
        out_shape=[jax.ShapeDtypeStruct((t, d), F32), jax.ShapeDtypeStruct((t * (d // 128), 128), F32),
                   jax.ShapeDtypeStruct((t, lw), F32)],
        compiler_params=_cparams(("parallel",)),
        name="outproj",
    )(ys, oa, *x_parts, w_glu.astype(BF16), g_ssm[None].astype(F32), w_out_bf[:sw], w_out_bf[sw:],
      g_ffn[None].astype(F32), wrh, wrl, br)


def _moe_kernel(te_ref, nt_ref, src_ref, nxt_ref, dprev_ref, h2_hbm, wgu_ref, bgu_ref, wd_ref, bd_ref,
                out_hbm, xbuf, obuf, wgu_bf, wd_bf, gsem, ssem):
    i = pl.program_id(0)
    nt = nt_ref[0]
    slot = i % 2
    other = 1 - slot
    tm = src_ref.shape[2]
    nblk = xbuf.shape[1] // tm

    def token_rows(ref, first_row):
        return ref.at[pl.ds(pl.multiple_of(first_row, nblk), nblk), :]

    def start_gather(idx_ref, s):
        def two(rr, _):
            for p in range(2):
                r = 2 * rr + p
                pltpu.make_async_copy(token_rows(h2_hbm, idx_ref[0, 0, r]), token_rows(xbuf.at[s], r * nblk),
                                      gsem.at[s]).start(priority=p)
            return 0
        lax.fori_loop(0, tm // 2, two, 0, unroll=4)

    def wait_gather(s):
        pltpu.make_async_copy(xbuf.at[s], xbuf.at[s], gsem.at[s]).wait()

    def start_scatter(idx_ref, s, inline=False):
        def one(r, p):
            pltpu.make_async_copy(token_rows(obuf.at[s], r * nblk), token_rows(out_hbm, idx_ref[0, 0, r]),
                                  ssem.at[0]).start(priority=p)

        if inline:
            for r in range(tm):
                one(r, r % 2)
            return

        def two(rr, _):
            for p in range(2):
                one(2 * rr + p, p)
            return 0
        lax.fori_loop(0, tm // 2, two, 0, unroll=4)

    def wait_scatter(s):
        pltpu.make_async_copy(obuf.at[s], obuf.at[s], ssem.at[0]).wait()

    @pl.when(i == 0)
    def _():
        obuf[...] = jnp.zeros(obuf.shape, obuf.dtype)
        start_gather(src_ref, 0)

    @pl.when(i < nt)
    def _():
        wait_gather(slot)

    prev = te_ref[jnp.maximum(i - 1, 0)]
    new_expert = jnp.logical_or(i == 0, te_ref[i] != prev)

    @pl.when(jnp.logical_and(i < nt, new_expert))
    def _():
        wgu_bf[...] = wgu_ref[0].astype(BF16)
        wd_bf[...] = wd_ref[0].astype(BF16)

    @pl.when(i < nt)
    def _():
        start_gather(nxt_ref, other)
        start_scatter(dprev_ref, other, inline=True)
        dff = wd_bf.shape[0]
        xin = xbuf.at[slot]
        x = jnp.concatenate([xin[pl.ds(j, tm, stride=nblk), :].astype(BF16) for j in range(nblk)], axis=1)
        gu = _dot(x, wgu_bf[...]) + bgu_ref[0]
        x_glu = jnp.minimum(gu[:, :dff], SWIGLU_LIMIT)
        x_lin = jnp.clip(gu[:, dff:], -SWIGLU_LIMIT, SWIGLU_LIMIT)
        hdn = x_glu * (1.0 / (1.0 + jnp.exp(-SWIGLU_ALPHA * x_glu))) * (x_lin + 1.0)
        out = _dot(hdn.astype(BF16), wd_bf[...]) + bd_ref[0]
        res = obuf.at[slot]
        for j in range(nblk):
            res[pl.ds(j, tm, stride=nblk), :] = out[:, j * 128:(j + 1) * 128]
        wait_scatter(other)

    @pl.when(i == nt)
    def _():
        wait_gather(slot)
        start_scatter(dprev_ref, other)
        wait_scatter(other)


def _moe_rows(h2, route, layer, w_gate_up, b_gate_up, w_down, b_down):
    tile_expert, n_used, src, dst = route
    depth, n_exp, d, dgu = w_gate_up.shape
    nblk = d // 128
    t = h2.shape[0] // nblk
    dff = w_down.shape[2]
    tm = MOE_TILE
    n_tiles = tile_expert.shape[0]
    by_e = lambda i, te, nt: (layer * n_exp + te[i], 0, 0)
    smem = lambda f: pl.BlockSpec((1, 1, tm), f, memory_space=pltpu.SMEM)
    any_ = pl.BlockSpec(memory_space=pl.ANY)
    return pl.pallas_call(
        _moe_kernel,
        grid_spec=pltpu.PrefetchScalarGridSpec(
            num_scalar_prefetch=2,
            grid=(n_tiles,),
            in_specs=[smem(lambda i, te, nt: (i, 0, 0)), smem(lambda i, te, nt: (i + 1, 0, 0)),
                      smem(lambda i, te, nt: (i, 0, 0)), any_,
                      pl.BlockSpec((1, d, dgu), by_e), pl.BlockSpec((1, 1, dgu), by_e),
                      pl.BlockSpec((1, dff, d), by_e), pl.BlockSpec((1, 1, d), by_e)],
            out_specs=any_,
            scratch_shapes=[pltpu.VMEM((2, tm * nblk, 128), F32), pltpu.VMEM((2, tm * nblk, 128), F32),
                            pltpu.VMEM((d, dgu), BF16), pltpu.VMEM((dff, d), BF16),
                            pltpu.SemaphoreType.DMA((2,)), pltpu.SemaphoreType.DMA((1,))]),
        out_shape=jax.ShapeDtypeStruct(((TOP_K * t + tm) * nblk, 128), F32),
        compiler_params=_cparams(("arbitrary",)),
        name="moe",
    )(tile_expert, n_used, src, src, dst, h2, w_gate_up.reshape(depth * n_exp, d, dgu),
      b_gate_up.reshape(depth * n_exp, 1, dgu), w_down.reshape(depth * n_exp, dff, d),
      b_down.reshape(depth * n_exp, 1, d))


def _moe_route(top_idx, n_exp, nblk):
    t = top_idx.shape[0]
    tm = MOE_TILE
    n_assign = t * TOP_K
    e_flat = top_idx.T.reshape(-1).astype(jnp.int32)
    order = jnp.argsort(e_flat).astype(jnp.int32)
    experts = jnp.arange(n_exp, dtype=jnp.int32)
    counts = jnp.sum((e_flat[:, None] == experts[None, :]).astype(jnp.int32), axis=0)
    starts = jnp.cumsum(counts) - counts
    tiles_e = (counts + tm - 1) // tm
    tile_end = jnp.cumsum(tiles_e)
    tile_beg = tile_end - tiles_e
    n_used = tile_end[-1]
    n_tiles = n_assign // tm + n_exp + 1
    tile = jnp.arange(n_tiles, dtype=jnp.int32)
    tile_c = jnp.minimum(tile, n_used - 1)
    te = jnp.sum((tile_end[None, :] <= tile_c[:, None]).astype(jnp.int32), axis=1)
    te = jnp.minimum(te, n_exp - 1)
    first = starts[te] + (tile - tile_beg[te]) * tm
    n_valid = jnp.where(tile < n_used, jnp.clip(counts[te] - (tile - tile_beg[te]) * tm, 0, tm), 0)
    r = jnp.arange(tm, dtype=jnp.int32)
    valid = r[None, :] < n_valid[:, None]
    a = order[jnp.clip(first[:, None] + r[None, :], 0, n_assign - 1)]
    src = jnp.where(valid, a % t, 0)
    dst = jnp.where(valid, a, n_assign + r[None, :])
    spare = jnp.broadcast_to(n_assign + r[None, :], (1, tm))
    src = jnp.concatenate([src, jnp.zeros((1, tm), jnp.int32)], axis=0)[:, None, :] * nblk
    dst = jnp.concatenate([spare, dst], axis=0)[:, None, :] * nblk
    return te.astype(jnp.int32), n_used.astype(jnp.int32)[None], src, dst


def _combine_kernel(x1_ref, g_ref, o0_ref, o1_ref, o2_ref, o3_ref, *out_refs, n_prompt_tiles):
    g = g_ref[...]
    tm, d = x1_ref.shape
    nblk = d // 128

    def emit(x2_ref):
        for j in range(nblk):
            cols = slice(j * 128, (j + 1) * 128)
            acc = x1_ref[:, cols]
            for k, o_ref in enumerate((o0_ref, o1_ref, o2_ref, o3_ref)):
                acc = acc + g[:, k:k + 1] * o_ref[pl.ds(j, tm, stride=nblk), :]
            x2_ref[:, cols] = acc

    if len(out_refs) == 1:
        emit(out_refs[0])
        return
    i = pl.program_id(0)

    @pl.when(i < n_prompt_tiles)
    def _():
        emit(out_refs[0])

    @pl.when(i >= n_prompt_tiles)
    def _():
        emit(out_refs[1])


def _combine(x1, gates, out_rows, n_prompt_tiles, split):
    t, d = x1.shape
    tm = TOK_TILE
    nt = t // tm
    rows = lambda k: pl.BlockSpec((tm * (d // 128), 128), lambda i: (k * nt + i, 0))
    if split:
        out_specs = [pl.BlockSpec((tm, d), lambda i: (jnp.minimum(i, n_prompt_tiles - 1), 0)),
                     pl.BlockSpec((tm, d), lambda i: (jnp.maximum(i - n_prompt_tiles, 0), 0))]
        out_shape = [jax.ShapeDtypeStruct((n_prompt_tiles * tm, d), F32),
                     jax.ShapeDtypeStruct(((nt - n_prompt_tiles) * tm, d), F32)]
    else:
        out_specs = pl.BlockSpec((tm, d), lambda i: (i, 0))
        out_shape = jax.ShapeDtypeStruct((t, d), F32)
    return pl.pallas_call(
        functools.partial(_combine_kernel, n_prompt_tiles=n_prompt_tiles),
        grid=(nt,),
        in_specs=[pl.BlockSpec((tm, d), lambda i: (i, 0)), pl.BlockSpec((tm, TOP_K), lambda i: (i, 0)),
                  rows(0), rows(1), rows(2), rows(3)],
        out_specs=out_specs,
        out_shape=out_shape,
        compiler_params=_cparams(("arbitrary",) if split else ("parallel",)),
        name="combine",
    )(x1, gates, out_rows, out_rows, out_rows, out_rows)


def _lambda_init(layer):
    return 0.8 - 0.6 * math.exp(-0.3 * layer)


def kernel(x_prompt, x_sample, cache_k, cache_v, state_ssm_re, state_ssm_im, g_mix, w_in, ssm_a_re, ssm_a_im, ssm_log_dt, ssm_b_re, ssm_b_im, ssm_c_re, ssm_c_im, ssm_d, w_glu, g_ssm_out, g_q, g_k, lambda_q1, lambda_k1, lambda_q2, lambda_k2, g_subln, w_out, g_ffn, w_router, b_router, w_gate_up, b_gate_up, w_down, b_down):
    batch, seq, d = x_prompt.shape
    n_streams, dec_seq, _ = x_sample.shape
    depth = w_in.shape[0]
    past_len = cache_k.shape[2]
    n_heads = cache_k.shape[3]
    aw = n_heads * HEAD_W
    n_groups, n_state = ssm_a_re.shape[1:]
    nq = n_groups // SSM_QG
    assert dec_seq == CHUNK and seq % SSM_ROWS == 0 and n_streams % 8 == 0 and n_state == SSM_STATE
    tp = batch * seq
    ts = n_streams * dec_seq
    n_prompt_sc = tp // SSM_ROWS
    sc_per_seq = seq // SSM_ROWS
    n_sample_sc = ts // SSM_ROWS
    slopes = jnp.asarray([2.0 ** (-8.0 * (h + 1) / n_heads) for h in range(n_heads)], F32)

    x_parts = (x_prompt.reshape(tp, d), x_sample.reshape(ts, d))
    outs = {name: [] for name in ("srp", "sip", "srs", "sis")}
    kv_out = None
    n_exp = w_router.shape[2]
    n_prompt_tiles = tp // TOK_TILE
    for l in range(depth):
        u, q, kb, vb, *kv_out = _inproj(x_parts, g_mix[l], w_in[l].astype(BF16), g_q[l], g_k[l], l, depth,
                                        n_prompt_tiles, kv_out)

        tables = _ssm_tables(ssm_a_re[l], ssm_a_im[l], ssm_log_dt[l], ssm_b_re[l], ssm_b_im[l],
                             ssm_c_re[l], ssm_c_im[l], ssm_d[l])

        def state_lanes(z):
            return z.reshape(n_sample_sc, 8, nq, SSM_QW).transpose(0, 2, 1, 3)

        h0_s = jnp.concatenate([state_lanes(state_ssm_re[l]), state_lanes(state_ssm_im[l])], axis=-1)
        h0_all = jnp.concatenate([jnp.zeros((n_prompt_sc,) + h0_s.shape[1:], F32), h0_s], axis=0)
        ys, fin = _ssm(u, tables, h0_all, n_prompt_sc, sc_per_seq)

        lam_init = _lambda_init(l)
        lam = (jnp.exp(jnp.sum(lambda_q1[l].astype(F32) * lambda_k1[l].astype(F32)))
               - jnp.exp(jnp.sum(lambda_q2[l].astype(F32) * lambda_k2[l].astype(F32))) + lam_init)[None]
        out_scale = 1.0 - lam_init
        bound = (SCORE_BOUND_SCALE * jnp.max(jnp.abs(g_q[l].astype(F32)))
                 * jnp.max(jnp.abs(g_k[l].astype(F32))))[None]
        oa = _attn_prompt(q, kb, vb, slopes, lam, bound, g_subln[l], out_scale, batch, seq)
        oa = _attn_sample(q, kb, vb, cache_k, cache_v, l, slopes, lam, g_subln[l], out_scale, tp, oa)

        x1, h2, routing = _outproj(ys, oa, x_parts, n_prompt_tiles, w_glu[l], g_ssm_out[l], w_out[l], g_ffn[l],
                                   w_router[l], b_router[l])
        gates = routing[:, :TOP_K]
        route = _moe_route(routing[:, TOP_K:2 * TOP_K].astype(jnp.int32), n_exp, d // 128)
        x_new = _combine(x1, gates, _moe_rows(h2, route, l, w_gate_up, b_gate_up, w_down, b_down),
                         n_prompt_tiles, split=(l == depth - 1))
        x_parts = tuple(x_new) if l == depth - 1 else (x_new,)

        fin_p = fin[:n_prompt_sc].reshape(batch, sc_per_seq, nq, 8, 2, SSM_QW)[:, -1, :, -1]
        outs["srp"].append(fin_p[:, :, 0].reshape(batch, n_groups, n_state))
        outs["sip"].append(fin_p[:, :, 1].reshape(batch, n_groups, n_state))
        fin_s = fin[n_prompt_sc:].reshape(n_sample_sc, nq, 8, 2, SSM_QW).transpose(0, 2, 3, 1, 4)
        outs["srs"].append(fin_s[:, :, 0].reshape(n_streams, n_groups, n_state))
        outs["sis"].append(fin_s[:, :, 1].reshape(n_streams, n_groups, n_state))

    st = {name: jnp.stack(vals) for name, vals in outs.items()}
    kp, vp, ks, vs = kv_out
    p_shape = (depth, batch, seq, n_heads, HEAD_W)
    s_shape = (depth, n_streams, dec_seq, n_heads, HEAD_W)
    y_prompt, y_sample = x_parts
    return (y_prompt.reshape(batch, seq, d), y_sample.reshape(n_streams, dec_seq, d),
            kp.reshape(p_shape), vp.reshape(p_shape), st["srp"], st["sip"],
            ks.reshape(s_shape), vs.reshape(s_shape), st["srs"], st["sis"])
```
